```python
import math
import jax
import jax.numpy as jnp
from jax import lax

D_MODEL = 1024
BATCH = 8
SEQ = 4096
DEPTH = 2

MEM_LEN = 256
EPS = 1e-6
NEG_INF = -1e30
FORCE_SCORE = 1e4

POOL_WIDTH = D_MODEL // 2
POOL_WINDOWS = (2, 4, 8, 16)
POOL_GROUPS = len(POOL_WINDOWS)
POOL_GROUP_DIM = POOL_WIDTH // POOL_GROUPS
DN_WIDTH = D_MODEL - POOL_WIDTH
DN_HEAD_DIM = 128
DN_HEADS = DN_WIDTH // DN_HEAD_DIM
DN_CONV = 4
DN_CHUNK = 64
A_SIZES = (POOL_WIDTH, DN_WIDTH, DN_WIDTH, DN_WIDTH, DN_WIDTH, DN_HEADS, DN_HEADS)
IN_A_COLS = POOL_WIDTH + 4 * DN_WIDTH + 2 * DN_HEADS

NSA_HEAD_DIM = 64
NSA_HEADS = D_MODEL // NSA_HEAD_DIM
NSA_GROUPS = 4
NSA_REP = NSA_HEADS // NSA_GROUPS
NSA_KV = NSA_GROUPS * NSA_HEAD_DIM
CMP_LEN = 32
CMP_STRIDE = 16
CMP_HIDDEN = 2 * NSA_HEAD_DIM
SLC_LEN = 64
SLC_TOP = 16
WINDOW = 512
NSA_QBLOCK = 32
C_SIZES = (D_MODEL, NSA_KV, NSA_KV, NSA_KV, NSA_KV, NSA_KV, NSA_KV, 3 * NSA_HEADS)
IN_C_COLS = D_MODEL + 6 * NSA_KV + 3 * NSA_HEADS

XA_HEADS = 4
XA_HEAD_DIM = D_MODEL // XA_HEADS
FF_DIM = 4 * D_MODEL

N_EVEN = (DEPTH + 1) // 2
N_ODD = DEPTH // 2

kernel_name = 'hybrid_pool_deltanet_nsa_trunk'


def rmsnorm(x, g):
    xf = x.astype(jnp.float32)
    y = xf * lax.rsqrt(jnp.mean(xf * xf, axis=-1, keepdims=True) + EPS)
    return (y * g.astype(jnp.float32)).astype(x.dtype)


def l2norm(x):
    return x * lax.rsqrt(jnp.sum(x * x, axis=-1, keepdims=True) + EPS)


def alibi_slopes(n_heads):
    h = jnp.arange(1, n_heads + 1, dtype=jnp.float32)
    return jnp.exp2(-8.0 * h / n_heads)


def split_cols(z, sizes):
    outs, start = [], 0
    for n in sizes:
        outs.append(z[..., start:start + n])
        start += n
    return outs


def causal_depthwise_conv(x, w):
    width, ch = w.shape
    return lax.conv_general_dilated(
        x, w[:, None, :].astype(x.dtype), window_strides=(1,), padding=[(width - 1, 0)],
        dimension_numbers=('NWC', 'WIO', 'NWC'), feature_group_count=ch)


def multiscale_pool(u, pool_w, pool_scale):
    b, s, _ = u.shape
    uf = u.astype(jnp.float32)
    pos1 = jnp.arange(1, s + 1, dtype=jnp.float32)[None, :, None]
    outs = []
    for gi, win in enumerate(POOL_WINDOWS):
        ug = uf[..., gi * POOL_GROUP_DIM:(gi + 1) * POOL_GROUP_DIM]
        csum = jnp.pad(jnp.cumsum(ug, axis=1), ((0, 0), (1, 0), (0, 0)))
        lower = jnp.pad(csum, ((0, 0), (win - 1, 0), (0, 0)))[:, :s]
        mean = (csum[:, 1:] - lower) / jnp.minimum(pos1, float(win))
        outs.append(mean - ug)
    y = jnp.stack(outs, axis=2)
    y = jnp.einsum('bsgc,gcd->bsgd', y, pool_w.astype(jnp.float32)).reshape(b, s, POOL_WIDTH)
    return (y * pool_scale.astype(jnp.float32)).astype(u.dtype)


def gated_delta_rule(q, k, v, beta, log_decay):
    b, h, s, dk = q.shape
    dv = v.shape[-1]
    c = DN_CHUNK
    n = s // c
    q = q * (dk ** -0.5)
    q, k, v = (a.reshape(b, h, n, c, a.shape[-1]) for a in (q, k, v))
    beta = beta.reshape(b, h, n, c)
    gc = jnp.cumsum(log_decay.reshape(b, h, n, c), axis=-1)
    causal = jnp.tril(jnp.ones((c, c), dtype=bool))
    strict = jnp.tril(jnp.ones((c, c), dtype=bool), -1)
    diff = gc[..., :, None] - gc[..., None, :]
    decay_mat = jnp.where(causal, jnp.exp(jnp.where(causal, diff, 0.0)), 0.0)
    kb = k * beta[..., None]
    a_low = jnp.where(strict, jnp.einsum('bhnid,bhnjd->bhnij', kb, k) * decay_mat, 0.0)
    eye = jnp.eye(c, dtype=jnp.float32)
    t_mat = lax.linalg.triangular_solve(eye + a_low, jnp.broadcast_to(eye, a_low.shape),
                                        left_side=True, lower=True, unit_diagonal=True)
    w_c = jnp.matmul(t_mat, kb * jnp.exp(gc)[..., None])
    u_c = jnp.matmul(t_mat, v * beta[..., None])
    attn = jnp.where(causal, jnp.einsum('bhnid,bhnjd->bhnij', q, k) * decay_mat, 0.0)

    def step(state, xs):
        q_i, k_i, w_i, u_i, attn_i, gc_i = xs
        v_new = u_i - jnp.matmul(w_i, state)
        o_i = jnp.matmul(q_i * jnp.exp(gc_i)[..., None], state) + jnp.matmul(attn_i, v_new)
        g_last = gc_i[..., -1]
        k_dec = k_i * jnp.exp(g_last[..., None] - gc_i)[..., None]
        state = state * jnp.exp(g_last)[..., None, None] + jnp.einsum('bhcd,bhce->bhde', k_dec, v_new)
        return state, o_i

    xs = tuple(jnp.moveaxis(a, 2, 0) for a in (q, k, w_c, u_c, attn, gc))
    state0 = jnp.zeros((b, h, dk, dv), jnp.float32)
    _, o = lax.scan(step, state0, xs)
    return jnp.moveaxis(o, 0, 2).reshape(b, h, s, dv)


def gated_deltanet(q, k, v, gate, beta_logit, alpha_logit, conv_w, a_log, dt_bias, o_norm):
    b, s, _ = q.shape
    f32 = jnp.float32
    qkv = jax.nn.silu(causal_depthwise_conv(jnp.concatenate([q, k, v], axis=-1), conv_w))
    qkv = jnp.transpose(qkv.astype(f32).reshape(b, s, 3, DN_HEADS, DN_HEAD_DIM), (2, 0, 3, 1, 4))
    qh, kh, vh = l2norm(qkv[0]), l2norm(qkv[1]), qkv[2]
    beta = jnp.transpose(jax.nn.sigmoid(beta_logit.astype(f32)), (0, 2, 1))
    log_decay = -jnp.exp(a_log.astype(f32)) * jax.nn.softplus(alpha_logit.astype(f32) + dt_bias.astype(f32))
    o = gated_delta_rule(qh, kh, vh, beta, jnp.transpose(log_decay, (0, 2, 1)))
    o = jnp.transpose(o, (0, 2, 1, 3))
    o = rmsnorm(o, o_norm) * jax.nn.silu(gate.astype(f32).reshape(b, s, DN_HEADS, DN_HEAD_DIM))
    return o.reshape(b, s, DN_WIDTH).astype(q.dtype)


def pool_delta_mixer(h, w_in, pool_w, pool_scale, conv_w, a_log, dt_bias, o_norm, w_out):
    z = h @ w_in
    u, q, k, v, gate, beta_l, alpha_l = split_cols(z, A_SIZES)
    y_pool = multiscale_pool(u, pool_w, pool_scale)
    y_dn = gated_deltanet(q, k, v, gate, beta_l, alpha_l, conv_w, a_log, dt_bias, o_norm)
    return jnp.concatenate([y_pool, y_dn], axis=-1) @ w_out


def compress_kv(kv, pe, w1, w2):
    b, s, g, d = kv.shape
    n_str = s // CMP_STRIDE
    per = CMP_LEN // CMP_STRIDE
    n_cmp = n_str - per + 1
    c = kv.reshape(b, n_str, CMP_STRIDE, g, d)
    blocks = jnp.concatenate([c[:, j:j + n_cmp] for j in range(per)], axis=2)
    blocks = blocks + pe[None, None, :, None, :].astype(kv.dtype)
    flat = jnp.transpose(blocks, (0, 1, 3, 2, 4)).reshape(b, n_cmp, g, CMP_LEN * d)
    return jax.nn.silu(flat @ w1) @ w2


def nsa_attention(q, kc, vc, ks, vs, kw, vw, gate_logit, pe_k, w1_k, w2_k, pe_v, w1_v, w2_v):
    b, s, _ = q.shape
    dt = q.dtype
    f32 = jnp.float32
    g_, r_, d_ = NSA_GROUPS, NSA_REP, NSA_HEAD_DIM
    qh = q.reshape(b, s, g_, r_, d_)
    ck = compress_kv(kc.reshape(b, s, g_, d_), pe_k, w1_k, w2_k)
    cv = compress_kv(vc.reshape(b, s, g_, d_), pe_v, w1_v, w2_v)
    n_cmp = ck.shape[1]
    n_slc = s // SLC_LEN
    top_n = min(SLC_TOP, n_slc)
    ks_b = jnp.transpose(ks.reshape(b, n_slc, SLC_LEN, g_, d_), (0, 3, 1, 2, 4))
    vs_b = jnp.transpose(vs.reshape(b, n_slc, SLC_LEN, g_, d_), (0, 3, 1, 2, 4))
    kw_p = jnp.pad(kw.reshape(b, s, g_, d_), ((0, 0), (WINDOW, 0), (0, 0), (0, 0)))
    vw_p = jnp.pad(vw.reshape(b, s, g_, d_), ((0, 0), (WINDOW, 0), (0, 0), (0, 0)))
    gates = jax.nn.sigmoid(gate_logit.astype(f32)).reshape(b, s, g_, r_, 3)
    slopes = alibi_slopes(NSA_HEADS).reshape(g_, r_)
    scale = d_ ** -0.5
    c_lo = jnp.arange(n_cmp) * CMP_STRIDE
    cmp_end = c_lo + CMP_LEN - 1
    s_lo = jnp.arange(n_slc) * SLC_LEN
    overlap = jnp.clip(jnp.minimum(c_lo[:, None] + CMP_LEN, s_lo[None, :] + SLC_LEN)
                       - jnp.maximum(c_lo[:, None], s_lo[None, :]), 0, None).astype(f32) / CMP_LEN
    blk = jnp.arange(n_slc)
    in_blk = jnp.arange(SLC_LEN)
    win_off = jnp.arange(WINDOW + NSA_QBLOCK)
    b_idx = jnp.arange(b)[:, None, None, None]
    g_idx = jnp.arange(g_)[None, :, None, None]

    def one_block(i):
        q0 = i * NSA_QBLOCK
        qb = lax.dynamic_slice_in_dim(qh, q0, NSA_QBLOCK, axis=1)
        t = q0 + jnp.arange(NSA_QBLOCK)
        dist_c = (t[:, None] - cmp_end[None, :]).astype(f32)
        ok_c = dist_c >= 0
        sc = jnp.einsum('bqgrd,bcgd->bgrqc', qb, ck).astype(f32) * scale
        sc = jnp.where(ok_c, sc - slopes[:, :, None, None] * dist_c, NEG_INF)
        p_cmp = jnp.where(jnp.any(ok_c, axis=-1)[:, None], jax.nn.softmax(sc, axis=-1), 0.0)
        o_cmp = jnp.einsum('bgrqc,bcgd->bqgrd', p_cmp.astype(dt), cv)
        imp = jnp.einsum('bgrqc,cn->bgqn', p_cmp, overlap)
        cur = t[:, None] // SLC_LEN
        forced = (blk[None, :] == 0) | (blk[None, :] == cur) | (blk[None, :] == cur - 1)
        causal_blk = blk[None, :] * SLC_LEN <= t[:, None]
        imp = jnp.where(forced, FORCE_SCORE, jnp.where(causal_blk, imp, -1.0))
        _, idx = lax.top_k(imp, top_n)
        kg = ks_b[b_idx, g_idx, idx]
        vg = vs_b[b_idx, g_idx, idx]
        dist_s = (t[None, None, :, None, None] - (idx[..., None] * SLC_LEN + in_blk)).astype(f32)
        ss = jnp.einsum('bqgrd,bgqnld->bgrqnl', qb, kg).astype(f32) * scale
        ss = jnp.where(dist_s[:, :, None] >= 0,
                       ss - slopes[None, :, :, None, None, None] * dist_s[:, :, None], NEG_INF)
        p_slc = jax.nn.softmax(ss.reshape(b, g_, r_, NSA_QBLOCK, -1), axis=-1).reshape(ss.shape)
        o_slc = jnp.einsum('bgrqnl,bgqnld->bqgrd', p_slc.astype(dt), vg)
        kwb = lax.dynamic_slice_in_dim(kw_p, q0, WINDOW + NSA_QBLOCK, axis=1)
        vwb = lax.dynamic_slice_in_dim(vw_p, q0, WINDOW + NSA_QBLOCK, axis=1)
        kpos = q0 - WINDOW + win_off
        dist_w = t[:, None] - kpos[None, :]
        ok_w = (dist_w >= 0) & (dist_w < WINDOW) & (kpos[None, :] >= 0)
        sw = jnp.einsum('bqgrd,bkgd->bgrqk', qb, kwb).astype(f32) * scale
        sw = jnp.where(ok_w, sw - slopes[:, :, None, None] * dist_w.astype(f32), NEG_INF)
        p_win = jax.nn.softmax(sw, axis=-1)
        o_win = jnp.einsum('bgrqk,bkgd->bqgrd', p_win.astype(dt), vwb)
        gb = lax.dynamic_slice_in_dim(gates, q0, NSA_QBLOCK, axis=1)
        o = gb[..., 0:1] * o_cmp + gb[..., 1:2] * o_slc + gb[..., 2:3] * o_win
        return o.reshape(b, NSA_QBLOCK, NSA_HEADS * d_).astype(dt)

    out = lax.map(one_block, jnp.arange(s // NSA_QBLOCK))
    return jnp.moveaxis(out, 0, 1).reshape(b, s, NSA_HEADS * d_)


def nsa_mixer(h, w_in, pe_k, w1_k, w2_k, pe_v, w1_v, w2_v, w_out):
    z = h @ w_in
    q, kc, vc, ks, vs, kw, vw, gl = split_cols(z, C_SIZES)
    return nsa_attention(q, kc, vc, ks, vs, kw, vw, gl, pe_k, w1_k, w2_k, pe_v, w1_v, w2_v) @ w_out


def memory_xattn(h, mem_h, wq, wk, wv, wo):
    b, s, _ = h.shape
    m = mem_h.shape[1]
    q = (h @ wq).reshape(b, s, XA_HEADS, XA_HEAD_DIM)
    k = (mem_h @ wk).reshape(b, m, XA_HEADS, XA_HEAD_DIM)
    v = (mem_h @ wv).reshape(b, m, XA_HEADS, XA_HEAD_DIM)
    sc = jnp.einsum('bshd,bmhd->bhsm', q, k).astype(jnp.float32) * (XA_HEAD_DIM ** -0.5)
    p = jax.nn.softmax(sc, axis=-1).astype(h.dtype)
    o = jnp.einsum('bhsm,bmhd->bshd', p, v).reshape(b, s, D_MODEL)
    return o @ wo


def squared_relu_mlp(h, w1, w2):
    return jnp.square(jax.nn.relu(h @ w1)) @ w2


def setup_inputs(seed: int = 0) -> dict:
    key = jax.random.key(seed)
    keys = iter(jax.random.split(key, 40))
    f32 = jnp.float32

    def dense(shape, fan_in):
        return jax.random.normal(next(keys), shape, f32) * (fan_in ** -0.5)

    def gain(shape):
        return 1.0 + 0.05 * jax.random.normal(next(keys), shape, f32)

    ne, no, dep = N_EVEN, N_ODD, DEPTH
    x = jax.random.normal(next(keys), (BATCH, SEQ, D_MODEL), f32)
    mem = jax.random.normal(next(keys), (BATCH, MEM_LEN, D_MODEL), f32)
    a_ln = gain((ne, D_MODEL))
    a_w_in = dense((ne, D_MODEL, IN_A_COLS), D_MODEL)
    a_pool_w = dense((ne, POOL_GROUPS, POOL_GROUP_DIM, POOL_GROUP_DIM), POOL_GROUP_DIM)
    a_pool_scale = gain((ne, POOL_WIDTH))
    a_conv_w = dense((ne, DN_CONV, 3 * DN_WIDTH), DN_CONV)
    a_a_log = jnp.log(jax.random.uniform(next(keys), (ne, DN_HEADS), f32, 1.0, 16.0))
    dt_init = jnp.exp(jax.random.uniform(next(keys), (ne, DN_HEADS), f32, math.log(1e-3), math.log(1e-1)))
    a_dt_bias = dt_init + jnp.log(-jnp.expm1(-dt_init))
    a_o_norm = gain((ne, DN_HEAD_DIM))
    a_w_out = dense((ne, D_MODEL, D_MODEL), D_MODEL)
    c_ln = gain((no, D_MODEL))
    c_w_in = dense((no, D_MODEL, IN_C_COLS), D_MODEL)
    c_pe_k = 0.1 * jax.random.normal(next(keys), (no, CMP_LEN, NSA_HEAD_DIM), f32)
    c_w1_k = dense((no, CMP_LEN * NSA_HEAD_DIM, CMP_HIDDEN), CMP_LEN * NSA_HEAD_DIM)
    c_w2_k = dense((no, CMP_HIDDEN, NSA_HEAD_DIM), CMP_HIDDEN)
    c_pe_v = 0.1 * jax.random.normal(next(keys), (no, CMP_LEN, NSA_HEAD_DIM), f32)
    c_w1_v = dense((no, CMP_LEN * NSA_HEAD_DIM, CMP_HIDDEN), CMP_LEN * NSA_HEAD_DIM)
    c_w2_v = dense((no, CMP_HIDDEN, NSA_HEAD_DIM), CMP_HIDDEN)
    c_w_out = dense((no, D_MODEL, D_MODEL), D_MODEL)
    xa_ln = gain((dep, D_MODEL))
    xa_mem_ln = gain((dep, D_MODEL))
    xa_wq = dense((dep, D_MODEL, D_MODEL), D_MODEL)
    xa_wk = dense((dep, D_MODEL, D_MODEL), D_MODEL)
    xa_wv = dense((dep, D_MODEL, D_MODEL), D_MODEL)
    xa_wo = dense((dep, D_MODEL, D_MODEL), D_MODEL)
    ff_ln = gain((dep, D_MODEL))
    ff_w1 = dense((dep, D_MODEL, FF_DIM), D_MODEL)
    ff_w2 = dense((dep, FF_DIM, D_MODEL), FF_DIM)
    final_ln = gain((D_MODEL,))
    return {'x': x, 'mem': mem,
            'a_ln': a_ln, 'a_w_in': a_w_in, 'a_pool_w': a_pool_w, 'a_pool_scale': a_pool_scale,
            'a_conv_w': a_conv_w, 'a_a_log': a_a_log, 'a_dt_bias': a_dt_bias, 'a_o_norm': a_o_norm,
            'a_w_out': a_w_out,
            'c_ln': c_ln, 'c_w_in': c_w_in, 'c_pe_k': c_pe_k, 'c_w1_k': c_w1_k, 'c_w2_k': c_w2_k,
            'c_pe_v': c_pe_v, 'c_w1_v': c_w1_v, 'c_w2_v': c_w2_v, 'c_w_out': c_w_out,
            'xa_ln': xa_ln, 'xa_mem_ln': xa_mem_ln, 'xa_wq': xa_wq, 'xa_wk': xa_wk, 'xa_wv': xa_wv,
            'xa_wo': xa_wo,
            'ff_ln': ff_ln, 'ff_w1': ff_w1, 'ff_w2': ff_w2, 'final_ln': final_ln}


def reference(x, mem,
              a_ln, a_w_in, a_pool_w, a_pool_scale, a_conv_w, a_a_log, a_dt_bias, a_o_norm, a_w_out,
              c_ln, c_w_in, c_pe_k, c_w1_k, c_w2_k, c_pe_v, c_w1_v, c_w2_v, c_w_out,
              xa_ln, xa_mem_ln, xa_wq, xa_wk, xa_wv, xa_wo,
              ff_ln, ff_w1, ff_w2, final_ln):
    for l in range(DEPTH):
        i = l // 2
        if l % 2 == 0:
            x = x + pool_delta_mixer(rmsnorm(x, a_ln[i]), a_w_in[i], a_pool_w[i], a_pool_scale[i],
                                     a_conv_w[i], a_a_log[i], a_dt_bias[i], a_o_norm[i], a_w_out[i])
        else:
            x = x + nsa_mixer(rmsnorm(x, c_ln[i]), c_w_in[i], c_pe_k[i], c_w1_k[i], c_w2_k[i],
                              c_pe_v[i], c_w1_v[i], c_w2_v[i], c_w_out[i])
        x = x + memory_xattn(rmsnorm(x, xa_ln[l]), rmsnorm(mem, xa_mem_ln[l]),
                             xa_wq[l], xa_wk[l], xa_wv[l], xa_wo[l])
        x = x + squared_relu_mlp(rmsnorm(x, ff_ln[l]), ff_w1[l], ff_w2[l])
    return rmsnorm(x, final_ln)
```

```python
import functools
import math

import jax
import jax.numpy as jnp
from jax import lax
from jax.experimental import pallas as pl
from jax.experimental.pallas import tpu as pltpu

F32 = jnp.float32
BF16 = jnp.bfloat16
HIGHEST = lax.Precision.HIGHEST

LANES = 128
VMEM_LIMIT = 56 * 1024 * 1024

EPS = 1e-6
NEG_INF = -1e30
FORCE_SCORE = 1e4

POOL_WINDOWS = (2, 4, 8, 16)
POOL_GROUP_DIM = 128
POOL_WIDTH = 512
DN_HEADS = 4
DN_HEAD_DIM = 128
DN_WIDTH = 512
DN_CONV = 4
DN_CHUNK = 64

NSA_HEAD_DIM = 64
NSA_GROUPS = 4
NSA_REP = 4
NSA_HEADS = 16
CMP_LEN = 32
CMP_STRIDE = 16
CMP_HIDDEN = 128
SLC_LEN = 64
SLC_TOP = 16
WINDOW = 512
SLC_CHUNK = 512

XA_HEADS = 4
XA_HEAD_DIM = 256


def _params(*sem):
    return pltpu.CompilerParams(dimension_semantics=sem, vmem_limit_bytes=VMEM_LIMIT)


def _const_spec(shape):
    nd = len(shape)
    return pl.BlockSpec(shape, lambda *_: (0,) * nd)


def _rms(x, g):
    return x * lax.rsqrt(jnp.mean(x * x, axis=-1, keepdims=True) + EPS) * g


def _sigmoid(x):
    return 1.0 / (1.0 + jnp.exp(-x))


def _div_pow2(x, n):
    return lax.shift_right_logical(x, jnp.int32(int(math.log2(n))))


def _dot(a, b, precision=None):
    return jnp.dot(a, b, preferred_element_type=F32, precision=precision)


def _dot_nt(a, b, precision=None):
    return lax.dot_general(a, b, (((1,), (1,)), ((), ())), preferred_element_type=F32,
                           precision=precision)


def _dot_tn(a, b):
    return lax.dot_general(a, b, (((0,), (0,)), ((), ())), preferred_element_type=F32)


def _norm_matmul_kernel(x_ref, g_ref, *refs):
    n = len(refs) // 2
    hb = _rms(x_ref[...], g_ref[...]).astype(BF16)
    for w_ref, o_ref in zip(refs[:n], refs[n:]):
        o_ref[...] = _dot(hb, w_ref[...]).astype(o_ref.dtype)


def norm_matmul(x2d, g, ws, out_dtypes, tm=512):
    t, d = x2d.shape
    tm = min(tm, t)
    return pl.pallas_call(
        _norm_matmul_kernel,
        grid=(t // tm,),
        in_specs=[pl.BlockSpec((tm, d), lambda i: (i, 0)), _const_spec((1, d))]
        + [_const_spec(w.shape) for w in ws],
        out_specs=[pl.BlockSpec((tm, w.shape[1]), lambda i: (i, 0)) for w in ws],
        out_shape=[jax.ShapeDtypeStruct((t, w.shape[1]), dt) for w, dt in zip(ws, out_dtypes)],
        compiler_params=_params("parallel"),
    )(x2d, g.reshape(1, d).astype(F32), *ws)


def _matmul_res_kernel(*refs):
    n = (len(refs) - 2) // 2
    res_ref, o_ref = refs[2 * n], refs[2 * n + 1]
    acc = res_ref[...]
    for a_ref, w_ref in zip(refs[:n], refs[n:2 * n]):
        acc = acc + _dot(a_ref[...], w_ref[...])
    o_ref[...] = acc


def matmul_res(a_list, w_list, res2d, tm=512):
    t, d = res2d.shape
    tm = min(tm, t)
    return pl.pallas_call(
        _matmul_res_kernel,
        grid=(t // tm,),
        in_specs=[pl.BlockSpec((tm, a.shape[1]), lambda i: (i, 0)) for a in a_list]
        + [_const_spec(w.shape) for w in w_list]
        + [pl.BlockSpec((tm, d), lambda i: (i, 0))],
        out_specs=pl.BlockSpec((tm, d), lambda i: (i, 0)),
        out_shape=jax.ShapeDtypeStruct((t, d), F32),
        compiler_params=_params("parallel"),
    )(*a_list, *w_list, res2d)


def _mlp_kernel(x_ref, g_ref, w1_ref, w2_ref, gf_ref, o_ref, *, ff_chunk, final_norm):
    x = x_ref[...]
    hb = _rms(x, g_ref[...]).astype(BF16)
    acc = x
    for c in range(w1_ref.shape[1] // ff_chunk):
        a = jnp.maximum(_dot(hb, w1_ref[:, c * ff_chunk:(c + 1) * ff_chunk]), 0.0)
        acc = acc + _dot((a * a).astype(BF16), w2_ref[c * ff_chunk:(c + 1) * ff_chunk, :])
    if final_norm:
        acc = _rms(acc, gf_ref[...])
    o_ref[...] = acc


def mlp_sublayer(x2d, g, w1, w2, gf, final_norm, tm=512, ff_chunk=1024):
    t, d = x2d.shape
    tm = min(tm, t)
    return pl.pallas_call(
        functools.partial(_mlp_kernel, ff_chunk=ff_chunk, final_norm=final_norm),
        grid=(t // tm,),
        in_specs=[pl.BlockSpec((tm, d), lambda i: (i, 0)), _const_spec((1, d)),
                  _const_spec(w1.shape), _const_spec(w2.shape), _const_spec((1, d))],
        out_specs=pl.BlockSpec((tm, d), lambda i: (i, 0)),
        out_shape=jax.ShapeDtypeStruct((t, d), F32),
        compiler_params=_params("parallel"),
    )(x2d, g.reshape(1, d).astype(F32), w1, w2, gf.reshape(1, d).astype(F32))


def _xattn_kernel(x_ref, g_ref, wq_ref, k_ref, v_ref, wo_ref, o_ref):
    x = x_ref[0]
    hb = _rms(x, g_ref[...]).astype(BF16)
    q = (_dot(hb, wq_ref[...]) * (XA_HEAD_DIM ** -0.5)).astype(BF16)
    heads = []
    for h in range(XA_HEADS):
        sl = slice(h * XA_HEAD_DIM, (h + 1) * XA_HEAD_DIM)
        s = _dot_nt(q[:, sl], k_ref[:, sl])
        p = jnp.exp(s - jnp.max(s, axis=-1, keepdims=True))
        l = jnp.sum(p, axis=-1, keepdims=True)
        heads.append((_dot(p.astype(BF16), v_ref[:, sl]) / l).astype(BF16))
    o_ref[0] = x + _dot(jnp.concatenate(heads, axis=1), wo_ref[...])


def xattn_sublayer(x, g, wq, k2d, v2d, wo, tq=512):
    b, s, d = x.shape
    tq = min(tq, s)
    m = k2d.shape[0] // b
    return pl.pallas_call(
        _xattn_kernel,
        grid=(b, s // tq),
        in_specs=[pl.BlockSpec((1, tq, d), lambda bi, i: (bi, i, 0)), _const_spec((1, d)),
                  _const_spec(wq.shape),
                  pl.BlockSpec((m, d), lambda bi, i: (bi, 0)),
                  pl.BlockSpec((m, d), lambda bi, i: (bi, 0)),
                  _const_spec(wo.shape)],
        out_specs=pl.BlockSpec((1, tq, d), lambda bi, i: (bi, i, 0)),
        out_shape=jax.ShapeDtypeStruct((b, s, d), F32),
        compiler_params=_params("parallel", "parallel"),
    )(x, g.reshape(1, d).astype(F32), wq, k2d, v2d, wo)


POOL_HALO = 16


def _pool_kernel(u_ref, halo_ref, w_ref, scale_ref, o_ref):
    i = pl.program_id(1)
    ts = u_ref.shape[1]
    u = u_ref[0]
    halo = jnp.where(i == 0, 0.0, halo_ref[0])
    ext = jnp.concatenate([halo, u], axis=0)
    sums = [None] * len(POOL_WINDOWS)
    cur = ext
    for gi, win in enumerate(POOL_WINDOWS):
        cur = cur[:, (POOL_GROUP_DIM if gi else 0):]
        cur = cur + pltpu.roll(cur, win // 2, axis=0)
        sums[gi] = cur[POOL_HALO:, :POOL_GROUP_DIM]
    pos1 = (i * ts + 1 + lax.broadcasted_iota(jnp.int32, (ts, 1), 0)).astype(F32)
    outs = []
    for gi, win in enumerate(POOL_WINDOWS):
        ug = u[:, gi * POOL_GROUP_DIM:(gi + 1) * POOL_GROUP_DIM]
        y = sums[gi] / jnp.minimum(pos1, float(win)) - ug
        outs.append(_dot(y.astype(BF16), w_ref[gi]))
    o_ref[0] = (jnp.concatenate(outs, axis=1) * scale_ref[...]).astype(o_ref.dtype)


def pool_mixer(z, pool_w, pool_scale, ts=512):
    b, s, _ = z.shape
    ts = min(ts, s)
    hb = ts // POOL_HALO
    return pl.pallas_call(
        _pool_kernel,
        grid=(b, s // ts),
        in_specs=[pl.BlockSpec((1, ts, POOL_WIDTH), lambda bi, i: (bi, i, 0)),
                  pl.BlockSpec((1, POOL_HALO, POOL_WIDTH),
                               lambda bi, i: (bi, jnp.maximum(i * hb - 1, 0), 0)),
                  _const_spec(pool_w.shape), _const_spec((1, POOL_WIDTH))],
        out_specs=pl.BlockSpec((1, ts, POOL_WIDTH), lambda bi, i: (bi, i, 0)),
        out_shape=jax.ShapeDtypeStruct((b, s, POOL_WIDTH), BF16),
        compiler_params=_params("parallel", "parallel"),
    )(z, z, pool_w, pool_scale.reshape(1, POOL_WIDTH).astype(F32))


DN_ROWS = DN_HEADS * DN_CHUNK
BETA_LANE = 0
ALPHA_LANE = DN_HEADS


def _stack_heads(x):
    return jnp.concatenate([x[:, h * DN_HEAD_DIM:(h + 1) * DN_HEAD_DIM] for h in range(DN_HEADS)],
                           axis=0)


def _stack_cols(x, lane0):
    return jnp.concatenate([x[:, lane0 + h:lane0 + h + 1] for h in range(DN_HEADS)], axis=0)


def _pick_head_block(wide, row_head):
    out = jnp.zeros((DN_ROWS, DN_HEAD_DIM), F32)
    for h in range(DN_HEADS):
        out = jnp.where(row_head == h, wide[:, h * DN_HEAD_DIM:(h + 1) * DN_HEAD_DIM], out)
    return out


def _deltanet_kernel(q_ref, k_ref, v_ref, gate_ref, ba_ref, cw_ref, alog_ref, dtb_ref, onorm_ref,
                     o_ref, state_ref, tail_ref):
    c = pl.program_id(1)

    @pl.when(c == 0)
    def _():
        state_ref[...] = jnp.zeros_like(state_ref)
        tail_ref[...] = jnp.zeros_like(tail_ref)

    x3 = jnp.concatenate([q_ref[0], k_ref[0], v_ref[0]], axis=1)
    ext = jnp.concatenate([tail_ref[...], x3], axis=0)
    tail_ref[...] = x3[DN_CHUNK - 8:, :]
    cw = cw_ref[...]
    y = cw[DN_CONV - 1:DN_CONV, :] * ext
    for j in range(1, DN_CONV):
        y = y + cw[DN_CONV - 1 - j:DN_CONV - j, :] * pltpu.roll(ext, j, axis=0)
    y = y[8:, :]
    y = y * _sigmoid(y)

    def l2n(a):
        return a * lax.rsqrt(jnp.sum(a * a, axis=-1, keepdims=True) + EPS)

    q_st = l2n(_stack_heads(y[:, :DN_WIDTH])) * (DN_HEAD_DIM ** -0.5)
    k_st = l2n(_stack_heads(y[:, DN_WIDTH:2 * DN_WIDTH]))
    v_st = _stack_heads(y[:, 2 * DN_WIDTH:])

    ba = ba_ref[0]
    beta_all = _sigmoid(ba)
    sp_in = ba + dtb_ref[...]
    softplus = jnp.maximum(sp_in, 0.0) + jnp.log(1.0 + jnp.exp(-jnp.abs(sp_in)))
    g_all = -jnp.exp(alog_ref[...]) * softplus
    ri = lax.broadcasted_iota(jnp.int32, (DN_CHUNK, DN_CHUNK), 0)
    ci = lax.broadcasted_iota(jnp.int32, (DN_CHUNK, DN_CHUNK), 1)
    gc_all = _dot((ri >= ci).astype(F32), g_all, precision=HIGHEST)
    beta_st = _stack_cols(beta_all, BETA_LANE)
    gc_st = _stack_cols(gc_all, ALPHA_LANE)
    glast_st = jnp.concatenate(
        [jnp.broadcast_to(gc_all[DN_CHUNK - 1:DN_CHUNK, ALPHA_LANE + h:ALPHA_LANE + h + 1],
                          (DN_CHUNK, 1)) for h in range(DN_HEADS)], axis=0)

    rr = lax.broadcasted_iota(jnp.int32, (DN_ROWS, DN_ROWS), 0)
    cc = lax.broadcasted_iota(jnp.int32, (DN_ROWS, DN_ROWS), 1)
    same_head = _div_pow2(rr, DN_CHUNK) == _div_pow2(cc, DN_CHUNK)
    causal = same_head & (rr >= cc)
    strict = same_head & (rr > cc)
    gcb = jnp.broadcast_to(gc_st, (DN_ROWS, DN_ROWS))
    decay = jnp.where(causal, jnp.exp(jnp.where(causal, gcb - gcb.T, 0.0)), 0.0)

    kb_st = k_st * beta_st
    k_bf = k_st.astype(BF16)
    a_low = jnp.where(strict, _dot_nt(kb_st.astype(BF16), k_bf) * decay, 0.0)
    p = -a_low
    t_mat = jnp.where(rr == cc, 1.0, 0.0) + p
    for _ in range(int(math.log2(DN_CHUNK)) - 1):
        p = _dot(p, p, precision=HIGHEST)
        t_mat = t_mat + _dot(t_mat, p, precision=HIGHEST)
    t_bf = t_mat.astype(BF16)
    egc = jnp.exp(gc_st)
    w_st = _dot(t_bf, (kb_st * egc).astype(BF16))
    u_st = _dot(t_bf, (v_st * beta_st).astype(BF16))
    attn = jnp.where(causal, _dot_nt(q_st.astype(BF16), k_bf) * decay, 0.0)

    state = state_ref[...]
    s_bf = state.astype(BF16)
    row_head = _div_pow2(lax.broadcasted_iota(jnp.int32, (DN_ROWS, 1), 0), DN_CHUNK)
    v_new = u_st - _pick_head_block(_dot(w_st.astype(BF16), s_bf), row_head)
    v_new_bf = v_new.astype(BF16)
    o_st = (_pick_head_block(_dot((q_st * egc).astype(BF16), s_bf), row_head)
            + _dot(attn.astype(BF16), v_new_bf))
    k_dec = (k_st * jnp.exp(glast_st - gc_st)).astype(BF16)
    zero = jnp.zeros_like(v_new_bf)
    v_wide = jnp.concatenate([jnp.where(row_head == h, v_new_bf, zero) for h in range(DN_HEADS)],
                             axis=1)
    state_scale = jnp.concatenate(
        [jnp.broadcast_to(jnp.exp(gc_all[DN_CHUNK - 1:DN_CHUNK, ALPHA_LANE + h:ALPHA_LANE + h + 1]),
                          (1, DN_HEAD_DIM)) for h in range(DN_HEADS)], axis=1)
    state_ref[...] = state * state_scale + _dot_tn(k_dec, v_wide)

    gate = gate_ref[0]
    outs = []
    for h in range(DN_HEADS):
        o_h = _rms(o_st[h * DN_CHUNK:(h + 1) * DN_CHUNK, :], onorm_ref[...])
        g_h = gate[:, h * DN_HEAD_DIM:(h + 1) * DN_HEAD_DIM]
        outs.append(o_h * (g_h * _sigmoid(g_h)))
    o_ref[0] = jnp.concatenate(outs, axis=1).astype(o_ref.dtype)


def gated_deltanet(z, ba_block, conv_w, a_log, dt_bias, o_norm):
    b, s, _ = z.shape
    lane_row = lambda vals, lane0: jnp.zeros((1, LANES), F32).at[0, lane0:lane0 + DN_HEADS].set(
        vals.astype(F32))
    col = lambda j: pl.BlockSpec((1, DN_CHUNK, DN_WIDTH), lambda bi, ci: (bi, ci, j))
    return pl.pallas_call(
        _deltanet_kernel,
        grid=(b, s // DN_CHUNK),
        in_specs=[col(1), col(2), col(3), col(4),
                  pl.BlockSpec((1, DN_CHUNK, LANES), lambda bi, ci: (bi, ci, ba_block)),
                  _const_spec(conv_w.shape), _const_spec((1, LANES)), _const_spec((1, LANES)),
                  _const_spec((1, DN_HEAD_DIM))],
        out_specs=pl.BlockSpec((1, DN_CHUNK, DN_WIDTH), lambda bi, ci: (bi, ci, 0)),
        out_shape=jax.ShapeDtypeStruct((b, s, DN_WIDTH), BF16),
        scratch_shapes=[pltpu.VMEM((DN_HEAD_DIM, DN_WIDTH), F32),
                        pltpu.VMEM((8, 3 * DN_WIDTH), F32)],
        compiler_params=_params("parallel", "arbitrary"),
    )(z, z, z, z, z, conv_w.astype(F32), lane_row(a_log, ALPHA_LANE), lane_row(dt_bias, ALPHA_LANE),
      o_norm.reshape(1, DN_HEAD_DIM).astype(F32))


def _compress_kernel(rk_ref, rv_ref, w1k_ref, pek_ref, w2k_ref, w1v_ref, pev_ref, w2v_ref,
                     ck_ref, cv_ref):
    n = rk_ref.shape[2]
    row = lax.broadcasted_iota(jnp.int32, (n, 1), 0)

    def one(r_ref, w1_ref, pe_ref, w2_ref, o_ref):
        w1 = w1_ref[...]
        pb = _dot(pe_ref[...], w1)
        bias = pb[0:1, :CMP_HIDDEN] + pb[1:2, CMP_HIDDEN:]
        for g in range(NSA_GROUPS):
            y = _dot(r_ref[0, g], w1)
            h = y[:, :CMP_HIDDEN] + pltpu.roll(y[:, CMP_HIDDEN:], n - 1, axis=0) + bias
            a = (h * _sigmoid(h)).astype(BF16)
            o_ref[0, g] = jnp.where(row < n - 1, _dot(a, w2_ref[...]), 0.0).astype(o_ref.dtype)

    one(rk_ref, w1k_ref, pek_ref, w2k_ref, ck_ref)
    one(rv_ref, w1v_ref, pev_ref, w2v_ref, cv_ref)


def compress_kv(rk, rv, w1k, pek, w2k, w1v, pev, w2v):
    b, g, n, w = rk.shape
    blk = pl.BlockSpec((1, g, n, w), lambda bi: (bi, 0, 0, 0))
    oblk = pl.BlockSpec((1, g, n, NSA_HEAD_DIM), lambda bi: (bi, 0, 0, 0))
    return pl.pallas_call(
        _compress_kernel,
        grid=(b,),
        in_specs=[blk, blk, _const_spec(w1k.shape), _const_spec(pek.shape), _const_spec(w2k.shape),
                  _const_spec(w1v.shape), _const_spec(pev.shape), _const_spec(w2v.shape)],
        out_specs=[oblk, oblk],
        out_shape=[jax.ShapeDtypeStruct((b, g, n, NSA_HEAD_DIM), BF16)] * 2,
        compiler_params=_params("parallel"),
    )(rk, rv, w1k, pek, w2k, w1v, pev, w2v)


def _nsa_kernel(q_ref, gate_ref, slope_ref, ck_ref, cv_ref, ks_ref, vs_ref, kw_ref, vw_ref,
                ovl_ref, o_ref, *, top_n):
    i = pl.program_id(2)
    tq = q_ref.shape[1]
    rows = NSA_REP * tq
    n_cmp = ck_ref.shape[2]
    n_slc = ovl_ref.shape[0]
    s_len = ks_ref.shape[2]
    q0 = i * tq

    q = q_ref[0]
    qs = jnp.concatenate([q[:, r * NSA_HEAD_DIM:(r + 1) * NSA_HEAD_DIM] for r in range(NSA_REP)],
                         axis=0)
    qs = (qs.astype(F32) * (NSA_HEAD_DIM ** -0.5)).astype(BF16)
    slope = jnp.concatenate(
        [jnp.broadcast_to(slope_ref[0, r:r + 1, 0:1], (tq, 1)) for r in range(NSA_REP)], axis=0)
    t_loc = lax.broadcasted_iota(jnp.int32, (tq, 1), 0)
    tpos = q0 + jnp.concatenate([t_loc] * NSA_REP, axis=0)

    cend = lax.broadcasted_iota(jnp.int32, (1, n_cmp), 1) * CMP_STRIDE + (CMP_LEN - 1)
    dist_c = tpos - cend
    sc = _dot_nt(qs, ck_ref[0, 0])
    sc = jnp.where(dist_c >= 0, sc - slope * dist_c.astype(F32), NEG_INF)
    e = jnp.exp(sc - jnp.max(sc, axis=-1, keepdims=True))
    p_cmp = jnp.where(tpos >= CMP_LEN - 1, e / jnp.sum(e, axis=-1, keepdims=True), 0.0)
    o_cmp = _dot(p_cmp.astype(BF16), cv_ref[0, 0])

    p_sum = p_cmp[0:tq]
    for r in range(1, NSA_REP):
        p_sum = p_sum + p_cmp[r * tq:(r + 1) * tq]
    imp = _dot_nt(ovl_ref[...], p_sum, precision=HIGHEST)
    blk = lax.broadcasted_iota(jnp.int32, (n_slc, tq), 0)
    tl = q0 + lax.broadcasted_iota(jnp.int32, (n_slc, tq), 1)
    cur = _div_pow2(tl, SLC_LEN)
    forced = (blk == 0) | (blk == cur) | (blk == cur - 1)
    val = jnp.where(forced, FORCE_SCORE, jnp.where(blk * SLC_LEN <= tl, imp, -1.0))
    rank = jnp.zeros((n_slc, tq), F32)
    for m in range(n_slc):
        vm = val[m:m + 1, :]
        rank = rank + jnp.where(vm > val, 1.0, jnp.where(vm == val, jnp.where(blk > m, 1.0, 0.0), 0.0))
    sel_t = jnp.where(rank < top_n, 1.0, 0.0).astype(BF16)

    blk_e = lax.broadcasted_iota(jnp.int32, (n_slc, SLC_CHUNK), 0)
    key_blk = _div_pow2(lax.broadcasted_iota(jnp.int32, (n_slc, SLC_CHUNK), 1), SLC_LEN)
    key_off = lax.broadcasted_iota(jnp.int32, (1, SLC_CHUNK), 1)

    def slc_step(j, carry):
        m_run, l_run, acc = carry
        k0 = pl.multiple_of(j * SLC_CHUNK, SLC_CHUNK)
        expand = jnp.where(blk_e == key_blk + j * (SLC_CHUNK // SLC_LEN), 1.0, 0.0).astype(BF16)
        chosen = _dot_tn(sel_t, expand)
        chosen = jnp.concatenate([chosen] * NSA_REP, axis=0)
        dist = tpos - (k0 + key_off)
        s = _dot_nt(qs, ks_ref[0, 0, pl.ds(k0, SLC_CHUNK), :])
        s = jnp.where(chosen > 0.5, jnp.where(dist >= 0, s - slope * dist.astype(F32), NEG_INF),
                      NEG_INF)
        m_new = jnp.maximum(m_run, jnp.max(s, axis=-1, keepdims=True))
        alpha = jnp.exp(m_run - m_new)
        p = jnp.exp(s - m_new)
        l_new = alpha * l_run + jnp.sum(p, axis=-1, keepdims=True)
        acc = alpha * acc + _dot(p.astype(BF16), vs_ref[0, 0, pl.ds(k0, SLC_CHUNK), :])
        return m_new, l_new, acc

    n_chunks = (q0 + tq + SLC_CHUNK - 1) // SLC_CHUNK
    m_fin, l_fin, acc = lax.fori_loop(
        0, n_chunks, slc_step,
        (jnp.full((rows, 1), NEG_INF, F32), jnp.zeros((rows, 1), F32),
         jnp.zeros((rows, NSA_HEAD_DIM), F32)))
    o_slc = acc / l_fin

    span = min(WINDOW + tq, s_len)
    w0 = pl.multiple_of(jnp.maximum(q0 + tq - span, 0), tq)
    dist_w = tpos - (w0 + lax.broadcasted_iota(jnp.int32, (1, span), 1))
    sw = _dot_nt(qs, kw_ref[0, 0, pl.ds(w0, span), :])
    sw = jnp.where(dist_w >= 0, jnp.where(dist_w < WINDOW, sw - slope * dist_w.astype(F32), NEG_INF),
                   NEG_INF)
    pw = jnp.exp(sw - jnp.max(sw, axis=-1, keepdims=True))
    o_win = _dot(pw.astype(BF16), vw_ref[0, 0, pl.ds(w0, span), :]) / jnp.sum(pw, axis=-1,
                                                                               keepdims=True)

    gates = _sigmoid(gate_ref[0])

    def gate_col(br):
        return jnp.concatenate([gates[:, 3 * r + br:3 * r + br + 1] for r in range(NSA_REP)], axis=0)

    o = gate_col(0) * o_cmp + gate_col(1) * o_slc + gate_col(2) * o_win
    o_ref[0] = jnp.concatenate([o[r * tq:(r + 1) * tq] for r in range(NSA_REP)],
                               axis=1).astype(o_ref.dtype)


def nsa_attention(z, zg, ck, cv, ks, vs, kw, vw, tq=128):
    b, s, _ = z.shape
    n_cmp = ck.shape[2]
    n_slc = s // SLC_LEN
    tq = min(tq, s)
    c_lo = jnp.arange(n_cmp) * CMP_STRIDE
    s_lo = jnp.arange(n_slc) * SLC_LEN
    ovl = (jnp.clip(jnp.minimum(c_lo[None, :] + CMP_LEN, s_lo[:, None] + SLC_LEN)
                    - jnp.maximum(c_lo[None, :], s_lo[:, None]), 0, None).astype(F32) / CMP_LEN)
    ovl = ovl * (jnp.arange(n_cmp) < n_cmp - 1)[None, :]
    hd = jnp.arange(1, NSA_HEADS + 1, dtype=F32)
    slopes = jnp.exp2(-8.0 * hd / NSA_HEADS).reshape(NSA_GROUPS, NSA_REP)
    slope_tab = jnp.zeros((NSA_GROUPS, 8, LANES), F32).at[:, :NSA_REP, :].set(slopes[:, :, None])
    kv_spec = pl.BlockSpec((1, 1, s, NSA_HEAD_DIM), lambda bi, g, i: (bi, g, 0, 0))
    c_spec = pl.BlockSpec((1, 1, n_cmp, NSA_HEAD_DIM), lambda bi, g, i: (bi, g, 0, 0))
    gw = NSA_REP * NSA_HEAD_DIM
    return pl.pallas_call(
        functools.partial(_nsa_kernel, top_n=min(SLC_TOP, n_slc)),
        grid=(b, NSA_GROUPS, s // tq),
        in_specs=[pl.BlockSpec((1, tq, gw), lambda bi, g, i: (bi, i, g)),
                  pl.BlockSpec((1, tq, LANES), lambda bi, g, i: (bi, i, g)),
                  pl.BlockSpec((1, 8, LANES), lambda bi, g, i: (g, 0, 0)),
                  c_spec, c_spec, kv_spec, kv_spec, kv_spec, kv_spec,
                  _const_spec(ovl.shape)],
        out_specs=pl.BlockSpec((1, tq, gw), lambda bi, g, i: (bi, i, g)),
        out_shape=jax.ShapeDtypeStruct((b, s, NSA_GROUPS * gw), BF16),
        compiler_params=_params("parallel", "parallel", "arbitrary"),
    )(z, zg, slope_tab, ck, cv, ks, vs, kw, vw, ovl)


def _pad_cols(w, n):
    return jnp.pad(w, ((0, 0), (0, n - w.shape[1])))


def pool_delta_layer(x, ln, w_in, pool_w, pool_scale, conv_w, a_log, dt_bias, o_norm, w_out):
    b, s, d = x.shape
    main = POOL_WIDTH + 4 * DN_WIDTH
    w_all = jnp.concatenate([w_in[:, :main], _pad_cols(w_in[:, main:], LANES)], axis=1).astype(BF16)
    (z,) = norm_matmul(x.reshape(b * s, d), ln, [w_all], [F32])
    z = z.reshape(b, s, main + LANES)
    y_pool = pool_mixer(z, pool_w.astype(BF16), pool_scale)
    y_dn = gated_deltanet(z, main // LANES, conv_w, a_log, dt_bias, o_norm)
    w_out = w_out.astype(BF16)
    return matmul_res([y_pool.reshape(b * s, POOL_WIDTH), y_dn.reshape(b * s, DN_WIDTH)],
                      [w_out[:POOL_WIDTH], w_out[POOL_WIDTH:]], x.reshape(b * s, d)).reshape(b, s, d)


def _cmp_weights(pe, w1, w2):
    half = (CMP_LEN // 2) * NSA_HEAD_DIM
    w1_pair = jnp.concatenate([w1[:half], w1[half:]], axis=1).astype(BF16)
    pe_rows = jnp.zeros((8, half), F32).at[0:2].set(pe.reshape(2, half)).astype(BF16)
    return w1_pair, pe_rows, w2.astype(BF16)


def nsa_layer(x, ln, w_in, pe_k, w1_k, w2_k, pe_v, w1_v, w2_v, w_out):
    b, s, d = x.shape
    g_, r_, hd = NSA_GROUPS, NSA_REP, NSA_HEAD_DIM
    kvw = g_ * hd
    main = d + 6 * kvw
    wg = w_in[:, main:].reshape(d, g_, r_ * 3)
    wg = jnp.pad(wg, ((0, 0), (0, 0), (0, LANES - r_ * 3))).reshape(d, g_ * LANES)
    z, zg = norm_matmul(x.reshape(b * s, d), ln, [w_in[:, :main].astype(BF16), wg.astype(BF16)],
                        [BF16, F32])
    z = z.reshape(b, s, main)
    zg = zg.reshape(b, s, g_ * LANES)

    def group_major(j):
        return jnp.transpose(z[..., d + j * kvw:d + (j + 1) * kvw].reshape(b, s, g_, hd), (0, 2, 1, 3))

    n_str = s // CMP_STRIDE
    rk = group_major(0).reshape(b, g_, n_str, CMP_STRIDE * hd)
    rv = group_major(1).reshape(b, g_, n_str, CMP_STRIDE * hd)
    ck, cv = compress_kv(rk, rv, *_cmp_weights(pe_k, w1_k, w2_k), *_cmp_weights(pe_v, w1_v, w2_v))
    o = nsa_attention(z, zg, ck, cv, group_major(2), group_major(3), group_major(4), group_major(5))
    return matmul_res([o.reshape(b * s, d)], [w_out.astype(BF16)], x.reshape(b * s, d)).reshape(b, s, d)


def kernel(x, mem, a_ln, a_w_in, a_pool_w, a_pool_scale, a_conv_w, a_a_log, a_dt_bias, a_o_norm, a_w_out, c_ln, c_w_in, c_pe_k, c_w1_k, c_w2_k, c_pe_v, c_w1_v, c_w2_v, c_w_out, xa_ln, xa_mem_ln, xa_wq, xa_wk, xa_wv, xa_wo, ff_ln, ff_w1, ff_w2, final_ln):
    b, s, d = x.shape
    depth = xa_ln.shape[0]
    mem2d = mem.reshape(b * mem.shape[1], d)
    for l in range(depth):
        i = l // 2
        if l % 2 == 0:
            x = pool_delta_layer(x, a_ln[i], a_w_in[i], a_pool_w[i], a_pool_scale[i], a_conv_w[i],
                                 a_a_log[i], a_dt_bias[i], a_o_norm[i], a_w_out[i])
        else:
            x = nsa_layer(x, c_ln[i], c_w_in[i], c_pe_k[i], c_w1_k[i], c_w2_k[i],
                          c_pe_v[i], c_w1_v[i], c_w2_v[i], c_w_out[i])
        mk, mv = norm_matmul(mem2d, xa_mem_ln[l], [xa_wk[l].astype(BF16), xa_wv[l].astype(BF16)],
                             [BF16, BF16])
        x = xattn_sublayer(x, xa_ln[l], xa_wq[l].astype(BF16), mk, mv, xa_wo[l].astype(BF16))
        x = mlp_sublayer(x.reshape(b * s, d), ff_ln[l], ff_w1[l].astype(BF16), ff_w2[l].astype(BF16),
                         final_ln, final_norm=(l == depth - 1)).reshape(b, s, d)
    return x
```

```python
import functools
import math

import jax
import jax.numpy as jnp
from jax import lax
from jax.experimental import pallas as pl
from jax.experimental.pallas import tpu as pltpu

F32 = jnp.float32
BF16 = jnp.bfloat16
HIGHEST = lax.Precision.HIGHEST

LANES = 128
VMEM_LIMIT = 56 * 1024 * 1024

EPS = 1e-6
NEG_INF = -1e30
FORCE_SCORE = 1e4

POOL_WINDOWS = (2, 4, 8, 16)
POOL_GROUP_DIM = 128
POOL_WIDTH = 512
DN_HEADS = 4
DN_HEAD_DIM = 128
DN_WIDTH = 512
DN_CONV = 4
DN_CHUNK = 64

NSA_HEAD_DIM = 64
NSA_GROUPS = 4
NSA_REP = 4
NSA_HEADS = 16
CMP_LEN = 32
CMP_STRIDE = 16
CMP_HIDDEN = 128
SLC_LEN = 64
SLC_TOP = 16
WINDOW = 512
SLC_CHUNK = 512
SLC_STREAMS = 1
LOG2E = 1.4426950408889634
SLOPE_PIECES = 3
POS_SPLIT = 64

XA_HEADS = 4
XA_HEAD_DIM = 256


def _params(*sem):
    return pltpu.CompilerParams(dimension_semantics=sem, vmem_limit_bytes=VMEM_LIMIT)


def _const_spec(shape):
    nd = len(shape)
    return pl.BlockSpec(shape, lambda *_: (0,) * nd)


def _rms(x, g):
    return x * lax.rsqrt(jnp.mean(x * x, axis=-1, keepdims=True) + EPS) * g


def _sigmoid(x):
    return 1.0 / (1.0 + jnp.exp(-x))


def _div_pow2(x, n):
    return lax.shift_right_logical(x, jnp.int32(int(math.log2(n))))


def _split_bf16(x):
    hi = x.astype(BF16)
    return hi, (x - hi.astype(F32)).astype(BF16)


def _dot(a, b, precision=None):
    return jnp.dot(a, b, preferred_element_type=F32, precision=precision)


def _dot_nt(a, b, precision=None):
    return lax.dot_general(a, b, (((1,), (1,)), ((), ())), preferred_element_type=F32,
                           precision=precision)


def _dot_tn(a, b):
    return lax.dot_general(a, b, (((0,), (0,)), ((), ())), preferred_element_type=F32)


def _norm_matmul_kernel(x_ref, g_ref, *refs):
    n = len(refs) // 2
    hb = _rms(x_ref[...], g_ref[...]).astype(BF16)
    for w_ref, o_ref in zip(refs[:n], refs[n:]):
        o_ref[...] = _dot(hb, w_ref[...]).astype(o_ref.dtype)


def norm_matmul(x2d, g, ws, out_dtypes, tm=512):
    t, d = x2d.shape
    tm = min(tm, t)
    return pl.pallas_call(
        _norm_matmul_kernel,
        grid=(t // tm,),
        in_specs=[pl.BlockSpec((tm, d), lambda i: (i, 0)), _const_spec((1, d))]
        + [_const_spec(w.shape) for w in ws],
        out_specs=[pl.BlockSpec((tm, w.shape[1]), lambda i: (i, 0)) for w in ws],
        out_shape=[jax.ShapeDtypeStruct((t, w.shape[1]), dt) for w, dt in zip(ws, out_dtypes)],
        compiler_params=_params("parallel"),
    )(x2d, g.reshape(1, d).astype(F32), *ws)


def _matmul_res_kernel(*refs):
    n = (len(refs) - 2) // 2
    res_ref, o_ref = refs[2 * n], refs[2 * n + 1]
    acc = res_ref[...]
    for a_ref, w_ref in zip(refs[:n], refs[n:2 * n]):
        acc = acc + _dot(a_ref[...], w_ref[...])
    o_ref[...] = acc


def matmul_res(a_list, w_list, res2d, tm=512):
    t, d = res2d.shape
    tm = min(tm, t)
    return pl.pallas_call(
        _matmul_res_kernel,
        grid=(t // tm,),
        in_specs=[pl.BlockSpec((tm, a.shape[1]), lambda i: (i, 0)) for a in a_list]
        + [_const_spec(w.shape) for w in w_list]
        + [pl.BlockSpec((tm, d), lambda i: (i, 0))],
        out_specs=pl.BlockSpec((tm, d), lambda i: (i, 0)),
        out_shape=jax.ShapeDtypeStruct((t, d), F32),
        compiler_params=_params("parallel"),
    )(*a_list, *w_list, res2d)


def _mlp_kernel(x_ref, g_ref, w1_ref, w2_ref, gf_ref, o_ref, *, ff_chunk, final_norm):
    x = x_ref[...]
    hb = _rms(x, g_ref[...]).astype(BF16)
    acc = x
    for c in range(w1_ref.shape[1] // ff_chunk):
        a = jnp.maximum(_dot(hb, w1_ref[:, c * ff_chunk:(c + 1) * ff_chunk]), 0.0)
        acc = acc + _dot((a * a).astype(BF16), w2_ref[c * ff_chunk:(c + 1) * ff_chunk, :])
    if final_norm:
        acc = _rms(acc, gf_ref[...])
    o_ref[...] = acc


def mlp_sublayer(x2d, g, w1, w2, gf, final_norm, tm=512, ff_chunk=1024):
    t, d = x2d.shape
    tm = min(tm, t)
    return pl.pallas_call(
        functools.partial(_mlp_kernel, ff_chunk=ff_chunk, final_norm=final_norm),
        grid=(t // tm,),
        in_specs=[pl.BlockSpec((tm, d), lambda i: (i, 0)), _const_spec((1, d)),
                  _const_spec(w1.shape), _const_spec(w2.shape), _const_spec((1, d))],
        out_specs=pl.BlockSpec((tm, d), lambda i: (i, 0)),
        out_shape=jax.ShapeDtypeStruct((t, d), F32),
        compiler_params=_params("parallel"),
    )(x2d, g.reshape(1, d).astype(F32), w1, w2, gf.reshape(1, d).astype(F32))


def _xattn_kernel(x_ref, g_ref, wq_ref, k_ref, v_ref, wo_ref, o_ref):
    x = x_ref[0]
    hb = _rms(x, g_ref[...]).astype(BF16)
    q = (_dot(hb, wq_ref[...]) * (XA_HEAD_DIM ** -0.5)).astype(BF16)
    heads = []
    for h in range(XA_HEADS):
        sl = slice(h * XA_HEAD_DIM, (h + 1) * XA_HEAD_DIM)
        s = _dot_nt(q[:, sl], k_ref[:, sl])
        p = jnp.exp(s - jnp.max(s, axis=-1, keepdims=True))
        l = jnp.sum(p, axis=-1, keepdims=True)
        heads.append((_dot(p.astype(BF16), v_ref[:, sl]) / l).astype(BF16))
    o_ref[0] = x + _dot(jnp.concatenate(heads, axis=1), wo_ref[...])


def xattn_sublayer(x, g, wq, k2d, v2d, wo, tq=512):
    b, s, d = x.shape
    tq = min(tq, s)
    m = k2d.shape[0] // b
    return pl.pallas_call(
        _xattn_kernel,
        grid=(b, s // tq),
        in_specs=[pl.BlockSpec((1, tq, d), lambda bi, i: (bi, i, 0)), _const_spec((1, d)),
                  _const_spec(wq.shape),
                  pl.BlockSpec((m, d), lambda bi, i: (bi, 0)),
                  pl.BlockSpec((m, d), lambda bi, i: (bi, 0)),
                  _const_spec(wo.shape)],
        out_specs=pl.BlockSpec((1, tq, d), lambda bi, i: (bi, i, 0)),
        out_shape=jax.ShapeDtypeStruct((b, s, d), F32),
        compiler_params=_params("parallel", "parallel"),
    )(x, g.reshape(1, d).astype(F32), wq, k2d, v2d, wo)


POOL_HALO = 16


def _pool_kernel(u_ref, halo_ref, w_ref, scale_ref, o_ref):
    i = pl.program_id(1)
    ts = u_ref.shape[1]
    u = u_ref[0]
    halo = jnp.where(i == 0, 0.0, halo_ref[0])
    ext = jnp.concatenate([halo, u], axis=0)
    sums = [None] * len(POOL_WINDOWS)
    cur = ext
    for gi, win in enumerate(POOL_WINDOWS):
        cur = cur[:, (POOL_GROUP_DIM if gi else 0):]
        cur = cur + pltpu.roll(cur, win // 2, axis=0)
        sums[gi] = cur[POOL_HALO:, :POOL_GROUP_DIM]
    pos1 = (i * ts + 1 + lax.broadcasted_iota(jnp.int32, (ts, 1), 0)).astype(F32)
    outs = []
    for gi, win in enumerate(POOL_WINDOWS):
        ug = u[:, gi * POOL_GROUP_DIM:(gi + 1) * POOL_GROUP_DIM]
        y = sums[gi] / jnp.minimum(pos1, float(win)) - ug
        outs.append(_dot(y.astype(BF16), w_ref[gi]))
    o_ref[0] = (jnp.concatenate(outs, axis=1) * scale_ref[...]).astype(o_ref.dtype)


def pool_mixer(z, pool_w, pool_scale, ts=512):
    b, s, _ = z.shape
    ts = min(ts, s)
    hb = ts // POOL_HALO
    return pl.pallas_call(
        _pool_kernel,
        grid=(b, s // ts),
        in_specs=[pl.BlockSpec((1, ts, POOL_WIDTH), lambda bi, i: (bi, i, 0)),
                  pl.BlockSpec((1, POOL_HALO, POOL_WIDTH),
                               lambda bi, i: (bi, jnp.maximum(i * hb - 1, 0), 0)),
                  _const_spec(pool_w.shape), _const_spec((1, POOL_WIDTH))],
        out_specs=pl.BlockSpec((1, ts, POOL_WIDTH), lambda bi, i: (bi, i, 0)),
        out_shape=jax.ShapeDtypeStruct((b, s, POOL_WIDTH), BF16),
        compiler_params=_params("parallel", "parallel"),
    )(z, z, pool_w, pool_scale.reshape(1, POOL_WIDTH).astype(F32))


DN_ROWS = DN_HEADS * DN_CHUNK
BETA_LANE = 0
ALPHA_LANE = DN_HEADS


def _stack_heads(x):
    return jnp.concatenate([x[:, h * DN_HEAD_DIM:(h + 1) * DN_HEAD_DIM] for h in range(DN_HEADS)],
                           axis=0)


def _stack_cols(x, lane0):
    return jnp.concatenate([x[:, lane0 + h:lane0 + h + 1] for h in range(DN_HEADS)], axis=0)


def _pick_head_block(wide, row_head):
    out = jnp.zeros((DN_ROWS, DN_HEAD_DIM), F32)
    for h in range(DN_HEADS):
        out = jnp.where(row_head == h, wide[:, h * DN_HEAD_DIM:(h + 1) * DN_HEAD_DIM], out)
    return out


def _deltanet_kernel(q_ref, k_ref, v_ref, gate_ref, ba_ref, cw_ref, alog_ref, dtb_ref, onorm_ref,
                     o_ref, state_ref, tail_ref):
    c = pl.program_id(1)

    @pl.when(c == 0)
    def _():
        state_ref[...] = jnp.zeros_like(state_ref)
        tail_ref[...] = jnp.zeros_like(tail_ref)

    x3 = jnp.concatenate([q_ref[0], k_ref[0], v_ref[0]], axis=1)
    ext = jnp.concatenate([tail_ref[...], x3], axis=0)
    tail_ref[...] = x3[DN_CHUNK - 8:, :]
    cw = cw_ref[...]
    y = cw[DN_CONV - 1:DN_CONV, :] * ext
    for j in range(1, DN_CONV):
        y = y + cw[DN_CONV - 1 - j:DN_CONV - j, :] * pltpu.roll(ext, j, axis=0)
    y = y[8:, :]
    y = y * _sigmoid(y)

    def l2n(a):
        return a * lax.rsqrt(jnp.sum(a * a, axis=-1, keepdims=True) + EPS)

    q_st = l2n(_stack_heads(y[:, :DN_WIDTH])) * (DN_HEAD_DIM ** -0.5)
    k_st = l2n(_stack_heads(y[:, DN_WIDTH:2 * DN_WIDTH]))
    v_st = _stack_heads(y[:, 2 * DN_WIDTH:])

    ba = ba_ref[0]
    beta_all = _sigmoid(ba)
    sp_in = ba + dtb_ref[...]
    softplus = jnp.maximum(sp_in, 0.0) + jnp.log(1.0 + jnp.exp(-jnp.abs(sp_in)))
    g_all = -jnp.exp(alog_ref[...]) * softplus
    ri = lax.broadcasted_iota(jnp.int32, (DN_CHUNK, DN_CHUNK), 0)
    ci = lax.broadcasted_iota(jnp.int32, (DN_CHUNK, DN_CHUNK), 1)
    gc_all = _dot((ri >= ci).astype(F32), g_all, precision=HIGHEST)
    beta_st = _stack_cols(beta_all, BETA_LANE)
    gc_st = _stack_cols(gc_all, ALPHA_LANE)
    glast_st = jnp.concatenate(
        [jnp.broadcast_to(gc_all[DN_CHUNK - 1:DN_CHUNK, ALPHA_LANE + h:ALPHA_LANE + h + 1],
                          (DN_CHUNK, 1)) for h in range(DN_HEADS)], axis=0)

    rr = lax.broadcasted_iota(jnp.int32, (DN_ROWS, DN_ROWS), 0)
    cc = lax.broadcasted_iota(jnp.int32, (DN_ROWS, DN_ROWS), 1)
    same_head = _div_pow2(rr, DN_CHUNK) == _div_pow2(cc, DN_CHUNK)
    causal = same_head & (rr >= cc)
    strict = same_head & (rr > cc)
    gcb = jnp.broadcast_to(gc_st, (DN_ROWS, DN_ROWS))
    decay = jnp.where(causal, jnp.exp(jnp.where(causal, gcb - gcb.T, 0.0)), 0.0)

    kb_st = k_st * beta_st
    k_bf = k_st.astype(BF16)
    a_low = jnp.where(strict, _dot_nt(kb_st.astype(BF16), k_bf) * decay, 0.0)
    p = -a_low
    t_mat = jnp.where(rr == cc, 1.0, 0.0) + p
    p_hi, p_lo = _split_bf16(p)
    for _ in range(int(math.log2(DN_CHUNK)) - 1):
        p = _dot(p_hi, p_hi) + (_dot(p_hi, p_lo) + _dot(p_lo, p_hi))
        p_hi, p_lo = _split_bf16(p)
        t_hi, t_lo = _split_bf16(t_mat)
        t_mat = t_mat + (_dot(t_hi, p_hi) + (_dot(t_hi, p_lo) + _dot(t_lo, p_hi)))
    t_bf = t_mat.astype(BF16)
    egc = jnp.exp(gc_st)
    w_st = _dot(t_bf, (kb_st * egc).astype(BF16))
    u_st = _dot(t_bf, (v_st * beta_st).astype(BF16))
    attn = jnp.where(causal, _dot_nt(q_st.astype(BF16), k_bf) * decay, 0.0)

    state = state_ref[...]
    s_bf = state.astype(BF16)
    row_head = _div_pow2(lax.broadcasted_iota(jnp.int32, (DN_ROWS, 1), 0), DN_CHUNK)
    v_new = u_st - _pick_head_block(_dot(w_st.astype(BF16), s_bf), row_head)
    v_new_bf = v_new.astype(BF16)
    o_st = (_pick_head_block(_dot((q_st * egc).astype(BF16), s_bf), row_head)
            + _dot(attn.astype(BF16), v_new_bf))
    k_dec = (k_st * jnp.exp(glast_st - gc_st)).astype(BF16)
    zero = jnp.zeros_like(v_new_bf)
    v_wide = jnp.concatenate([jnp.where(row_head == h, v_new_bf, zero) for h in range(DN_HEADS)],
                             axis=1)
    state_scale = jnp.concatenate(
        [jnp.broadcast_to(jnp.exp(gc_all[DN_CHUNK - 1:DN_CHUNK, ALPHA_LANE + h:ALPHA_LANE + h + 1]),
                          (1, DN_HEAD_DIM)) for h in range(DN_HEADS)], axis=1)
    state_ref[...] = state * state_scale + _dot_tn(k_dec, v_wide)

    gate = gate_ref[0]
    outs = []
    for h in range(DN_HEADS):
        o_h = _rms(o_st[h * DN_CHUNK:(h + 1) * DN_CHUNK, :], onorm_ref[...])
        g_h = gate[:, h * DN_HEAD_DIM:(h + 1) * DN_HEAD_DIM]
        outs.append(o_h * (g_h * _sigmoid(g_h)))
    o_ref[0] = jnp.concatenate(outs, axis=1).astype(o_ref.dtype)


def gated_deltanet(z, ba_block, conv_w, a_log, dt_bias, o_norm):
    b, s, _ = z.shape
    lane_row = lambda vals, lane0: jnp.zeros((1, LANES), F32).at[0, lane0:lane0 + DN_HEADS].set(
        vals.astype(F32))
    col = lambda j: pl.BlockSpec((1, DN_CHUNK, DN_WIDTH), lambda bi, ci: (bi, ci, j))
    return pl.pallas_call(
        _deltanet_kernel,
        grid=(b, s // DN_CHUNK),
        in_specs=[col(1), col(2), col(3), col(4),
                  pl.BlockSpec((1, DN_CHUNK, LANES), lambda bi, ci: (bi, ci, ba_block)),
                  _const_spec(conv_w.shape), _const_spec((1, LANES)), _const_spec((1, LANES)),
                  _const_spec((1, DN_HEAD_DIM))],
        out_specs=pl.BlockSpec((1, DN_CHUNK, DN_WIDTH), lambda bi, ci: (bi, ci, 0)),
        out_shape=jax.ShapeDtypeStruct((b, s, DN_WIDTH), BF16),
        scratch_shapes=[pltpu.VMEM((DN_HEAD_DIM, DN_WIDTH), F32),
                        pltpu.VMEM((8, 3 * DN_WIDTH), F32)],
        compiler_params=_params("parallel", "arbitrary"),
    )(z, z, z, z, z, conv_w.astype(F32), lane_row(a_log, ALPHA_LANE), lane_row(dt_bias, ALPHA_LANE),
      o_norm.reshape(1, DN_HEAD_DIM).astype(F32))


def _compress_kernel(rk_ref, rv_ref, w1k_ref, pek_ref, w2k_ref, w1v_ref, pev_ref, w2v_ref,
                     ck_ref, cv_ref):
    n = rk_ref.shape[2]
    row = lax.broadcasted_iota(jnp.int32, (n, 1), 0)

    def one(r_ref, w1_ref, pe_ref, w2_ref, o_ref):
        w1 = w1_ref[...]
        pb = _dot(pe_ref[...], w1)
        bias = pb[0:1, :CMP_HIDDEN] + pb[1:2, CMP_HIDDEN:]
        for g in range(NSA_GROUPS):
            y = _dot(r_ref[0, g], w1)
            h = y[:, :CMP_HIDDEN] + pltpu.roll(y[:, CMP_HIDDEN:], n - 1, axis=0) + bias
            a = (h * _sigmoid(h)).astype(BF16)
            o_ref[0, g] = jnp.where(row < n - 1, _dot(a, w2_ref[...]), 0.0).astype(o_ref.dtype)

    one(rk_ref, w1k_ref, pek_ref, w2k_ref, ck_ref)
    one(rv_ref, w1v_ref, pev_ref, w2v_ref, cv_ref)


def compress_kv(rk, rv, w1k, pek, w2k, w1v, pev, w2v):
    b, g, n, w = rk.shape
    blk = pl.BlockSpec((1, g, n, w), lambda bi: (bi, 0, 0, 0))
    oblk = pl.BlockSpec((1, g, n, NSA_HEAD_DIM), lambda bi: (bi, 0, 0, 0))
    return pl.pallas_call(
        _compress_kernel,
        grid=(b,),
        in_specs=[blk, blk, _const_spec(w1k.shape), _const_spec(pek.shape), _const_spec(w2k.shape),
                  _const_spec(w1v.shape), _const_spec(pev.shape), _const_spec(w2v.shape)],
        out_specs=[oblk, oblk],
        out_shape=[jax.ShapeDtypeStruct((b, g, n, NSA_HEAD_DIM), BF16)] * 2,
        compiler_params=_params("parallel"),
    )(rk, rv, w1k, pek, w2k, w1v, pev, w2v)


def _aug_keys(k, pos):
    lane = lax.broadcasted_iota(jnp.int32, k.shape, 1)
    hi = (_div_pow2(pos, POS_SPLIT) * POS_SPLIT).astype(F32)
    lo = (pos & (POS_SPLIT - 1)).astype(F32)
    cols = jnp.where(lane < SLOPE_PIECES, hi, jnp.where(lane < 2 * SLOPE_PIECES, lo, 0.0))
    return jnp.concatenate([k, cols.astype(BF16)], axis=1)


def _nsa_kernel(q_ref, gate_ref, slope_ref, ck_ref, cv_ref, ks_ref, vs_ref, kw_ref, vw_ref,
                ovl_ref, o_ref, kc_aug, ks_aug, kw_aug, sel_bias, *, top_n):
    i = pl.program_id(2)
    tq = q_ref.shape[1]
    rows = NSA_REP * tq
    n_cmp = ck_ref.shape[2]
    n_slc = ovl_ref.shape[0]
    s_len = ks_ref.shape[2]
    q0 = i * tq

    @pl.when(i == 0)
    def _():
        cpos = lax.broadcasted_iota(jnp.int32, (n_cmp, 1), 0) * CMP_STRIDE + (CMP_LEN - 1)
        kc_aug[...] = _aug_keys(ck_ref[0, 0], cpos)

        def fill(c, carry):
            r0 = pl.multiple_of(c * SLC_CHUNK, SLC_CHUNK)
            pos = r0 + lax.broadcasted_iota(jnp.int32, (SLC_CHUNK, 1), 0)
            ks_aug[pl.ds(r0, SLC_CHUNK), :] = _aug_keys(ks_ref[0, 0, pl.ds(r0, SLC_CHUNK), :], pos)
            kw_aug[pl.ds(r0, SLC_CHUNK), :] = _aug_keys(kw_ref[0, 0, pl.ds(r0, SLC_CHUNK), :], pos)
            return carry

        lax.fori_loop(0, s_len // SLC_CHUNK, fill, 0)

    q = q_ref[0]
    qs = jnp.concatenate([q[:, r * NSA_HEAD_DIM:(r + 1) * NSA_HEAD_DIM] for r in range(NSA_REP)],
                         axis=0)
    qs = (qs.astype(F32) * (NSA_HEAD_DIM ** -0.5 * LOG2E)).astype(BF16)
    slope_cols = jnp.concatenate(
        [jnp.broadcast_to(slope_ref[0, r:r + 1, 0:NSA_HEAD_DIM], (tq, NSA_HEAD_DIM))
         for r in range(NSA_REP)], axis=0).astype(BF16)
    q_aug = jnp.concatenate([qs, slope_cols], axis=1)
    t_lane = q0 + lax.broadcasted_iota(jnp.int32, (1, tq), 1)

    def all_heads(x):
        return jnp.concatenate([x] * NSA_REP, axis=1)

    def softmax_keys(s):
        p = jnp.exp2(s - jnp.max(s, axis=0, keepdims=True))
        return p, jnp.sum(p, axis=0, keepdims=True)

    span = min(WINDOW + tq, s_len)
    w0 = pl.multiple_of(jnp.maximum(q0 + tq - span, 0), tq)
    dist_w = t_lane - (w0 + lax.broadcasted_iota(jnp.int32, (span, 1), 0))
    band = jnp.where(dist_w >= 0, jnp.where(dist_w < WINDOW, 0.0, NEG_INF), NEG_INF)
    pw, l_w = softmax_keys(_dot_nt(kw_aug[pl.ds(w0, span), :], q_aug) + all_heads(band))
    o_win = _dot(vw_ref[0, 0, :, pl.ds(w0, span)], pw.astype(BF16)) / l_w

    cend = lax.broadcasted_iota(jnp.int32, (n_cmp, 1), 0) * CMP_STRIDE + (CMP_LEN - 1)
    e, l_c = softmax_keys(_dot_nt(kc_aug[...], q_aug)
                          + all_heads(jnp.where(cend <= t_lane, 0.0, NEG_INF)))
    p_cmp = e * all_heads(jnp.where(t_lane >= CMP_LEN - 1, 1.0, 0.0)) / l_c
    o_cmp = _dot(cv_ref[0, 0], p_cmp.astype(BF16))

    p_sum = p_cmp[:, 0:tq]
    for r in range(1, NSA_REP):
        p_sum = p_sum + p_cmp[:, r * tq:(r + 1) * tq]
    imp = _dot(ovl_ref[...], p_sum, precision=HIGHEST)
    blk = lax.broadcasted_iota(jnp.int32, (n_slc, tq), 0)
    tl = q0 + lax.broadcasted_iota(jnp.int32, (n_slc, tq), 1)
    cur = _div_pow2(tl, SLC_LEN)
    forced = (blk == 0) | (blk == cur) | (blk == cur - 1)
    val = jnp.where(forced, FORCE_SCORE, jnp.where(blk * SLC_LEN <= tl, imp, -1.0))
    rank = jnp.zeros((n_slc, tq), F32)
    for m in range(n_slc):
        vm = val[m:m + 1, :]
        rank = rank + jnp.where(vm > val, 1.0, jnp.where(vm == val, jnp.where(blk > m, 1.0, 0.0), 0.0))
    sel_bias[...] = jnp.where(rank < top_n, 0.0, NEG_INF)

    blocks_per_chunk = SLC_CHUNK // SLC_LEN
    key_off = lax.broadcasted_iota(jnp.int32, (SLC_CHUNK, 1), 0)

    def slc_step(j, carry):
        m_run, l_run, acc = carry
        k0 = pl.multiple_of(j * SLC_CHUNK, SLC_CHUNK)
        sb = sel_bias[pl.ds(pl.multiple_of(j * blocks_per_chunk, blocks_per_chunk),
                            blocks_per_chunk), :]
        bias = jnp.concatenate([jnp.broadcast_to(sb[n:n + 1, :], (SLC_LEN, tq))
                                for n in range(blocks_per_chunk)], axis=0)
        bias = jnp.where(k0 + key_off <= t_lane, bias, NEG_INF)
        bias = jnp.concatenate([bias] * (NSA_REP // SLC_STREAMS), axis=1)
        keys = ks_aug[pl.ds(k0, SLC_CHUNK), :]
        vals = vs_ref[0, 0, :, pl.ds(k0, SLC_CHUNK)]
        sls = [slice(h * rows // SLC_STREAMS, (h + 1) * rows // SLC_STREAMS)
               for h in range(SLC_STREAMS)]
        ss = [_dot_nt(keys, q_aug[sl, :]) + bias for sl in sls]
        ms, ls, alphas, ps = [], [], [], []
        for s, sl in zip(ss, sls):
            m_new = jnp.maximum(m_run[:, sl], jnp.max(s, axis=0, keepdims=True))
            alpha = jnp.exp2(m_run[:, sl] - m_new)
            p = jnp.exp2(s - m_new)
            ms.append(m_new)
            alphas.append(alpha)
            ls.append(alpha * l_run[:, sl] + jnp.sum(p, axis=0, keepdims=True))
            ps.append(p.astype(BF16))
        accs = [alpha * acc[:, sl] + _dot(vals, p) for alpha, p, sl in zip(alphas, ps, sls)]
        return (jnp.concatenate(ms, axis=1), jnp.concatenate(ls, axis=1),
                jnp.concatenate(accs, axis=1))

    n_chunks = (q0 + tq + SLC_CHUNK - 1) // SLC_CHUNK
    m_fin, l_fin, acc = lax.fori_loop(
        0, n_chunks, slc_step,
        (jnp.full((1, rows), NEG_INF, F32), jnp.zeros((1, rows), F32),
         jnp.zeros((NSA_HEAD_DIM, rows), F32)))
    o_slc = acc / l_fin

    gates = _sigmoid(gate_ref[0]).T

    def gate_rows(br):
        return jnp.concatenate([gates[3 * r + br:3 * r + br + 1, :] for r in range(NSA_REP)], axis=1)

    o = gate_rows(0) * o_cmp + gate_rows(1) * o_slc + gate_rows(2) * o_win
    o = jnp.concatenate([o[:, r * tq:(r + 1) * tq] for r in range(NSA_REP)], axis=0)
    o_ref[0] = o.T.astype(o_ref.dtype)


def nsa_attention(z, zg, ck, cv, ks, vs, kw, vw, tq=128):
    b, s, _ = z.shape
    n_cmp = ck.shape[2]
    n_slc = s // SLC_LEN
    tq = min(tq, s)
    c_lo = jnp.arange(n_cmp) * CMP_STRIDE
    s_lo = jnp.arange(n_slc) * SLC_LEN
    ovl = (jnp.clip(jnp.minimum(c_lo[None, :] + CMP_LEN, s_lo[:, None] + SLC_LEN)
                    - jnp.maximum(c_lo[None, :], s_lo[:, None]), 0, None).astype(F32) / CMP_LEN)
    ovl = ovl * (jnp.arange(n_cmp) < n_cmp - 1)[None, :]
    hd = jnp.arange(1, NSA_HEADS + 1, dtype=F32)
    rest = jnp.exp2(-8.0 * hd / NSA_HEADS) * LOG2E
    pieces = []
    for _ in range(SLOPE_PIECES):
        piece = rest.astype(BF16).astype(F32)
        pieces.append(piece)
        rest = rest - piece
    pieces = jnp.stack(pieces * 2, axis=-1).reshape(NSA_GROUPS, NSA_REP, 2 * SLOPE_PIECES)
    slope_tab = jnp.zeros((NSA_GROUPS, 8, LANES), F32).at[:, :NSA_REP, :2 * SLOPE_PIECES].set(pieces)
    k_spec = pl.BlockSpec((1, 1, s, NSA_HEAD_DIM), lambda bi, g, i: (bi, g, 0, 0))
    v_spec = pl.BlockSpec((1, 1, NSA_HEAD_DIM, s), lambda bi, g, i: (bi, g, 0, 0))
    ck_spec = pl.BlockSpec((1, 1, n_cmp, NSA_HEAD_DIM), lambda bi, g, i: (bi, g, 0, 0))
    cv_spec = pl.BlockSpec((1, 1, NSA_HEAD_DIM, n_cmp), lambda bi, g, i: (bi, g, 0, 0))
    gw = NSA_REP * NSA_HEAD_DIM
    return pl.pallas_call(
        functools.partial(_nsa_kernel, top_n=min(SLC_TOP, n_slc)),
        grid=(b, NSA_GROUPS, s // tq),
        in_specs=[pl.BlockSpec((1, tq, gw), lambda bi, g, i: (bi, i, g)),
                  pl.BlockSpec((1, tq, LANES), lambda bi, g, i: (bi, i, g)),
                  pl.BlockSpec((1, 8, LANES), lambda bi, g, i: (g, 0, 0)),
                  ck_spec, cv_spec, k_spec, v_spec, k_spec, v_spec,
                  _const_spec(ovl.shape)],
        out_specs=pl.BlockSpec((1, tq, gw), lambda bi, g, i: (bi, i, g)),
        out_shape=jax.ShapeDtypeStruct((b, s, NSA_GROUPS * gw), BF16),
        scratch_shapes=[pltpu.VMEM((n_cmp, 2 * NSA_HEAD_DIM), BF16),
                        pltpu.VMEM((s, 2 * NSA_HEAD_DIM), BF16),
                        pltpu.VMEM((s, 2 * NSA_HEAD_DIM), BF16),
                        pltpu.VMEM((n_slc, tq), F32)],
        compiler_params=_params("parallel", "parallel", "arbitrary"),
    )(z, zg, slope_tab, ck, cv, ks, vs, kw, vw, ovl)


def _pad_cols(w, n):
    return jnp.pad(w, ((0, 0), (0, n - w.shape[1])))


def pool_delta_layer(x, ln, w_in, pool_w, pool_scale, conv_w, a_log, dt_bias, o_norm, w_out):
    b, s, d = x.shape
    main = POOL_WIDTH + 4 * DN_WIDTH
    w_all = jnp.concatenate([w_in[:, :main], _pad_cols(w_in[:, main:], LANES)], axis=1).astype(BF16)
    (z,) = norm_matmul(x.reshape(b * s, d), ln, [w_all], [F32])
    z = z.reshape(b, s, main + LANES)
    y_pool = pool_mixer(z, pool_w.astype(BF16), pool_scale)
    y_dn = gated_deltanet(z, main // LANES, conv_w, a_log, dt_bias, o_norm)
    w_out = w_out.astype(BF16)
    return matmul_res([y_pool.reshape(b * s, POOL_WIDTH), y_dn.reshape(b * s, DN_WIDTH)],
                      [w_out[:POOL_WIDTH], w_out[POOL_WIDTH:]], x.reshape(b * s, d)).reshape(b, s, d)


def _cmp_weights(pe, w1, w2):
    half = (CMP_LEN // 2) * NSA_HEAD_DIM
    w1_pair = jnp.concatenate([w1[:half], w1[half:]], axis=1).astype(BF16)
    pe_rows = jnp.zeros((8, half), F32).at[0:2].set(pe.reshape(2, half)).astype(BF16)
    return w1_pair, pe_rows, w2.astype(BF16)


def nsa_layer(x, ln, w_in, pe_k, w1_k, w2_k, pe_v, w1_v, w2_v, w_out):
    b, s, d = x.shape
    g_, r_, hd = NSA_GROUPS, NSA_REP, NSA_HEAD_DIM
    kvw = g_ * hd
    main = d + 6 * kvw
    wg = w_in[:, main:].reshape(d, g_, r_ * 3)
    wg = jnp.pad(wg, ((0, 0), (0, 0), (0, LANES - r_ * 3))).reshape(d, g_ * LANES)
    z, zg = norm_matmul(x.reshape(b * s, d), ln, [w_in[:, :main].astype(BF16), wg.astype(BF16)],
                        [BF16, F32])
    z = z.reshape(b, s, main)
    zg = zg.reshape(b, s, g_ * LANES)

    def group_major(j):
        return jnp.transpose(z[..., d + j * kvw:d + (j + 1) * kvw].reshape(b, s, g_, hd), (0, 2, 1, 3))

    n_str = s // CMP_STRIDE
    rk = group_major(0).reshape(b, g_, n_str, CMP_STRIDE * hd)
    rv = group_major(1).reshape(b, g_, n_str, CMP_STRIDE * hd)
    ck, cv = compress_kv(rk, rv, *_cmp_weights(pe_k, w1_k, w2_k), *_cmp_weights(pe_v, w1_v, w2_v))

    def channel_major(j):
        return jnp.transpose(z[..., d + j * kvw:d + (j + 1) * kvw].reshape(b, s, g_, hd), (0, 2, 3, 1))

    o = nsa_attention(z, zg, ck, jnp.swapaxes(cv, 2, 3), group_major(2), channel_major(3),
                      group_major(4), channel_major(5))
    return matmul_res([o.reshape(b * s, d)], [w_out.astype(BF16)], x.reshape(b * s, d)).reshape(b, s, d)


def kernel(x, mem, a_ln, a_w_in, a_pool_w, a_pool_scale, a_conv_w, a_a_log, a_dt_bias, a_o_norm, a_w_out, c_ln, c_w_in, c_pe_k, c_w1_k, c_w2_k, c_pe_v, c_w1_v, c_w2_v, c_w_out, xa_ln, xa_mem_ln, xa_wq, xa_wk, xa_wv, xa_wo, ff_ln, ff_w1, ff_w2, final_ln):
    b, s, d = x.shape
    depth = xa_ln.shape[0]
    mem2d = mem.reshape(b * mem.shape[1], d)
    for l in range(depth):
        i = l // 2
        if l % 2 == 0:
            x = pool_delta_layer(x, a_ln[i], a_w_in[i], a_pool_w[i], a_pool_scale[i], a_conv_w[i],
                                 a_a_log[i], a_dt_bias[i], a_o_norm[i], a_w_out[i])
        else:
            x = nsa_layer(x, c_ln[i], c_w_in[i], c_pe_k[i], c_w1_k[i], c_w2_k[i],
                          c_pe_v[i], c_w1_v[i], c_w2_v[i], c_w_out[i])
        mk, mv = norm_matmul(mem2d, xa_mem_ln[l], [xa_wk[l].astype(BF16), xa_wv[l].astype(BF16)],
                             [BF16, BF16])
        x = xattn_sublayer(x, xa_ln[l], xa_wq[l].astype(BF16), mk, mv, xa_wo[l].astype(BF16))
        x = mlp_sublayer(x.reshape(b * s, d), ff_ln[l], ff_w1[l].astype(BF16), ff_w2[l].astype(BF16),
                         final_ln, final_norm=(l == depth - 1)).reshape(b, s, d)
    return x
```

```python
import functools
import math

import jax
import jax.numpy as jnp
from jax import lax
from jax.experimental import pallas as pl
from jax.experimental.pallas import tpu as pltpu

F32 = jnp.float32
BF16 = jnp.bfloat16
HIGHEST = lax.Precision.HIGHEST

LANES = 128
VMEM_LIMIT = 56 * 1024 * 1024

EPS = 1e-6
NEG_INF = -1e30
FORCE_SCORE = 1e4

POOL_WINDOWS = (2, 4, 8, 16)
POOL_GROUP_DIM = 128
POOL_WIDTH = 512
DN_HEADS = 4
DN_HEAD_DIM = 128
DN_WIDTH = 512
DN_CONV = 4
DN_CHUNK = 64

NSA_HEAD_DIM = 64
NSA_GROUPS = 4
NSA_REP = 4
NSA_HEADS = 16
CMP_LEN = 32
CMP_STRIDE = 16
CMP_HIDDEN = 128
SLC_LEN = 64
SLC_TOP = 16
WINDOW = 512
SLC_CHUNK = 512
SLC_KEY_BLOCK = 256
RANK_GROUP = 8
V_PAD_ROWS = 16
LOG2E = 1.4426950408889634
SLOPE_PIECES = 3
POS_SPLIT = 64

XA_HEADS = 4
XA_HEAD_DIM = 256


def _params(*sem):
    return pltpu.CompilerParams(dimension_semantics=sem, vmem_limit_bytes=VMEM_LIMIT)


def _const_spec(shape):
    nd = len(shape)
    return pl.BlockSpec(shape, lambda *_: (0,) * nd)


def _rms(x, g):
    return x * lax.rsqrt(jnp.mean(x * x, axis=-1, keepdims=True) + EPS) * g


def _sigmoid(x):
    return 1.0 / (1.0 + jnp.exp(-x))


def _div_pow2(x, n):
    return lax.shift_right_logical(x, jnp.int32(int(math.log2(n))))


def _dot(a, b, precision=None):
    return jnp.dot(a, b, preferred_element_type=F32, precision=precision)


def _dot_nt(a, b, precision=None):
    return lax.dot_general(a, b, (((1,), (1,)), ((), ())), preferred_element_type=F32,
                           precision=precision)


def _dot_tn(a, b):
    return lax.dot_general(a, b, (((0,), (0,)), ((), ())), preferred_element_type=F32)


def _norm_matmul_kernel(x_ref, g_ref, *refs):
    n = len(refs) // 2
    hb = _rms(x_ref[...], g_ref[...]).astype(BF16)
    for w_ref, o_ref in zip(refs[:n], refs[n:]):
        o_ref[...] = _dot(hb, w_ref[...]).astype(o_ref.dtype)


def norm_matmul(x2d, g, ws, out_dtypes, tm=512):
    t, d = x2d.shape
    tm = min(tm, t)
    return pl.pallas_call(
        _norm_matmul_kernel,
        grid=(t // tm,),
        in_specs=[pl.BlockSpec((tm, d), lambda i: (i, 0)), _const_spec((1, d))]
        + [_const_spec(w.shape) for w in ws],
        out_specs=[pl.BlockSpec((tm, w.shape[1]), lambda i: (i, 0)) for w in ws],
        out_shape=[jax.ShapeDtypeStruct((t, w.shape[1]), dt) for w, dt in zip(ws, out_dtypes)],
        compiler_params=_params("parallel"),
    )(x2d, g.reshape(1, d).astype(F32), *ws)


def _matmul_res_kernel(*refs):
    n = (len(refs) - 2) // 2
    res_ref, o_ref = refs[2 * n], refs[2 * n + 1]
    acc = res_ref[...]
    for a_ref, w_ref in zip(refs[:n], refs[n:2 * n]):
        acc = acc + _dot(a_ref[...], w_ref[...])
    o_ref[...] = acc


def matmul_res(a_list, w_list, res2d, tm=512):
    t, d = res2d.shape
    tm = min(tm, t)
    return pl.pallas_call(
        _matmul_res_kernel,
        grid=(t // tm,),
        in_specs=[pl.BlockSpec((tm, a.shape[1]), lambda i: (i, 0)) for a in a_list]
        + [_const_spec(w.shape) for w in w_list]
        + [pl.BlockSpec((tm, d), lambda i: (i, 0))],
        out_specs=pl.BlockSpec((tm, d), lambda i: (i, 0)),
        out_shape=jax.ShapeDtypeStruct((t, d), F32),
        compiler_params=_params("parallel"),
    )(*a_list, *w_list, res2d)


def _mlp_kernel(x_ref, g_ref, w1_ref, w2_ref, gf_ref, o_ref, *, ff_chunk, final_norm):
    x = x_ref[...]
    hb = _rms(x, g_ref[...]).astype(BF16)
    acc = x
    for c in range(w1_ref.shape[1] // ff_chunk):
        a = jnp.maximum(_dot(hb, w1_ref[:, c * ff_chunk:(c + 1) * ff_chunk]), 0.0)
        acc = acc + _dot((a * a).astype(BF16), w2_ref[c * ff_chunk:(c + 1) * ff_chunk, :])
    if final_norm:
        acc = _rms(acc, gf_ref[...])
    o_ref[...] = acc


def mlp_sublayer(x2d, g, w1, w2, gf, final_norm, tm=512, ff_chunk=1024):
    t, d = x2d.shape
    tm = min(tm, t)
    return pl.pallas_call(
        functools.partial(_mlp_kernel, ff_chunk=ff_chunk, final_norm=final_norm),
        grid=(t // tm,),
        in_specs=[pl.BlockSpec((tm, d), lambda i: (i, 0)), _const_spec((1, d)),
                  _const_spec(w1.shape), _const_spec(w2.shape), _const_spec((1, d))],
        out_specs=pl.BlockSpec((tm, d), lambda i: (i, 0)),
        out_shape=jax.ShapeDtypeStruct((t, d), F32),
        compiler_params=_params("parallel"),
    )(x2d, g.reshape(1, d).astype(F32), w1, w2, gf.reshape(1, d).astype(F32))


def _xattn_kernel(x_ref, g_ref, wq_ref, k_ref, v_ref, wo_ref, o_ref):
    x = x_ref[0]
    hb = _rms(x, g_ref[...]).astype(BF16)
    q = (_dot(hb, wq_ref[...]) * (XA_HEAD_DIM ** -0.5)).astype(BF16)
    heads = []
    for h in range(XA_HEADS):
        sl = slice(h * XA_HEAD_DIM, (h + 1) * XA_HEAD_DIM)
        s = _dot_nt(q[:, sl], k_ref[:, sl])
        p = jnp.exp(s - jnp.max(s, axis=-1, keepdims=True))
        l = jnp.sum(p, axis=-1, keepdims=True)
        heads.append((_dot(p.astype(BF16), v_ref[:, sl]) / l).astype(BF16))
    o_ref[0] = x + _dot(jnp.concatenate(heads, axis=1), wo_ref[...])


def xattn_sublayer(x, g, wq, k2d, v2d, wo, tq=512):
    b, s, d = x.shape
    tq = min(tq, s)
    m = k2d.shape[0] // b
    return pl.pallas_call(
        _xattn_kernel,
        grid=(b, s // tq),
        in_specs=[pl.BlockSpec((1, tq, d), lambda bi, i: (bi, i, 0)), _const_spec((1, d)),
                  _const_spec(wq.shape),
                  pl.BlockSpec((m, d), lambda bi, i: (bi, 0)),
                  pl.BlockSpec((m, d), lambda bi, i: (bi, 0)),
                  _const_spec(wo.shape)],
        out_specs=pl.BlockSpec((1, tq, d), lambda bi, i: (bi, i, 0)),
        out_shape=jax.ShapeDtypeStruct((b, s, d), F32),
        compiler_params=_params("parallel", "parallel"),
    )(x, g.reshape(1, d).astype(F32), wq, k2d, v2d, wo)


POOL_HALO = 16


def _pool_kernel(u_ref, halo_ref, w_ref, scale_ref, o_ref):
    i = pl.program_id(1)
    ts = u_ref.shape[1]
    u = u_ref[0]
    halo = jnp.where(i == 0, 0.0, halo_ref[0])
    ext = jnp.concatenate([halo, u], axis=0)
    sums = [None] * len(POOL_WINDOWS)
    cur = ext
    for gi, win in enumerate(POOL_WINDOWS):
        cur = cur[:, (POOL_GROUP_DIM if gi else 0):]
        cur = cur + pltpu.roll(cur, win // 2, axis=0)
        sums[gi] = cur[POOL_HALO:, :POOL_GROUP_DIM]
    pos1 = (i * ts + 1 + lax.broadcasted_iota(jnp.int32, (ts, 1), 0)).astype(F32)
    outs = []
    for gi, win in enumerate(POOL_WINDOWS):
        ug = u[:, gi * POOL_GROUP_DIM:(gi + 1) * POOL_GROUP_DIM]
        y = sums[gi] / jnp.minimum(pos1, float(win)) - ug
        outs.append(_dot(y.astype(BF16), w_ref[gi]))
    o_ref[0] = (jnp.concatenate(outs, axis=1) * scale_ref[...]).astype(o_ref.dtype)


def pool_mixer(z, pool_w, pool_scale, ts=512):
    b, s, _ = z.shape
    ts = min(ts, s)
    hb = ts // POOL_HALO
    return pl.pallas_call(
        _pool_kernel,
        grid=(b, s // ts),
        in_specs=[pl.BlockSpec((1, ts, POOL_WIDTH), lambda bi, i: (bi, i, 0)),
                  pl.BlockSpec((1, POOL_HALO, POOL_WIDTH),
                               lambda bi, i: (bi, jnp.maximum(i * hb - 1, 0), 0)),
                  _const_spec(pool_w.shape), _const_spec((1, POOL_WIDTH))],
        out_specs=pl.BlockSpec((1, ts, POOL_WIDTH), lambda bi, i: (bi, i, 0)),
        out_shape=jax.ShapeDtypeStruct((b, s, POOL_WIDTH), BF16),
        compiler_params=_params("parallel", "parallel"),
    )(z, z, pool_w, pool_scale.reshape(1, POOL_WIDTH).astype(F32))


DN_ROWS = DN_HEADS * DN_CHUNK
BETA_LANE = 0
ALPHA_LANE = DN_HEADS


def _stack_heads(x):
    return jnp.concatenate([x[:, h * DN_HEAD_DIM:(h + 1) * DN_HEAD_DIM] for h in range(DN_HEADS)],
                           axis=0)


def _stack_cols(x, lane0):
    return jnp.concatenate([x[:, lane0 + h:lane0 + h + 1] for h in range(DN_HEADS)], axis=0)


def _pick_head_block(wide, row_head):
    out = jnp.zeros((DN_ROWS, DN_HEAD_DIM), F32)
    for h in range(DN_HEADS):
        out = jnp.where(row_head == h, wide[:, h * DN_HEAD_DIM:(h + 1) * DN_HEAD_DIM], out)
    return out


def _deltanet_kernel(q_ref, k_ref, v_ref, gate_ref, ba_ref, cw_ref, alog_ref, dtb_ref, onorm_ref,
                     o_ref, state_ref, tail_ref):
    c = pl.program_id(1)

    @pl.when(c == 0)
    def _():
        state_ref[...] = jnp.zeros_like(state_ref)
        tail_ref[...] = jnp.zeros_like(tail_ref)

    x3 = jnp.concatenate([q_ref[0], k_ref[0], v_ref[0]], axis=1)
    ext = jnp.concatenate([tail_ref[...], x3], axis=0)
    tail_ref[...] = x3[DN_CHUNK - 8:, :]
    cw = cw_ref[...]
    y = cw[DN_CONV - 1:DN_CONV, :] * ext
    for j in range(1, DN_CONV):
        y = y + cw[DN_CONV - 1 - j:DN_CONV - j, :] * pltpu.roll(ext, j, axis=0)
    y = y[8:, :]
    y = y * _sigmoid(y)

    def l2n(a):
        return a * lax.rsqrt(jnp.sum(a * a, axis=-1, keepdims=True) + EPS)

    q_st = l2n(_stack_heads(y[:, :DN_WIDTH])) * (DN_HEAD_DIM ** -0.5)
    k_st = l2n(_stack_heads(y[:, DN_WIDTH:2 * DN_WIDTH]))
    v_st = _stack_heads(y[:, 2 * DN_WIDTH:])

    ba = ba_ref[0]
    beta_all = _sigmoid(ba)
    sp_in = ba + dtb_ref[...]
    softplus = jnp.maximum(sp_in, 0.0) + jnp.log(1.0 + jnp.exp(-jnp.abs(sp_in)))
    g_all = -jnp.exp(alog_ref[...]) * softplus
    ri = lax.broadcasted_iota(jnp.int32, (DN_CHUNK, DN_CHUNK), 0)
    ci = lax.broadcasted_iota(jnp.int32, (DN_CHUNK, DN_CHUNK), 1)
    gc_all = _dot((ri >= ci).astype(F32), g_all, precision=HIGHEST)
    beta_st = _stack_cols(beta_all, BETA_LANE)
    gc_st = _stack_cols(gc_all, ALPHA_LANE)
    glast_st = jnp.concatenate(
        [jnp.broadcast_to(gc_all[DN_CHUNK - 1:DN_CHUNK, ALPHA_LANE + h:ALPHA_LANE + h + 1],
                          (DN_CHUNK, 1)) for h in range(DN_HEADS)], axis=0)

    rr = lax.broadcasted_iota(jnp.int32, (DN_ROWS, DN_ROWS), 0)
    cc = lax.broadcasted_iota(jnp.int32, (DN_ROWS, DN_ROWS), 1)
    same_head = _div_pow2(rr, DN_CHUNK) == _div_pow2(cc, DN_CHUNK)
    causal = same_head & (rr >= cc)
    strict = same_head & (rr > cc)
    gcb = jnp.broadcast_to(gc_st, (DN_ROWS, DN_ROWS))
    decay = jnp.where(causal, jnp.exp(jnp.where(causal, gcb - gcb.T, 0.0)), 0.0)

    kb_st = k_st * beta_st
    k_bf = k_st.astype(BF16)
    a_low = jnp.where(strict, _dot_nt(kb_st.astype(BF16), k_bf) * decay, 0.0)
    lane_head = _div_pow2(lax.broadcasted_iota(jnp.int32, (DN_CHUNK, DN_ROWS), 1), DN_CHUNK)

    def block_diag(w):
        return jnp.concatenate([jnp.where(lane_head == h, w, 0.0) for h in range(DN_HEADS)], axis=0)

    def hi_lo(x):
        hi = x.astype(BF16).astype(F32)
        return hi, x - hi

    def times_p(x, p_wide):
        p_hi, p_lo = hi_lo(p_wide)
        d_hi, d_lo = block_diag(p_hi).astype(BF16), block_diag(p_lo).astype(BF16)
        x_hi, x_lo = hi_lo(x)
        n = x.shape[0]
        top = _dot(jnp.concatenate([x_hi, x_lo], axis=0).astype(BF16), d_hi)
        return top[:n] + top[n:] + _dot(x_hi.astype(BF16), d_lo)

    p = -(a_low[0:DN_CHUNK] + a_low[DN_CHUNK:2 * DN_CHUNK]
          + a_low[2 * DN_CHUNK:3 * DN_CHUNK] + a_low[3 * DN_CHUNK:])
    wr = lax.broadcasted_iota(jnp.int32, (DN_CHUNK, DN_ROWS), 0)
    wc = lax.broadcasted_iota(jnp.int32, (DN_CHUNK, DN_ROWS), 1)
    t_wide = jnp.where(wr == (wc & (DN_CHUNK - 1)), 1.0, 0.0) + p
    n_sq = int(math.log2(DN_CHUNK)) - 1
    for j in range(n_sq):
        if j == 0:
            p = times_p(p, p)
        else:
            both = times_p(jnp.concatenate([p, t_wide], axis=0), p)
            p, t_wide = both[:DN_CHUNK], t_wide + both[DN_CHUNK:]
    t_wide = t_wide + times_p(t_wide, p)
    t_bf = block_diag(t_wide).astype(BF16)
    egc = jnp.exp(gc_st)
    wu = _dot(t_bf, jnp.concatenate([(kb_st * egc).astype(BF16), (v_st * beta_st).astype(BF16)],
                                    axis=1))
    w_st, u_st = wu[:, :DN_HEAD_DIM], wu[:, DN_HEAD_DIM:]
    attn = jnp.where(causal, _dot_nt(q_st.astype(BF16), k_bf) * decay, 0.0)

    state = state_ref[...]
    s_bf = state.astype(BF16)
    row_head = _div_pow2(lax.broadcasted_iota(jnp.int32, (DN_ROWS, 1), 0), DN_CHUNK)
    v_new = u_st - _pick_head_block(_dot(w_st.astype(BF16), s_bf), row_head)
    v_new_bf = v_new.astype(BF16)
    o_st = (_pick_head_block(_dot((q_st * egc).astype(BF16), s_bf), row_head)
            + _dot(attn.astype(BF16), v_new_bf))
    k_dec = (k_st * jnp.exp(glast_st - gc_st)).astype(BF16)
    zero = jnp.zeros_like(v_new_bf)
    v_wide = jnp.concatenate([jnp.where(row_head == h, v_new_bf, zero) for h in range(DN_HEADS)],
                             axis=1)
    state_scale = jnp.concatenate(
        [jnp.broadcast_to(jnp.exp(gc_all[DN_CHUNK - 1:DN_CHUNK, ALPHA_LANE + h:ALPHA_LANE + h + 1]),
                          (1, DN_HEAD_DIM)) for h in range(DN_HEADS)], axis=1)
    state_ref[...] = state * state_scale + _dot_tn(k_dec, v_wide)

    gate = gate_ref[0]
    outs = []
    for h in range(DN_HEADS):
        o_h = _rms(o_st[h * DN_CHUNK:(h + 1) * DN_CHUNK, :], onorm_ref[...])
        g_h = gate[:, h * DN_HEAD_DIM:(h + 1) * DN_HEAD_DIM]
        outs.append(o_h * (g_h * _sigmoid(g_h)))
    o_ref[0] = jnp.concatenate(outs, axis=1).astype(o_ref.dtype)


def gated_deltanet(z, ba_block, conv_w, a_log, dt_bias, o_norm):
    b, s, _ = z.shape
    lane_row = lambda vals, lane0: jnp.zeros((1, LANES), F32).at[0, lane0:lane0 + DN_HEADS].set(
        vals.astype(F32))
    col = lambda j: pl.BlockSpec((1, DN_CHUNK, DN_WIDTH), lambda bi, ci: (bi, ci, j))
    return pl.pallas_call(
        _deltanet_kernel,
        grid=(b, s // DN_CHUNK),
        in_specs=[col(1), col(2), col(3), col(4),
                  pl.BlockSpec((1, DN_CHUNK, LANES), lambda bi, ci: (bi, ci, ba_block)),
                  _const_spec(conv_w.shape), _const_spec((1, LANES)), _const_spec((1, LANES)),
                  _const_spec((1, DN_HEAD_DIM))],
        out_specs=pl.BlockSpec((1, DN_CHUNK, DN_WIDTH), lambda bi, ci: (bi, ci, 0)),
        out_shape=jax.ShapeDtypeStruct((b, s, DN_WIDTH), BF16),
        scratch_shapes=[pltpu.VMEM((DN_HEAD_DIM, DN_WIDTH), F32),
                        pltpu.VMEM((8, 3 * DN_WIDTH), F32)],
        compiler_params=_params("parallel", "arbitrary"),
    )(z, z, z, z, z, conv_w.astype(F32), lane_row(a_log, ALPHA_LANE), lane_row(dt_bias, ALPHA_LANE),
      o_norm.reshape(1, DN_HEAD_DIM).astype(F32))


def _compress_kernel(rk_ref, rv_ref, w1k_ref, pek_ref, w2k_ref, w1v_ref, pev_ref, w2v_ref,
                     ck_ref, cv_ref):
    n = rk_ref.shape[2]
    row = lax.broadcasted_iota(jnp.int32, (n, 1), 0)

    def one(r_ref, w1_ref, pe_ref, w2_ref, o_ref):
        w1 = w1_ref[...]
        pb = _dot(pe_ref[...], w1)
        bias = pb[0:1, :CMP_HIDDEN] + pb[1:2, CMP_HIDDEN:]
        for g in range(NSA_GROUPS):
            y = _dot(r_ref[0, g], w1)
            h = y[:, :CMP_HIDDEN] + pltpu.roll(y[:, CMP_HIDDEN:], n - 1, axis=0) + bias
            a = (h * _sigmoid(h)).astype(BF16)
            o_ref[0, g] = jnp.where(row < n - 1, _dot(a, w2_ref[...]), 0.0).astype(o_ref.dtype)

    one(rk_ref, w1k_ref, pek_ref, w2k_ref, ck_ref)
    one(rv_ref, w1v_ref, pev_ref, w2v_ref, cv_ref)


def compress_kv(rk, rv, w1k, pek, w2k, w1v, pev, w2v):
    b, g, n, w = rk.shape
    blk = pl.BlockSpec((1, g, n, w), lambda bi: (bi, 0, 0, 0))
    oblk = pl.BlockSpec((1, g, n, NSA_HEAD_DIM), lambda bi: (bi, 0, 0, 0))
    return pl.pallas_call(
        _compress_kernel,
        grid=(b,),
        in_specs=[blk, blk, _const_spec(w1k.shape), _const_spec(pek.shape), _const_spec(w2k.shape),
                  _const_spec(w1v.shape), _const_spec(pev.shape), _const_spec(w2v.shape)],
        out_specs=[oblk, oblk],
        out_shape=[jax.ShapeDtypeStruct((b, g, n, NSA_HEAD_DIM), BF16)] * 2,
        compiler_params=_params("parallel"),
    )(rk, rv, w1k, pek, w2k, w1v, pev, w2v)


def _aug_keys(k, pos):
    lane = lax.broadcasted_iota(jnp.int32, k.shape, 1)
    hi = (_div_pow2(pos, POS_SPLIT) * POS_SPLIT).astype(F32)
    lo = (pos & (POS_SPLIT - 1)).astype(F32)
    cols = jnp.where(lane < SLOPE_PIECES, hi, jnp.where(lane < 2 * SLOPE_PIECES, lo, 0.0))
    return jnp.concatenate([k, cols.astype(BF16)], axis=1)


def _nsa_kernel(q_ref, gate_ref, slope_ref, ck_ref, cv_ref, ks_ref, vs_ref, kw_ref, vw_ref,
                ovl_ref, o_ref, kc_aug, ks_aug, kw_aug, vs_aug, vw_aug, s_buf, rank_ref, *, top_n):
    i = pl.program_id(2)
    tq = q_ref.shape[1]
    rows = NSA_REP * tq
    n_cmp = ck_ref.shape[2]
    n_slc = ovl_ref.shape[0]
    s_len = ks_ref.shape[2]
    q0 = i * tq

    @pl.when(i == 0)
    def _():
        cpos = lax.broadcasted_iota(jnp.int32, (n_cmp, 1), 0) * CMP_STRIDE + (CMP_LEN - 1)
        kc_aug[...] = _aug_keys(ck_ref[0, 0], cpos)

        def fill(c, carry):
            r0 = pl.multiple_of(c * SLC_CHUNK, SLC_CHUNK)
            pos = r0 + lax.broadcasted_iota(jnp.int32, (SLC_CHUNK, 1), 0)
            blk_lane = lax.broadcasted_iota(jnp.int32, (SLC_CHUNK, LANES), 1)
            onehot = jnp.where(blk_lane == _div_pow2(pos, SLC_LEN), 1.0, 0.0).astype(BF16)
            ks_aug[pl.ds(r0, SLC_CHUNK), :] = jnp.concatenate(
                [_aug_keys(ks_ref[0, 0, pl.ds(r0, SLC_CHUNK), :], pos), onehot], axis=1)
            kw_aug[pl.ds(r0, SLC_CHUNK), :] = _aug_keys(kw_ref[0, 0, pl.ds(r0, SLC_CHUNK), :], pos)
            return carry

        lax.fori_loop(0, s_len // SLC_CHUNK, fill, 0)
        ones_rows = jnp.where(lax.broadcasted_iota(jnp.int32, (V_PAD_ROWS, s_len), 0) == 0, 1.0, 0.0)
        for v_ref, v_aug in ((vs_ref, vs_aug), (vw_ref, vw_aug)):
            v_aug[0:NSA_HEAD_DIM, :] = v_ref[0, 0]
            v_aug[NSA_HEAD_DIM:, :] = ones_rows.astype(BF16)

    q_t = (q_ref[0].astype(F32) * (NSA_HEAD_DIM ** -0.5 * LOG2E)).T
    q_t = jnp.concatenate([q_t[r * NSA_HEAD_DIM:(r + 1) * NSA_HEAD_DIM, :] for r in range(NSA_REP)],
                          axis=1).astype(BF16)
    q_aug = jnp.concatenate([q_t, slope_ref[0].astype(BF16)], axis=0)
    t_lane = q0 + lax.broadcasted_iota(jnp.int32, (1, tq), 1)

    def all_heads(x):
        return jnp.concatenate([x] * NSA_REP, axis=1)

    def col_max(x):
        return jnp.max(x, axis=0, keepdims=True)

    def normalized(acc):
        return acc[:NSA_HEAD_DIM] / acc[NSA_HEAD_DIM:NSA_HEAD_DIM + 1]

    span = min(WINDOW + tq, s_len)
    w0 = pl.multiple_of(jnp.maximum(q0 + tq - span, 0), tq)
    dist_w = t_lane - (w0 + lax.broadcasted_iota(jnp.int32, (span, 1), 0))
    band = jnp.where(dist_w >= 0, jnp.where(dist_w < WINDOW, 0.0, NEG_INF), NEG_INF)
    sw = _dot(kw_aug[pl.ds(w0, span), :], q_aug) + all_heads(band)

    cend = lax.broadcasted_iota(jnp.int32, (n_cmp, 1), 0) * CMP_STRIDE + (CMP_LEN - 1)
    sc = _dot(kc_aug[...], q_aug) + all_heads(jnp.where(cend <= t_lane, 0.0, NEG_INF))

    tile_k0 = pl.multiple_of(q0, tq)
    key_in_tile = lax.broadcasted_iota(jnp.int32, (tq, 1), 0)
    qry_in_tile = lax.broadcasted_iota(jnp.int32, (1, tq), 1)
    sd = (_dot(ks_aug[pl.ds(tile_k0, tq), 0:LANES], q_aug)
          + all_heads(jnp.where(key_in_tile <= qry_in_tile, 0.0, NEG_INF)))

    e = jnp.exp2(sc - col_max(sc))
    p_cmp = (e * all_heads(jnp.where(t_lane >= CMP_LEN - 1, 1.0, 0.0))
             / jnp.sum(e, axis=0, keepdims=True))
    pw = jnp.exp2(sw - col_max(sw)).astype(BF16)
    m0 = col_max(sd)
    pd = jnp.exp2(sd - m0).astype(BF16)
    o_cmp = _dot(cv_ref[0, 0], p_cmp.astype(BF16))
    o_win = normalized(_dot(vw_aug[:, pl.ds(w0, span)], pw))
    acc0 = _dot(vs_aug[:, pl.ds(tile_k0, tq)], pd)

    p_sum = p_cmp[:, 0:tq]
    for r in range(1, NSA_REP):
        p_sum = p_sum + p_cmp[:, r * tq:(r + 1) * tq]
    imp = _dot(ovl_ref[...], p_sum, precision=HIGHEST)
    blk = lax.broadcasted_iota(jnp.int32, (n_slc, tq), 0)
    tl = q0 + lax.broadcasted_iota(jnp.int32, (n_slc, tq), 1)
    cur = _div_pow2(tl, SLC_LEN)
    forced = (blk == 0) | (blk == cur) | (blk == cur - 1)
    val = jnp.where(forced, FORCE_SCORE, jnp.where(blk * SLC_LEN <= tl, imp, -1.0))
    n_grp = n_slc // RANK_GROUP
    val_grp = [val[RANK_GROUP * g:RANK_GROUP * (g + 1)] for g in range(n_grp)]
    row_in_grp = lax.broadcasted_iota(jnp.int32, (RANK_GROUP, tq), 0)
    rank_ref[...] = jnp.zeros_like(rank_ref)
    for mg in range(n_grp):
        @pl.when(mg * RANK_GROUP * SLC_LEN < q0 + tq)
        def _():
            parts = [rank_ref[RANK_GROUP * g:RANK_GROUP * (g + 1), :] for g in range(n_grp)]
            for m in range(RANK_GROUP * mg, RANK_GROUP * (mg + 1)):
                vm = val[m:m + 1, :]
                for g in range(n_grp):
                    if g < mg:
                        beats = jnp.where(vm > val_grp[g], 1.0, 0.0)
                    elif g > mg:
                        beats = jnp.where(vm >= val_grp[g], 1.0, 0.0)
                    else:
                        beats = jnp.where(row_in_grp > m - RANK_GROUP * mg,
                                          jnp.where(vm >= val_grp[g], 1.0, 0.0),
                                          jnp.where(vm > val_grp[g], 1.0, 0.0))
                    parts[g] = parts[g] + beats
            for g in range(n_grp):
                rank_ref[RANK_GROUP * g:RANK_GROUP * (g + 1), :] = parts[g]
    rank = rank_ref[...]

    before_tile = blk * SLC_LEN < q0
    sel_mask = jnp.where(before_tile, jnp.where(rank < top_n, 0.0, NEG_INF), NEG_INF)
    sel_mask = jnp.concatenate([sel_mask, jnp.zeros((LANES - n_slc, tq), F32)], axis=0)
    q_sel = jnp.concatenate([q_aug, all_heads(sel_mask.astype(BF16))], axis=0)

    n_kb = SLC_CHUNK // SLC_KEY_BLOCK
    last_chunk = s_len // SLC_CHUNK - 1

    def chunk_logits(j, kb):
        kk = pl.multiple_of(j * SLC_CHUNK, SLC_CHUNK) + kb * SLC_KEY_BLOCK
        return _dot(ks_aug[pl.ds(kk, SLC_KEY_BLOCK), :], q_sel)

    m1 = m0
    for kb in range(n_kb):
        s_new = chunk_logits(0, kb)
        s_buf[kb * SLC_KEY_BLOCK:(kb + 1) * SLC_KEY_BLOCK, :] = s_new
        m1 = jnp.maximum(m1, col_max(s_new))

    def slc_step(j, carry):
        m_prev, m_cur, acc = carry
        acc = jnp.exp2(m_prev - m_cur) * acc
        k0 = pl.multiple_of(j * SLC_CHUNK, SLC_CHUNK)
        j_next = jnp.minimum(j + 1, last_chunk)
        m_next = m_cur
        for kb in range(n_kb):
            blk_rows = slice(kb * SLC_KEY_BLOCK, (kb + 1) * SLC_KEY_BLOCK)
            s_new = chunk_logits(j_next, kb)
            p = jnp.exp2(s_buf[blk_rows, :] - m_cur).astype(BF16)
            acc = acc + _dot(vs_aug[:, pl.ds(k0 + kb * SLC_KEY_BLOCK, SLC_KEY_BLOCK)], p)
            s_buf[blk_rows, :] = s_new
            m_next = jnp.maximum(m_next, col_max(s_new))
        return m_cur, m_next, acc

    n_before = (q0 + SLC_CHUNK - 1) // SLC_CHUNK
    _, _, acc = lax.fori_loop(0, n_before, slc_step, (m0, m1, acc0))
    o_slc = normalized(acc)

    gates = _sigmoid(gate_ref[0]).T

    def gate_rows(br):
        return jnp.concatenate([gates[3 * r + br:3 * r + br + 1, :] for r in range(NSA_REP)], axis=1)

    o = gate_rows(0) * o_cmp + gate_rows(1) * o_slc + gate_rows(2) * o_win
    o = jnp.concatenate([o[:, r * tq:(r + 1) * tq] for r in range(NSA_REP)], axis=0)
    o_ref[0] = o.T.astype(o_ref.dtype)


def nsa_attention(z, zg, ck, cv, ks, vs, kw, vw, tq=128):
    b, s, _ = z.shape
    n_cmp = ck.shape[2]
    n_slc = s // SLC_LEN
    top_n = min(SLC_TOP, n_slc)
    assert tq == 2 * SLC_LEN and top_n >= 3 and n_slc <= LANES and s % SLC_CHUNK == 0
    c_lo = jnp.arange(n_cmp) * CMP_STRIDE
    s_lo = jnp.arange(n_slc) * SLC_LEN
    ovl = (jnp.clip(jnp.minimum(c_lo[None, :] + CMP_LEN, s_lo[:, None] + SLC_LEN)
                    - jnp.maximum(c_lo[None, :], s_lo[:, None]), 0, None).astype(F32) / CMP_LEN)
    ovl = ovl * (jnp.arange(n_cmp) < n_cmp - 1)[None, :]
    hd = jnp.arange(1, NSA_HEADS + 1, dtype=F32)
    rest = jnp.exp2(-8.0 * hd / NSA_HEADS) * LOG2E
    pieces = []
    for _ in range(SLOPE_PIECES):
        piece = rest.astype(BF16).astype(F32)
        pieces.append(piece)
        rest = rest - piece
    pieces = jnp.stack(pieces * 2, axis=0).reshape(2 * SLOPE_PIECES, NSA_GROUPS, NSA_REP)
    slope_tab = jnp.zeros((NSA_GROUPS, NSA_HEAD_DIM, NSA_REP, tq), F32)
    slope_tab = slope_tab.at[:, :2 * SLOPE_PIECES].set(
        jnp.transpose(pieces, (1, 0, 2))[..., None]).reshape(NSA_GROUPS, NSA_HEAD_DIM, NSA_REP * tq)
    k_spec = pl.BlockSpec((1, 1, s, NSA_HEAD_DIM), lambda bi, g, i: (bi, g, 0, 0))
    v_spec = pl.BlockSpec((1, 1, NSA_HEAD_DIM, s), lambda bi, g, i: (bi, g, 0, 0))
    ck_spec = pl.BlockSpec((1, 1, n_cmp, NSA_HEAD_DIM), lambda bi, g, i: (bi, g, 0, 0))
    cv_spec = pl.BlockSpec((1, 1, NSA_HEAD_DIM, n_cmp), lambda bi, g, i: (bi, g, 0, 0))
    gw = NSA_REP * NSA_HEAD_DIM
    return pl.pallas_call(
        functools.partial(_nsa_kernel, top_n=top_n),
        grid=(b, NSA_GROUPS, s // tq),
        in_specs=[pl.BlockSpec((1, tq, gw), lambda bi, g, i: (bi, i, g)),
                  pl.BlockSpec((1, tq, LANES), lambda bi, g, i: (bi, i, g)),
                  pl.BlockSpec((1, NSA_HEAD_DIM, NSA_REP * tq), lambda bi, g, i: (g, 0, 0)),
                  ck_spec, cv_spec, k_spec, v_spec, k_spec, v_spec,
                  _const_spec(ovl.shape)],
        out_specs=pl.BlockSpec((1, tq, gw), lambda bi, g, i: (bi, i, g)),
        out_shape=jax.ShapeDtypeStruct((b, s, NSA_GROUPS * gw), BF16),
        scratch_shapes=[pltpu.VMEM((n_cmp, 2 * NSA_HEAD_DIM), BF16),
                        pltpu.VMEM((s, 2 * NSA_HEAD_DIM + LANES), BF16),
                        pltpu.VMEM((s, 2 * NSA_HEAD_DIM), BF16),
                        pltpu.VMEM((NSA_HEAD_DIM + V_PAD_ROWS, s), BF16),
                        pltpu.VMEM((NSA_HEAD_DIM + V_PAD_ROWS, s), BF16),
                        pltpu.VMEM((SLC_CHUNK, NSA_REP * tq), F32),
                        pltpu.VMEM((n_slc, tq), F32)],
        compiler_params=_params("parallel", "parallel", "arbitrary"),
    )(z, zg, slope_tab, ck, cv, ks, vs, kw, vw, ovl)


def _pad_cols(w, n):
    return jnp.pad(w, ((0, 0), (0, n - w.shape[1])))


def pool_delta_layer(x, ln, w_in, pool_w, pool_scale, conv_w, a_log, dt_bias, o_norm, w_out):
    b, s, d = x.shape
    main = POOL_WIDTH + 4 * DN_WIDTH
    w_all = jnp.concatenate([w_in[:, :main], _pad_cols(w_in[:, main:], LANES)], axis=1).astype(BF16)
    (z,) = norm_matmul(x.reshape(b * s, d), ln, [w_all], [F32])
    z = z.reshape(b, s, main + LANES)
    y_pool = pool_mixer(z, pool_w.astype(BF16), pool_scale)
    y_dn = gated_deltanet(z, main // LANES, conv_w, a_log, dt_bias, o_norm)
    w_out = w_out.astype(BF16)
    return matmul_res([y_pool.reshape(b * s, POOL_WIDTH), y_dn.reshape(b * s, DN_WIDTH)],
                      [w_out[:POOL_WIDTH], w_out[POOL_WIDTH:]], x.reshape(b * s, d)).reshape(b, s, d)


def _cmp_weights(pe, w1, w2):
    half = (CMP_LEN // 2) * NSA_HEAD_DIM
    w1_pair = jnp.concatenate([w1[:half], w1[half:]], axis=1).astype(BF16)
    pe_rows = jnp.zeros((8, half), F32).at[0:2].set(pe.reshape(2, half)).astype(BF16)
    return w1_pair, pe_rows, w2.astype(BF16)


def nsa_layer(x, ln, w_in, pe_k, w1_k, w2_k, pe_v, w1_v, w2_v, w_out):
    b, s, d = x.shape
    g_, r_, hd = NSA_GROUPS, NSA_REP, NSA_HEAD_DIM
    kvw = g_ * hd
    main = d + 6 * kvw
    wg = w_in[:, main:].reshape(d, g_, r_ * 3)
    wg = jnp.pad(wg, ((0, 0), (0, 0), (0, LANES - r_ * 3))).reshape(d, g_ * LANES)
    z, zg = norm_matmul(x.reshape(b * s, d), ln, [w_in[:, :main].astype(BF16), wg.astype(BF16)],
                        [BF16, F32])
    z = z.reshape(b, s, main)
    zg = zg.reshape(b, s, g_ * LANES)

    def group_major(j):
        return jnp.transpose(z[..., d + j * kvw:d + (j + 1) * kvw].reshape(b, s, g_, hd), (0, 2, 1, 3))

    n_str = s // CMP_STRIDE
    rk = group_major(0).reshape(b, g_, n_str, CMP_STRIDE * hd)
    rv = group_major(1).reshape(b, g_, n_str, CMP_STRIDE * hd)
    ck, cv = compress_kv(rk, rv, *_cmp_weights(pe_k, w1_k, w2_k), *_cmp_weights(pe_v, w1_v, w2_v))

    def channel_major(j):
        return jnp.transpose(z[..., d + j * kvw:d + (j + 1) * kvw].reshape(b, s, g_, hd), (0, 2, 3, 1))

    o = nsa_attention(z, zg, ck, jnp.swapaxes(cv, 2, 3), group_major(2), channel_major(3),
                      group_major(4), channel_major(5))
    return matmul_res([o.reshape(b * s, d)], [w_out.astype(BF16)], x.reshape(b * s, d)).reshape(b, s, d)


def kernel(x, mem, a_ln, a_w_in, a_pool_w, a_pool_scale, a_conv_w, a_a_log, a_dt_bias, a_o_norm, a_w_out, c_ln, c_w_in, c_pe_k, c_w1_k, c_w2_k, c_pe_v, c_w1_v, c_w2_v, c_w_out, xa_ln, xa_mem_ln, xa_wq, xa_wk, xa_wv, xa_wo, ff_ln, ff_w1, ff_w2, final_ln):
    b, s, d = x.shape
    depth = xa_ln.shape[0]
    mem2d = mem.reshape(b * mem.shape[1], d)
    for l in range(depth):
        i = l // 2
        if l % 2 == 0:
            x = pool_delta_layer(x, a_ln[i], a_w_in[i], a_pool_w[i], a_pool_scale[i], a_conv_w[i],
                                 a_a_log[i], a_dt_bias[i], a_o_norm[i], a_w_out[i])
        else:
            x = nsa_layer(x, c_ln[i], c_w_in[i], c_pe_k[i], c_w1_k[i], c_w2_k[i],
                          c_pe_v[i], c_w1_v[i], c_w2_v[i], c_w_out[i])
        mk, mv = norm_matmul(mem2d, xa_mem_ln[l], [xa_wk[l].astype(BF16), xa_wv[l].astype(BF16)],
                             [BF16, BF16])
        x = xattn_sublayer(x, xa_ln[l], xa_wq[l].astype(BF16), mk, mv, xa_wo[l].astype(BF16))
        x = mlp_sublayer(x.reshape(b * s, d), ff_ln[l], ff_w1[l].astype(BF16), ff_w2[l].astype(BF16),
                         final_ln, final_norm=(l == depth - 1)).reshape(b, s, d)
    return x
```

```python
import functools
import math

import jax
import jax.numpy as jnp
from jax import lax
from jax.experimental import pallas as pl
from jax.experimental.pallas import tpu as pltpu

F32 = jnp.float32
BF16 = jnp.bfloat16
HIGHEST = lax.Precision.HIGHEST

LANES = 128
VMEM_LIMIT = 56 * 1024 * 1024

EPS = 1e-6
NEG_INF = -1e30
FORCE_SCORE = 1e4

POOL_WINDOWS = (2, 4, 8, 16)
POOL_GROUP_DIM = 128
POOL_WIDTH = 512
DN_HEADS = 4
DN_HEAD_DIM = 128
DN_WIDTH = 512
DN_CONV = 4
DN_CHUNK = 64

NSA_HEAD_DIM = 64
NSA_GROUPS = 4
NSA_REP = 4
NSA_HEADS = 16
CMP_LEN = 32
CMP_STRIDE = 16
CMP_HIDDEN = 128
SLC_LEN = 64
SLC_TOP = 16
WINDOW = 512
SLC_CHUNK = 512
SLC_KEY_BLOCK = 256
RANK_GROUP = 8
V_PAD_ROWS = 16
LOG2E = 1.4426950408889634
SLOPE_PIECES = 3
POS_SPLIT = 64

XA_HEADS = 4
XA_HEAD_DIM = 256


def _params(*sem):
    return pltpu.CompilerParams(dimension_semantics=sem, vmem_limit_bytes=VMEM_LIMIT)


def _const_spec(shape):
    nd = len(shape)
    return pl.BlockSpec(shape, lambda *_: (0,) * nd)


def _rms(x, g):
    return x * lax.rsqrt(jnp.mean(x * x, axis=-1, keepdims=True) + EPS) * g


def _sigmoid(x):
    return 1.0 / (1.0 + jnp.exp(-x))


def _div_pow2(x, n):
    return lax.shift_right_logical(x, jnp.int32(int(math.log2(n))))


def _dot(a, b, precision=None):
    return jnp.dot(a, b, preferred_element_type=F32, precision=precision)


def _dot_nt(a, b, precision=None):
    return lax.dot_general(a, b, (((1,), (1,)), ((), ())), preferred_element_type=F32,
                           precision=precision)


def _dot_tn(a, b):
    return lax.dot_general(a, b, (((0,), (0,)), ((), ())), preferred_element_type=F32)


def _norm_matmul_kernel(x_ref, g_ref, *refs, n_plain):
    n = len(refs) // 2
    hb = _rms(x_ref[...], g_ref[...]).astype(BF16)
    for j, (w_ref, o_ref) in enumerate(zip(refs[:n], refs[n:])):
        if j < n_plain:
            o_ref[...] = _dot(hb, w_ref[...]).astype(o_ref.dtype)
        else:
            o_ref[...] = _dot_nt(w_ref[...], hb).astype(o_ref.dtype)


def norm_matmul(x2d, g, ws, out_dtypes, wts=(), tm=512):
    t, d = x2d.shape
    tm = min(tm, t)
    return pl.pallas_call(
        functools.partial(_norm_matmul_kernel, n_plain=len(ws)),
        grid=(t // tm,),
        in_specs=[pl.BlockSpec((tm, d), lambda i: (i, 0)), _const_spec((1, d))]
        + [_const_spec(w.shape) for w in (*ws, *wts)],
        out_specs=[pl.BlockSpec((tm, w.shape[1]), lambda i: (i, 0)) for w in ws]
        + [pl.BlockSpec((w.shape[0], tm), lambda i: (0, i)) for w in wts],
        out_shape=[jax.ShapeDtypeStruct((t, w.shape[1]), dt) for w, dt in zip(ws, out_dtypes)]
        + [jax.ShapeDtypeStruct((w.shape[0], t), BF16) for w in wts],
        compiler_params=_params("parallel"),
    )(x2d, g.reshape(1, d).astype(F32), *ws, *wts)


def _matmul_res_kernel(*refs):
    n = (len(refs) - 2) // 2
    res_ref, o_ref = refs[2 * n], refs[2 * n + 1]
    acc = res_ref[...]
    for a_ref, w_ref in zip(refs[:n], refs[n:2 * n]):
        acc = acc + _dot(a_ref[...], w_ref[...])
    o_ref[...] = acc


def matmul_res(a_list, w_list, res2d, tm=512):
    t, d = res2d.shape
    tm = min(tm, t)
    return pl.pallas_call(
        _matmul_res_kernel,
        grid=(t // tm,),
        in_specs=[pl.BlockSpec((tm, a.shape[1]), lambda i: (i, 0)) for a in a_list]
        + [_const_spec(w.shape) for w in w_list]
        + [pl.BlockSpec((tm, d), lambda i: (i, 0))],
        out_specs=pl.BlockSpec((tm, d), lambda i: (i, 0)),
        out_shape=jax.ShapeDtypeStruct((t, d), F32),
        compiler_params=_params("parallel"),
    )(*a_list, *w_list, res2d)


def _mlp_kernel(x_ref, g_ref, w1_ref, w2_ref, gf_ref, o_ref, *, ff_chunk, final_norm):
    x = x_ref[...]
    hb = _rms(x, g_ref[...]).astype(BF16)
    acc = x
    for c in range(w1_ref.shape[1] // ff_chunk):
        a = jnp.maximum(_dot(hb, w1_ref[:, c * ff_chunk:(c + 1) * ff_chunk]), 0.0)
        acc = acc + _dot((a * a).astype(BF16), w2_ref[c * ff_chunk:(c + 1) * ff_chunk, :])
    if final_norm:
        acc = _rms(acc, gf_ref[...])
    o_ref[...] = acc


def mlp_sublayer(x2d, g, w1, w2, gf, final_norm, tm=512, ff_chunk=1024):
    t, d = x2d.shape
    tm = min(tm, t)
    return pl.pallas_call(
        functools.partial(_mlp_kernel, ff_chunk=ff_chunk, final_norm=final_norm),
        grid=(t // tm,),
        in_specs=[pl.BlockSpec((tm, d), lambda i: (i, 0)), _const_spec((1, d)),
                  _const_spec(w1.shape), _const_spec(w2.shape), _const_spec((1, d))],
        out_specs=pl.BlockSpec((tm, d), lambda i: (i, 0)),
        out_shape=jax.ShapeDtypeStruct((t, d), F32),
        compiler_params=_params("parallel"),
    )(x2d, g.reshape(1, d).astype(F32), w1, w2, gf.reshape(1, d).astype(F32))


def _xattn_kernel(x_ref, g_ref, wq_ref, k_ref, v_ref, wo_ref, o_ref):
    x = x_ref[0]
    hb = _rms(x, g_ref[...]).astype(BF16)
    q = (_dot(hb, wq_ref[...]) * (XA_HEAD_DIM ** -0.5)).astype(BF16)
    heads = []
    for h in range(XA_HEADS):
        sl = slice(h * XA_HEAD_DIM, (h + 1) * XA_HEAD_DIM)
        s = _dot_nt(q[:, sl], k_ref[:, sl])
        p = jnp.exp(s - jnp.max(s, axis=-1, keepdims=True))
        l = jnp.sum(p, axis=-1, keepdims=True)
        heads.append((_dot(p.astype(BF16), v_ref[:, sl]) / l).astype(BF16))
    o_ref[0] = x + _dot(jnp.concatenate(heads, axis=1), wo_ref[...])


def xattn_sublayer(x, g, wq, k2d, v2d, wo, tq=512):
    b, s, d = x.shape
    tq = min(tq, s)
    m = k2d.shape[0] // b
    return pl.pallas_call(
        _xattn_kernel,
        grid=(b, s // tq),
        in_specs=[pl.BlockSpec((1, tq, d), lambda bi, i: (bi, i, 0)), _const_spec((1, d)),
                  _const_spec(wq.shape),
                  pl.BlockSpec((m, d), lambda bi, i: (bi, 0)),
                  pl.BlockSpec((m, d), lambda bi, i: (bi, 0)),
                  _const_spec(wo.shape)],
        out_specs=pl.BlockSpec((1, tq, d), lambda bi, i: (bi, i, 0)),
        out_shape=jax.ShapeDtypeStruct((b, s, d), F32),
        compiler_params=_params("parallel", "parallel"),
    )(x, g.reshape(1, d).astype(F32), wq, k2d, v2d, wo)


POOL_HALO = 16


def _pool_kernel(u_ref, halo_ref, w_ref, scale_ref, o_ref):
    i = pl.program_id(1)
    ts = u_ref.shape[1]
    u = u_ref[0]
    halo = jnp.where(i == 0, 0.0, halo_ref[0])
    ext = jnp.concatenate([halo, u], axis=0)
    sums = [None] * len(POOL_WINDOWS)
    cur = ext
    for gi, win in enumerate(POOL_WINDOWS):
        cur = cur[:, (POOL_GROUP_DIM if gi else 0):]
        cur = cur + pltpu.roll(cur, win // 2, axis=0)
        sums[gi] = cur[POOL_HALO:, :POOL_GROUP_DIM]
    pos1 = (i * ts + 1 + lax.broadcasted_iota(jnp.int32, (ts, 1), 0)).astype(F32)
    outs = []
    for gi, win in enumerate(POOL_WINDOWS):
        ug = u[:, gi * POOL_GROUP_DIM:(gi + 1) * POOL_GROUP_DIM]
        y = sums[gi] / jnp.minimum(pos1, float(win)) - ug
        outs.append(_dot(y.astype(BF16), w_ref[gi]))
    o_ref[0] = (jnp.concatenate(outs, axis=1) * scale_ref[...]).astype(o_ref.dtype)


def pool_mixer(z, pool_w, pool_scale, ts=512):
    b, s, _ = z.shape
    ts = min(ts, s)
    hb = ts // POOL_HALO
    return pl.pallas_call(
        _pool_kernel,
        grid=(b, s // ts),
        in_specs=[pl.BlockSpec((1, ts, POOL_WIDTH), lambda bi, i: (bi, i, 0)),
                  pl.BlockSpec((1, POOL_HALO, POOL_WIDTH),
                               lambda bi, i: (bi, jnp.maximum(i * hb - 1, 0), 0)),
                  _const_spec(pool_w.shape), _const_spec((1, POOL_WIDTH))],
        out_specs=pl.BlockSpec((1, ts, POOL_WIDTH), lambda bi, i: (bi, i, 0)),
        out_shape=jax.ShapeDtypeStruct((b, s, POOL_WIDTH), BF16),
        compiler_params=_params("parallel", "parallel"),
    )(z, z, pool_w, pool_scale.reshape(1, POOL_WIDTH).astype(F32))


DN_ROWS = DN_HEADS * DN_CHUNK
DN_CHUNKS_PER_STEP = 4
BETA_LANE = 0
ALPHA_LANE = DN_HEADS


def _stack_heads(x):
    return jnp.concatenate([x[:, h * DN_HEAD_DIM:(h + 1) * DN_HEAD_DIM] for h in range(DN_HEADS)],
                           axis=0)


def _stack_cols(x, lane0):
    return jnp.concatenate([x[:, lane0 + h:lane0 + h + 1] for h in range(DN_HEADS)], axis=0)


def _pick_head_block(wide, row_head):
    out = jnp.zeros((DN_ROWS, DN_HEAD_DIM), F32)
    for h in range(DN_HEADS):
        out = jnp.where(row_head == h, wide[:, h * DN_HEAD_DIM:(h + 1) * DN_HEAD_DIM], out)
    return out


def _deltanet_kernel(q_ref, k_ref, v_ref, gate_ref, ba_ref, cw_ref, alog_ref, dtb_ref, onorm_ref,
                     o_ref, state_ref, tail_ref):
    c = pl.program_id(1)

    @pl.when(c == 0)
    def _():
        state_ref[...] = jnp.zeros_like(state_ref)
        tail_ref[...] = jnp.zeros_like(tail_ref)

    blk_len = q_ref.shape[1]
    x3 = jnp.concatenate([q_ref[0], k_ref[0], v_ref[0]], axis=1)
    ext = jnp.concatenate([tail_ref[...], x3], axis=0)
    tail_ref[...] = x3[blk_len - 8:, :]
    cw = cw_ref[...]
    y = cw[DN_CONV - 1:DN_CONV, :] * ext
    for j in range(1, DN_CONV):
        y = y + cw[DN_CONV - 1 - j:DN_CONV - j, :] * pltpu.roll(ext, j, axis=0)
    y = y[8:, :]
    y = y * _sigmoid(y)

    def l2n(a):
        return a * lax.rsqrt(jnp.sum(a * a, axis=-1, keepdims=True) + EPS)

    ba = ba_ref[0]
    beta_all = _sigmoid(ba)
    sp_in = ba + dtb_ref[...]
    softplus = jnp.maximum(sp_in, 0.0) + jnp.log(1.0 + jnp.exp(-jnp.abs(sp_in)))
    g_all = -jnp.exp(alog_ref[...]) * softplus
    ri = lax.broadcasted_iota(jnp.int32, (blk_len, blk_len), 0)
    ci = lax.broadcasted_iota(jnp.int32, (blk_len, blk_len), 1)
    in_chunk_tril = (ri >= ci) & (_div_pow2(ri, DN_CHUNK) == _div_pow2(ci, DN_CHUNK))
    gc_blk = _dot(jnp.where(in_chunk_tril, 1.0, 0.0), g_all, precision=HIGHEST)

    rr = lax.broadcasted_iota(jnp.int32, (DN_ROWS, DN_ROWS), 0)
    cc = lax.broadcasted_iota(jnp.int32, (DN_ROWS, DN_ROWS), 1)
    same_head = _div_pow2(rr, DN_CHUNK) == _div_pow2(cc, DN_CHUNK)
    causal = same_head & (rr >= cc)
    strict = same_head & (rr > cc)
    lane_head = _div_pow2(lax.broadcasted_iota(jnp.int32, (DN_CHUNK, DN_ROWS), 1), DN_CHUNK)

    def block_diag(w):
        return jnp.concatenate([jnp.where(lane_head == h, w, 0.0) for h in range(DN_HEADS)], axis=0)

    def hi_lo(x):
        hi = x.astype(BF16).astype(F32)
        return hi, x - hi

    def times_p(x, p_wide):
        p_hi, p_lo = hi_lo(p_wide)
        d_hi, d_lo = block_diag(p_hi).astype(BF16), block_diag(p_lo).astype(BF16)
        x_hi, x_lo = hi_lo(x)
        n = x.shape[0]
        top = _dot(jnp.concatenate([x_hi, x_lo], axis=0).astype(BF16), d_hi)
        return top[:n] + top[n:] + _dot(x_hi.astype(BF16), d_lo)

    wr = lax.broadcasted_iota(jnp.int32, (DN_CHUNK, DN_ROWS), 0)
    wc = lax.broadcasted_iota(jnp.int32, (DN_CHUNK, DN_ROWS), 1)
    eye_wide = jnp.where(wr == (wc & (DN_CHUNK - 1)), 1.0, 0.0)
    n_sq = int(math.log2(DN_CHUNK)) - 1

    def chunk_prep(ci):
        rows = slice(ci * DN_CHUNK, (ci + 1) * DN_CHUNK)
        yc = y[rows, :]
        q_st = l2n(_stack_heads(yc[:, :DN_WIDTH])) * (DN_HEAD_DIM ** -0.5)
        k_st = l2n(_stack_heads(yc[:, DN_WIDTH:2 * DN_WIDTH]))
        v_st = _stack_heads(yc[:, 2 * DN_WIDTH:])
        gc_all = gc_blk[rows, :]
        beta_st = _stack_cols(beta_all[rows, :], BETA_LANE)
        gc_st = _stack_cols(gc_all, ALPHA_LANE)
        g_last = [gc_all[DN_CHUNK - 1:DN_CHUNK, ALPHA_LANE + h:ALPHA_LANE + h + 1]
                  for h in range(DN_HEADS)]
        glast_st = jnp.concatenate([jnp.broadcast_to(g, (DN_CHUNK, 1)) for g in g_last], axis=0)
        gcb = jnp.broadcast_to(gc_st, (DN_ROWS, DN_ROWS))
        decay = jnp.where(causal, jnp.exp(jnp.where(causal, gcb - gcb.T, 0.0)), 0.0)
        kb_st = k_st * beta_st
        k_bf = k_st.astype(BF16)
        a_low = jnp.where(strict, _dot_nt(kb_st.astype(BF16), k_bf) * decay, 0.0)
        p = -(a_low[0:DN_CHUNK] + a_low[DN_CHUNK:2 * DN_CHUNK]
              + a_low[2 * DN_CHUNK:3 * DN_CHUNK] + a_low[3 * DN_CHUNK:])
        t_wide = eye_wide + p
        yield None
        for j in range(n_sq):
            if j == 0:
                p = times_p(p, p)
            else:
                both = times_p(jnp.concatenate([p, t_wide], axis=0), p)
                p, t_wide = both[:DN_CHUNK], t_wide + both[DN_CHUNK:]
            yield None
        t_wide = t_wide + times_p(t_wide, p)
        t_bf = block_diag(t_wide).astype(BF16)
        egc = jnp.exp(gc_st)
        wu = _dot(t_bf, jnp.concatenate([(kb_st * egc).astype(BF16),
                                         (v_st * beta_st).astype(BF16)], axis=1))
        attn = jnp.where(causal, _dot_nt(q_st.astype(BF16), k_bf) * decay, 0.0)
        yield dict(
            w=wu[:, :DN_HEAD_DIM].astype(BF16), u=wu[:, DN_HEAD_DIM:], attn=attn.astype(BF16),
            q_dec=(q_st * egc).astype(BF16), k_dec=(k_st * jnp.exp(glast_st - gc_st)).astype(BF16),
            state_scale=jnp.concatenate(
                [jnp.broadcast_to(jnp.exp(g), (1, DN_HEAD_DIM)) for g in g_last], axis=1))

    preps = [chunk_prep(ci) for ci in range(blk_len // DN_CHUNK)]
    for _ in range(n_sq + 1):
        for prep in preps:
            next(prep)
    preps = [next(prep) for prep in preps]

    state = state_ref[...]
    row_head = _div_pow2(lax.broadcasted_iota(jnp.int32, (DN_ROWS, 1), 0), DN_CHUNK)
    gate = gate_ref[0]
    out_rows = []
    for ci, pr in enumerate(preps):
        s_bf = state.astype(BF16)
        v_new = pr["u"] - _pick_head_block(_dot(pr["w"], s_bf), row_head)
        v_new_bf = v_new.astype(BF16)
        o_st = _pick_head_block(_dot(pr["q_dec"], s_bf), row_head) + _dot(pr["attn"], v_new_bf)
        zero = jnp.zeros_like(v_new_bf)
        v_wide = jnp.concatenate([jnp.where(row_head == h, v_new_bf, zero) for h in range(DN_HEADS)],
                                 axis=1)
        state = state * pr["state_scale"] + _dot_tn(pr["k_dec"], v_wide)
        outs = []
        for h in range(DN_HEADS):
            o_h = _rms(o_st[h * DN_CHUNK:(h + 1) * DN_CHUNK, :], onorm_ref[...])
            g_h = gate[ci * DN_CHUNK:(ci + 1) * DN_CHUNK, h * DN_HEAD_DIM:(h + 1) * DN_HEAD_DIM]
            outs.append(o_h * (g_h * _sigmoid(g_h)))
        out_rows.append(jnp.concatenate(outs, axis=1))
    state_ref[...] = state
    o_ref[0] = jnp.concatenate(out_rows, axis=0).astype(o_ref.dtype)


def gated_deltanet(z, ba_block, conv_w, a_log, dt_bias, o_norm):
    b, s, _ = z.shape
    lane_row = lambda vals, lane0: jnp.zeros((1, LANES), F32).at[0, lane0:lane0 + DN_HEADS].set(
        vals.astype(F32))
    blk_len = min(DN_CHUNKS_PER_STEP * DN_CHUNK, s)
    col = lambda j: pl.BlockSpec((1, blk_len, DN_WIDTH), lambda bi, ci: (bi, ci, j))
    return pl.pallas_call(
        _deltanet_kernel,
        grid=(b, s // blk_len),
        in_specs=[col(1), col(2), col(3), col(4),
                  pl.BlockSpec((1, blk_len, LANES), lambda bi, ci: (bi, ci, ba_block)),
                  _const_spec(conv_w.shape), _const_spec((1, LANES)), _const_spec((1, LANES)),
                  _const_spec((1, DN_HEAD_DIM))],
        out_specs=pl.BlockSpec((1, blk_len, DN_WIDTH), lambda bi, ci: (bi, ci, 0)),
        out_shape=jax.ShapeDtypeStruct((b, s, DN_WIDTH), BF16),
        scratch_shapes=[pltpu.VMEM((DN_HEAD_DIM, DN_WIDTH), F32),
                        pltpu.VMEM((8, 3 * DN_WIDTH), F32)],
        compiler_params=_params("parallel", "arbitrary"),
    )(z, z, z, z, z, conv_w.astype(F32), lane_row(a_log, ALPHA_LANE), lane_row(dt_bias, ALPHA_LANE),
      o_norm.reshape(1, DN_HEAD_DIM).astype(F32))


def _compress_kernel(rk_ref, rv_ref, w1k_ref, pek_ref, w2k_ref, w1v_ref, pev_ref, w2v_ref,
                     ck_ref, cv_ref):
    n = rk_ref.shape[2]
    row = lax.broadcasted_iota(jnp.int32, (n, 1), 0)

    def one(r_ref, w1_ref, pe_ref, w2_ref, o_ref):
        w1 = w1_ref[...]
        pb = _dot(pe_ref[...], w1)
        bias = pb[0:1, :CMP_HIDDEN] + pb[1:2, CMP_HIDDEN:]
        for g in range(NSA_GROUPS):
            y = _dot(r_ref[0, g], w1)
            h = y[:, :CMP_HIDDEN] + pltpu.roll(y[:, CMP_HIDDEN:], n - 1, axis=0) + bias
            a = (h * _sigmoid(h)).astype(BF16)
            o_ref[0, g] = jnp.where(row < n - 1, _dot(a, w2_ref[...]), 0.0).astype(o_ref.dtype)

    one(rk_ref, w1k_ref, pek_ref, w2k_ref, ck_ref)
    one(rv_ref, w1v_ref, pev_ref, w2v_ref, cv_ref)


def compress_kv(rk, rv, w1k, pek, w2k, w1v, pev, w2v):
    b, g, n, w = rk.shape
    blk = pl.BlockSpec((1, g, n, w), lambda bi: (bi, 0, 0, 0))
    oblk = pl.BlockSpec((1, g, n, NSA_HEAD_DIM), lambda bi: (bi, 0, 0, 0))
    return pl.pallas_call(
        _compress_kernel,
        grid=(b,),
        in_specs=[blk, blk, _const_spec(w1k.shape), _const_spec(pek.shape), _const_spec(w2k.shape),
                  _const_spec(w1v.shape), _const_spec(pev.shape), _const_spec(w2v.shape)],
        out_specs=[oblk, oblk],
        out_shape=[jax.ShapeDtypeStruct((b, g, n, NSA_HEAD_DIM), BF16)] * 2,
        compiler_params=_params("parallel"),
    )(rk, rv, w1k, pek, w2k, w1v, pev, w2v)


def _aug_keys(k, pos):
    n = k.shape[0]
    lane = lax.broadcasted_iota(jnp.int32, (n, NSA_HEAD_DIM), 1)
    hi = (_div_pow2(pos, POS_SPLIT) * POS_SPLIT).astype(F32)
    lo = (pos & (POS_SPLIT - 1)).astype(F32)
    cols = jnp.where(lane < SLOPE_PIECES, hi, jnp.where(lane < 2 * SLOPE_PIECES, lo, 0.0))
    return jnp.concatenate([k[:, :NSA_HEAD_DIM], cols.astype(BF16)], axis=1)


def _nsa_kernel(q_ref, gate_ref, slope_ref, ck_ref, cv_ref, ks_ref, vs_ref, kw_ref, vw_ref,
                ovl_ref, o_ref, kc_aug, ks_aug, kw_aug, vs_aug, vw_aug, s_buf, rank_ref, chunk_list,
                *, top_n):
    i = pl.program_id(2)
    tq = q_ref.shape[1]
    rows = NSA_REP * tq
    n_cmp = ck_ref.shape[2]
    n_slc = ovl_ref.shape[0]
    s_len = ks_ref.shape[1]
    q0 = i * tq

    @pl.when(i == 0)
    def _():
        cpos = lax.broadcasted_iota(jnp.int32, (n_cmp, 1), 0) * CMP_STRIDE + (CMP_LEN - 1)
        kc_aug[...] = _aug_keys(ck_ref[0, 0], cpos)

        def fill(c, carry):
            r0 = pl.multiple_of(c * SLC_CHUNK, SLC_CHUNK)
            pos = r0 + lax.broadcasted_iota(jnp.int32, (SLC_CHUNK, 1), 0)
            blk_lane = lax.broadcasted_iota(jnp.int32, (SLC_CHUNK, LANES), 1)
            onehot = jnp.where(blk_lane == _div_pow2(pos, SLC_LEN), 1.0, 0.0).astype(BF16)
            ks_aug[pl.ds(r0, SLC_CHUNK), :] = jnp.concatenate(
                [_aug_keys(ks_ref[0, pl.ds(r0, SLC_CHUNK), :], pos), onehot], axis=1)
            kw_aug[pl.ds(r0, SLC_CHUNK), :] = _aug_keys(kw_ref[0, pl.ds(r0, SLC_CHUNK), :], pos)
            return carry

        lax.fori_loop(0, s_len // SLC_CHUNK, fill, 0)
        ones_rows = jnp.where(lax.broadcasted_iota(jnp.int32, (V_PAD_ROWS, s_len), 0) == 0, 1.0, 0.0)
        for v_ref, v_aug in ((vs_ref, vs_aug), (vw_ref, vw_aug)):
            v_aug[0:NSA_HEAD_DIM, :] = v_ref[...]
            v_aug[NSA_HEAD_DIM:, :] = ones_rows.astype(BF16)

    q_t = (q_ref[0].astype(F32) * (NSA_HEAD_DIM ** -0.5 * LOG2E)).T
    q_t = jnp.concatenate([q_t[r * NSA_HEAD_DIM:(r + 1) * NSA_HEAD_DIM, :] for r in range(NSA_REP)],
                          axis=1).astype(BF16)
    q_aug = jnp.concatenate([q_t, slope_ref[0].astype(BF16)], axis=0)
    t_lane = q0 + lax.broadcasted_iota(jnp.int32, (1, tq), 1)

    def all_heads(x):
        return jnp.concatenate([x] * NSA_REP, axis=1)

    def col_max(x):
        return jnp.max(x, axis=0, keepdims=True)

    def normalized(acc):
        return acc[:NSA_HEAD_DIM] / acc[NSA_HEAD_DIM:NSA_HEAD_DIM + 1]

    span = min(WINDOW + tq, s_len)
    w0 = pl.multiple_of(jnp.maximum(q0 + tq - span, 0), tq)
    dist_w = t_lane - (w0 + lax.broadcasted_iota(jnp.int32, (span, 1), 0))
    band = jnp.where(dist_w >= 0, jnp.where(dist_w < WINDOW, 0.0, NEG_INF), NEG_INF)
    sw = _dot(kw_aug[pl.ds(w0, span), :], q_aug) + all_heads(band)

    cend = lax.broadcasted_iota(jnp.int32, (n_cmp, 1), 0) * CMP_STRIDE + (CMP_LEN - 1)
    sc = _dot(kc_aug[...], q_aug) + all_heads(jnp.where(cend <= t_lane, 0.0, NEG_INF))

    tile_k0 = pl.multiple_of(q0, tq)
    key_in_tile = lax.broadcasted_iota(jnp.int32, (tq, 1), 0)
    qry_in_tile = lax.broadcasted_iota(jnp.int32, (1, tq), 1)
    sd = (_dot(ks_aug[pl.ds(tile_k0, tq), 0:LANES], q_aug)
          + all_heads(jnp.where(key_in_tile <= qry_in_tile, 0.0, NEG_INF)))

    e = jnp.exp2(sc - col_max(sc))
    p_cmp = (e * all_heads(jnp.where(t_lane >= CMP_LEN - 1, 1.0, 0.0))
             / jnp.sum(e, axis=0, keepdims=True))
    pw = jnp.exp2(sw - col_max(sw)).astype(BF16)
    m0 = col_max(sd)
    pd = jnp.exp2(sd - m0).astype(BF16)
    o_cmp = _dot(cv_ref[0, 0], p_cmp.astype(BF16))
    o_win = normalized(_dot(vw_aug[:, pl.ds(w0, span)], pw))
    acc0 = _dot(vs_aug[:, pl.ds(tile_k0, tq)], pd)

    p_sum = p_cmp[:, 0:tq]
    for r in range(1, NSA_REP):
        p_sum = p_sum + p_cmp[:, r * tq:(r + 1) * tq]
    imp = _dot(ovl_ref[...], p_sum, precision=HIGHEST)
    blk = lax.broadcasted_iota(jnp.int32, (n_slc, tq), 0)
    tl = q0 + lax.broadcasted_iota(jnp.int32, (n_slc, tq), 1)
    cur = _div_pow2(tl, SLC_LEN)
    forced = (blk == 0) | (blk == cur) | (blk == cur - 1)
    val = jnp.where(forced, FORCE_SCORE, jnp.where(blk * SLC_LEN <= tl, imp, -1.0))
    n_grp = n_slc // RANK_GROUP
    val_grp = [val[RANK_GROUP * g:RANK_GROUP * (g + 1)] for g in range(n_grp)]
    row_in_grp = lax.broadcasted_iota(jnp.int32, (RANK_GROUP, tq), 0)
    rank_ref[...] = jnp.zeros_like(rank_ref)
    for mg in range(n_grp):
        @pl.when(mg * RANK_GROUP * SLC_LEN < q0 + tq)
        def _():
            parts = [rank_ref[RANK_GROUP * g:RANK_GROUP * (g + 1), :] for g in range(n_grp)]
            for m in range(RANK_GROUP * mg, RANK_GROUP * (mg + 1)):
                vm = val[m:m + 1, :]
                for g in range(n_grp):
                    if g < mg:
                        beats = jnp.where(vm > val_grp[g], 1.0, 0.0)
                    elif g > mg:
                        beats = jnp.where(vm >= val_grp[g], 1.0, 0.0)
                    else:
                        beats = jnp.where(row_in_grp > m - RANK_GROUP * mg,
                                          jnp.where(vm >= val_grp[g], 1.0, 0.0),
                                          jnp.where(vm > val_grp[g], 1.0, 0.0))
                    parts[g] = parts[g] + beats
            for g in range(n_grp):
                rank_ref[RANK_GROUP * g:RANK_GROUP * (g + 1), :] = parts[g]
    rank = rank_ref[...]

    before_tile = blk * SLC_LEN < q0
    sel_mask = jnp.where(before_tile, jnp.where(rank < top_n, 0.0, NEG_INF), NEG_INF)
    sel_mask = jnp.concatenate([sel_mask, jnp.zeros((LANES - n_slc, tq), F32)], axis=0)
    q_sel = jnp.concatenate([q_aug, all_heads(sel_mask.astype(BF16))], axis=0)

    n_chunks = s_len // SLC_CHUNK
    blk_any = jnp.max(sel_mask[:n_slc], axis=1, keepdims=True)
    blocks_per_chunk = SLC_CHUNK // SLC_LEN
    needed = jnp.int32(0)
    for j in range(n_chunks):
        hit = jnp.max(blk_any[j * blocks_per_chunk:(j + 1) * blocks_per_chunk]) > 0.5 * NEG_INF
        needed = needed | (hit.astype(jnp.int32) << j)
    n_needed = jnp.int32(0)
    for j in range(n_chunks):
        chunk_list[n_needed] = jnp.int32(j)
        n_needed = n_needed + ((needed >> j) & 1)

    n_kb = SLC_CHUNK // SLC_KEY_BLOCK

    def chunk_logits(j, kb):
        kk = pl.multiple_of(j * SLC_CHUNK, SLC_CHUNK) + kb * SLC_KEY_BLOCK
        return _dot(ks_aug[pl.ds(kk, SLC_KEY_BLOCK), :], q_sel)

    m1 = m0
    first = chunk_list[0]
    for kb in range(n_kb):
        s_new = chunk_logits(first, kb)
        s_buf[kb * SLC_KEY_BLOCK:(kb + 1) * SLC_KEY_BLOCK, :] = s_new
        m1 = jnp.maximum(m1, col_max(s_new))

    def slc_step(c, carry):
        m_prev, m_cur, acc = carry
        acc = jnp.exp2(m_prev - m_cur) * acc
        k0 = pl.multiple_of(chunk_list[c] * SLC_CHUNK, SLC_CHUNK)
        j_next = chunk_list[jnp.minimum(c + 1, n_needed - 1)]
        m_next = m_cur
        for kb in range(n_kb):
            blk_rows = slice(kb * SLC_KEY_BLOCK, (kb + 1) * SLC_KEY_BLOCK)
            s_new = chunk_logits(j_next, kb)
            p = jnp.exp2(s_buf[blk_rows, :] - m_cur).astype(BF16)
            acc = acc + _dot(vs_aug[:, pl.ds(k0 + kb * SLC_KEY_BLOCK, SLC_KEY_BLOCK)], p)
            s_buf[blk_rows, :] = s_new
            m_next = jnp.maximum(m_next, col_max(s_new))
        return m_cur, m_next, acc

    _, _, acc = lax.fori_loop(0, n_needed, slc_step, (m0, m1, acc0))
    o_slc = normalized(acc)

    gates = _sigmoid(gate_ref[0]).T

    def gate_rows(br):
        return jnp.concatenate([gates[3 * r + br:3 * r + br + 1, :] for r in range(NSA_REP)], axis=1)

    o = gate_rows(0) * o_cmp + gate_rows(1) * o_slc + gate_rows(2) * o_win
    o = jnp.concatenate([o[:, r * tq:(r + 1) * tq] for r in range(NSA_REP)], axis=0)
    o_ref[0] = o.T.astype(o_ref.dtype)


def nsa_attention(z, zg, ck, cv, zk, zvt, tq=128):
    b, s, _ = z.shape
    n_cmp = ck.shape[2]
    n_slc = s // SLC_LEN
    top_n = min(SLC_TOP, n_slc)
    assert tq == 2 * SLC_LEN and top_n >= 3 and n_slc <= LANES and s % SLC_CHUNK == 0
    c_lo = jnp.arange(n_cmp) * CMP_STRIDE
    s_lo = jnp.arange(n_slc) * SLC_LEN
    ovl = (jnp.clip(jnp.minimum(c_lo[None, :] + CMP_LEN, s_lo[:, None] + SLC_LEN)
                    - jnp.maximum(c_lo[None, :], s_lo[:, None]), 0, None).astype(F32) / CMP_LEN)
    ovl = ovl * (jnp.arange(n_cmp) < n_cmp - 1)[None, :]
    hd = jnp.arange(1, NSA_HEADS + 1, dtype=F32)
    rest = jnp.exp2(-8.0 * hd / NSA_HEADS) * LOG2E
    pieces = []
    for _ in range(SLOPE_PIECES):
        piece = rest.astype(BF16).astype(F32)
        pieces.append(piece)
        rest = rest - piece
    pieces = jnp.stack(pieces * 2, axis=0).reshape(2 * SLOPE_PIECES, NSA_GROUPS, NSA_REP)
    slope_tab = jnp.zeros((NSA_GROUPS, NSA_HEAD_DIM, NSA_REP, tq), F32)
    slope_tab = slope_tab.at[:, :2 * SLOPE_PIECES].set(
        jnp.transpose(pieces, (1, 0, 2))[..., None]).reshape(NSA_GROUPS, NSA_HEAD_DIM, NSA_REP * tq)
    def k_spec(branch):
        return pl.BlockSpec((1, s, LANES), lambda bi, g, i: (bi, 0, branch * NSA_GROUPS + g))

    def v_spec(branch):
        return pl.BlockSpec((NSA_HEAD_DIM, s), lambda bi, g, i: (branch * NSA_GROUPS + g, bi))

    ck_spec = pl.BlockSpec((1, 1, n_cmp, NSA_HEAD_DIM), lambda bi, g, i: (bi, g, 0, 0))
    cv_spec = pl.BlockSpec((1, 1, NSA_HEAD_DIM, n_cmp), lambda bi, g, i: (bi, g, 0, 0))
    gw = NSA_REP * NSA_HEAD_DIM
    return pl.pallas_call(
        functools.partial(_nsa_kernel, top_n=top_n),
        grid=(b, NSA_GROUPS, s // tq),
        in_specs=[pl.BlockSpec((1, tq, gw), lambda bi, g, i: (bi, i, g)),
                  pl.BlockSpec((1, tq, LANES), lambda bi, g, i: (bi, i, g)),
                  pl.BlockSpec((1, NSA_HEAD_DIM, NSA_REP * tq), lambda bi, g, i: (g, 0, 0)),
                  ck_spec, cv_spec, k_spec(0), v_spec(0), k_spec(1), v_spec(1),
                  _const_spec(ovl.shape)],
        out_specs=pl.BlockSpec((1, tq, gw), lambda bi, g, i: (bi, i, g)),
        out_shape=jax.ShapeDtypeStruct((b, s, NSA_GROUPS * gw), BF16),
        scratch_shapes=[pltpu.VMEM((n_cmp, 2 * NSA_HEAD_DIM), BF16),
                        pltpu.VMEM((s, 2 * NSA_HEAD_DIM + LANES), BF16),
                        pltpu.VMEM((s, 2 * NSA_HEAD_DIM), BF16),
                        pltpu.VMEM((NSA_HEAD_DIM + V_PAD_ROWS, s), BF16),
                        pltpu.VMEM((NSA_HEAD_DIM + V_PAD_ROWS, s), BF16),
                        pltpu.VMEM((SLC_CHUNK, NSA_REP * tq), F32),
                        pltpu.VMEM((n_slc, tq), F32),
                        pltpu.SMEM((s // SLC_CHUNK,), jnp.int32)],
        compiler_params=_params("parallel", "parallel", "arbitrary"),
    )(z, zg, slope_tab, ck, cv, zk, zvt, zk, zvt, ovl)


def _pad_cols(w, n):
    return jnp.pad(w, ((0, 0), (0, n - w.shape[1])))


def pool_delta_layer(x, ln, w_in, pool_w, pool_scale, conv_w, a_log, dt_bias, o_norm, w_out):
    b, s, d = x.shape
    main = POOL_WIDTH + 4 * DN_WIDTH
    w_all = jnp.concatenate([w_in[:, :main], _pad_cols(w_in[:, main:], LANES)], axis=1).astype(BF16)
    (z,) = norm_matmul(x.reshape(b * s, d), ln, [w_all], [F32])
    z = z.reshape(b, s, main + LANES)
    y_pool = pool_mixer(z, pool_w.astype(BF16), pool_scale)
    y_dn = gated_deltanet(z, main // LANES, conv_w, a_log, dt_bias, o_norm)
    w_out = w_out.astype(BF16)
    return matmul_res([y_pool.reshape(b * s, POOL_WIDTH), y_dn.reshape(b * s, DN_WIDTH)],
                      [w_out[:POOL_WIDTH], w_out[POOL_WIDTH:]], x.reshape(b * s, d)).reshape(b, s, d)


def _cmp_weights(pe, w1, w2):
    half = (CMP_LEN // 2) * NSA_HEAD_DIM
    w1_pair = jnp.concatenate([w1[:half], w1[half:]], axis=1).astype(BF16)
    pe_rows = jnp.zeros((8, half), F32).at[0:2].set(pe.reshape(2, half)).astype(BF16)
    return w1_pair, pe_rows, w2.astype(BF16)


def nsa_layer(x, ln, w_in, pe_k, w1_k, w2_k, pe_v, w1_v, w2_v, w_out):
    b, s, d = x.shape
    g_, r_, hd = NSA_GROUPS, NSA_REP, NSA_HEAD_DIM
    kvw = g_ * hd
    main = d + 6 * kvw
    wg = w_in[:, main:].reshape(d, g_, r_ * 3)
    wg = jnp.pad(wg, ((0, 0), (0, 0), (0, LANES - r_ * 3))).reshape(d, g_ * LANES)
    cols = lambda j: w_in[:, d + j * kvw:d + (j + 1) * kvw]

    def lane_block_per_group(w):
        return jnp.pad(w.reshape(d, g_, hd), ((0, 0), (0, 0), (0, LANES - hd))).reshape(d, g_ * LANES)

    w_k = jnp.concatenate([lane_block_per_group(cols(2)), lane_block_per_group(cols(4))], axis=1)
    w_vt = jnp.concatenate([cols(3), cols(5)], axis=1).T
    z, zg, zk, zvt = norm_matmul(
        x.reshape(b * s, d), ln,
        [w_in[:, :d + 2 * kvw].astype(BF16), wg.astype(BF16), w_k.astype(BF16)], [BF16, F32, BF16],
        wts=[w_vt.astype(BF16)])
    z = z.reshape(b, s, d + 2 * kvw)
    zg = zg.reshape(b, s, g_ * LANES)
    zk = zk.reshape(b, s, 2 * g_ * LANES)

    def group_major(j):
        return jnp.transpose(z[..., d + j * kvw:d + (j + 1) * kvw].reshape(b, s, g_, hd), (0, 2, 1, 3))

    n_str = s // CMP_STRIDE
    rk = group_major(0).reshape(b, g_, n_str, CMP_STRIDE * hd)
    rv = group_major(1).reshape(b, g_, n_str, CMP_STRIDE * hd)
    ck, cv = compress_kv(rk, rv, *_cmp_weights(pe_k, w1_k, w2_k), *_cmp_weights(pe_v, w1_v, w2_v))
    o = nsa_attention(z, zg, ck, jnp.swapaxes(cv, 2, 3), zk, zvt)
    return matmul_res([o.reshape(b * s, d)], [w_out.astype(BF16)], x.reshape(b * s, d)).reshape(b, s, d)


def kernel(x, mem, a_ln, a_w_in, a_pool_w, a_pool_scale, a_conv_w, a_a_log, a_dt_bias, a_o_norm, a_w_out, c_ln, c_w_in, c_pe_k, c_w1_k, c_w2_k, c_pe_v, c_w1_v, c_w2_v, c_w_out, xa_ln, xa_mem_ln, xa_wq, xa_wk, xa_wv, xa_wo, ff_ln, ff_w1, ff_w2, final_ln):
    b, s, d = x.shape
    depth = xa_ln.shape[0]
    mem2d = mem.reshape(b * mem.shape[1], d)
    for l in range(depth):
        i = l // 2
        if l % 2 == 0:
            x = pool_delta_layer(x, a_ln[i], a_w_in[i], a_pool_w[i], a_pool_scale[i], a_conv_w[i],
                                 a_a_log[i], a_dt_bias[i], a_o_norm[i], a_w_out[i])
        else:
            x = nsa_layer(x, c_ln[i], c_w_in[i], c_pe_k[i], c_w1_k[i], c_w2_k[i],
                          c_pe_v[i], c_w1_v[i], c_w2_v[i], c_w_out[i])
        mk, mv = norm_matmul(mem2d, xa_mem_ln[l], [xa_wk[l].astype(BF16), xa_wv[l].astype(BF16)],
                             [BF16, BF16])
        x = xattn_sublayer(x, xa_ln[l], xa_wq[l].astype(BF16), mk, mv, xa_wo[l].astype(BF16))
        x = mlp_sublayer(x.reshape(b * s, d), ff_ln[l], ff_w1[l].astype(BF16), ff_w2[l].astype(BF16),
                         final_ln, final_norm=(l == depth - 1)).reshape(b, s, d)
    return x
```

```python
import functools
import math

import jax
import jax.numpy as jnp
from jax import lax
from jax.experimental import pallas as pl
from jax.experimental.pallas import tpu as pltpu

F32 = jnp.float32
BF16 = jnp.bfloat16
HIGHEST = lax.Precision.HIGHEST

LANES = 128
VMEM_LIMIT = 56 * 1024 * 1024

EPS = 1e-6
NEG_INF = -1e30
FORCE_SCORE = 1e4

POOL_WINDOWS = (2, 4, 8, 16)
POOL_GROUP_DIM = 128
POOL_WIDTH = 512
DN_HEADS = 4
DN_HEAD_DIM = 128
DN_WIDTH = 512
DN_CONV = 4
DN_CHUNK = 64

NSA_HEAD_DIM = 64
NSA_GROUPS = 4
NSA_REP = 4
NSA_HEADS = 16
CMP_LEN = 32
CMP_STRIDE = 16
CMP_HIDDEN = 128
SLC_LEN = 64
SLC_TOP = 16
WINDOW = 512
SLC_CHUNK = 512
SLC_KEY_BLOCK = 256
RANK_GROUP = 8
V_PAD_ROWS = 16
LOG2E = 1.4426950408889634
SLOPE_PIECES = 3
POS_SPLIT = 64

XA_HEADS = 4
XA_HEAD_DIM = 256


def _params(*sem):
    return pltpu.CompilerParams(dimension_semantics=sem, vmem_limit_bytes=VMEM_LIMIT)


def _const_spec(shape):
    nd = len(shape)
    return pl.BlockSpec(shape, lambda *_: (0,) * nd)


def _rms(x, g):
    return x * lax.rsqrt(jnp.mean(x * x, axis=-1, keepdims=True) + EPS) * g


def _sigmoid(x):
    return 1.0 / (1.0 + jnp.exp(-x))


def _div_pow2(x, n):
    return lax.shift_right_logical(x, jnp.int32(int(math.log2(n))))


def _dot(a, b, precision=None):
    return jnp.dot(a, b, preferred_element_type=F32, precision=precision)


def _dot_nt(a, b, precision=None):
    return lax.dot_general(a, b, (((1,), (1,)), ((), ())), preferred_element_type=F32,
                           precision=precision)


def _dot_tn(a, b):
    return lax.dot_general(a, b, (((0,), (0,)), ((), ())), preferred_element_type=F32)


def _norm_matmul_kernel(x_ref, g_ref, *refs, n_plain):
    n = len(refs) // 2
    hb = _rms(x_ref[...], g_ref[...]).astype(BF16)
    for j, (w_ref, o_ref) in enumerate(zip(refs[:n], refs[n:])):
        if j < n_plain:
            o_ref[...] = _dot(hb, w_ref[...]).astype(o_ref.dtype)
        else:
            o_ref[...] = _dot_nt(w_ref[...], hb).astype(o_ref.dtype)


def norm_matmul(x2d, g, ws, out_dtypes, wts=(), tm=512):
    t, d = x2d.shape
    tm = min(tm, t)
    return pl.pallas_call(
        functools.partial(_norm_matmul_kernel, n_plain=len(ws)),
        grid=(t // tm,),
        in_specs=[pl.BlockSpec((tm, d), lambda i: (i, 0)), _const_spec((1, d))]
        + [_const_spec(w.shape) for w in (*ws, *wts)],
        out_specs=[pl.BlockSpec((tm, w.shape[1]), lambda i: (i, 0)) for w in ws]
        + [pl.BlockSpec((w.shape[0], tm), lambda i: (0, i)) for w in wts],
        out_shape=[jax.ShapeDtypeStruct((t, w.shape[1]), dt) for w, dt in zip(ws, out_dtypes)]
        + [jax.ShapeDtypeStruct((w.shape[0], t), BF16) for w in wts],
        compiler_params=_params("parallel"),
    )(x2d, g.reshape(1, d).astype(F32), *ws, *wts)


def _mlp_kernel(x_ref, g_ref, w1_ref, w2_ref, gf_ref, o_ref, *, ff_chunk, final_norm):
    x = x_ref[...]
    hb = _rms(x, g_ref[...]).astype(BF16)
    acc = x
    for c in range(w1_ref.shape[1] // ff_chunk):
        a = jnp.maximum(_dot(hb, w1_ref[:, c * ff_chunk:(c + 1) * ff_chunk]), 0.0)
        acc = acc + _dot((a * a).astype(BF16), w2_ref[c * ff_chunk:(c + 1) * ff_chunk, :])
    if final_norm:
        acc = _rms(acc, gf_ref[...])
    o_ref[...] = acc


def mlp_sublayer(x2d, g, w1, w2, gf, final_norm, tm=512, ff_chunk=1024):
    t, d = x2d.shape
    tm = min(tm, t)
    return pl.pallas_call(
        functools.partial(_mlp_kernel, ff_chunk=ff_chunk, final_norm=final_norm),
        grid=(t // tm,),
        in_specs=[pl.BlockSpec((tm, d), lambda i: (i, 0)), _const_spec((1, d)),
                  _const_spec(w1.shape), _const_spec(w2.shape), _const_spec((1, d))],
        out_specs=pl.BlockSpec((tm, d), lambda i: (i, 0)),
        out_shape=jax.ShapeDtypeStruct((t, d), F32),
        compiler_params=_params("parallel"),
    )(x2d, g.reshape(1, d).astype(F32), w1, w2, gf.reshape(1, d).astype(F32))


def _xattn_kernel(x_ref, *refs, n_mix):
    g_ref, wq_ref, k_ref, v_ref, wo_ref, o_ref = refs[2 * n_mix:]
    x = x_ref[0]
    for a_ref, w_ref in zip(refs[:n_mix], refs[n_mix:2 * n_mix]):
        x = x + _dot(a_ref[0], w_ref[...])
    hb = _rms(x, g_ref[...]).astype(BF16)
    q = (_dot(hb, wq_ref[...]) * (XA_HEAD_DIM ** -0.5)).astype(BF16)
    heads = []
    for h in range(XA_HEADS):
        sl = slice(h * XA_HEAD_DIM, (h + 1) * XA_HEAD_DIM)
        s = _dot_nt(q[:, sl], k_ref[:, sl])
        p = jnp.exp(s - jnp.max(s, axis=-1, keepdims=True))
        l = jnp.sum(p, axis=-1, keepdims=True)
        heads.append((_dot(p.astype(BF16), v_ref[:, sl]) / l).astype(BF16))
    o_ref[0] = x + _dot(jnp.concatenate(heads, axis=1), wo_ref[...])


def xattn_sublayer(x, mix_a, mix_w, g, wq, k2d, v2d, wo, tq=512):
    b, s, d = x.shape
    tq = min(tq, s)
    m = k2d.shape[0] // b
    return pl.pallas_call(
        functools.partial(_xattn_kernel, n_mix=len(mix_a)),
        grid=(b, s // tq),
        in_specs=[pl.BlockSpec((1, tq, d), lambda bi, i: (bi, i, 0))]
        + [pl.BlockSpec((1, tq, a.shape[2]), lambda bi, i: (bi, i, 0)) for a in mix_a]
        + [_const_spec(w.shape) for w in mix_w]
        + [_const_spec((1, d)), _const_spec(wq.shape),
           pl.BlockSpec((m, d), lambda bi, i: (bi, 0)),
           pl.BlockSpec((m, d), lambda bi, i: (bi, 0)),
           _const_spec(wo.shape)],
        out_specs=pl.BlockSpec((1, tq, d), lambda bi, i: (bi, i, 0)),
        out_shape=jax.ShapeDtypeStruct((b, s, d), F32),
        compiler_params=_params("parallel", "parallel"),
    )(x, *mix_a, *mix_w, g.reshape(1, d).astype(F32), wq, k2d, v2d, wo)


POOL_HALO = 16


def _pool_kernel(u_ref, halo_ref, w_ref, scale_ref, o_ref):
    i = pl.program_id(1)
    ts = u_ref.shape[1]
    u = u_ref[0]
    halo = jnp.where(i == 0, 0.0, halo_ref[0])
    ext = jnp.concatenate([halo, u], axis=0)
    sums = [None] * len(POOL_WINDOWS)
    cur = ext
    for gi, win in enumerate(POOL_WINDOWS):
        cur = cur[:, (POOL_GROUP_DIM if gi else 0):]
        cur = cur + pltpu.roll(cur, win // 2, axis=0)
        sums[gi] = cur[POOL_HALO:, :POOL_GROUP_DIM]
    pos1 = (i * ts + 1 + lax.broadcasted_iota(jnp.int32, (ts, 1), 0)).astype(F32)
    outs = []
    for gi, win in enumerate(POOL_WINDOWS):
        ug = u[:, gi * POOL_GROUP_DIM:(gi + 1) * POOL_GROUP_DIM]
        y = sums[gi] / jnp.minimum(pos1, float(win)) - ug
        outs.append(_dot(y.astype(BF16), w_ref[gi]))
    o_ref[0] = (jnp.concatenate(outs, axis=1) * scale_ref[...]).astype(o_ref.dtype)


def pool_mixer(z, pool_w, pool_scale, ts=512):
    b, s, _ = z.shape
    ts = min(ts, s)
    hb = ts // POOL_HALO
    return pl.pallas_call(
        _pool_kernel,
        grid=(b, s // ts),
        in_specs=[pl.BlockSpec((1, ts, POOL_WIDTH), lambda bi, i: (bi, i, 0)),
                  pl.BlockSpec((1, POOL_HALO, POOL_WIDTH),
                               lambda bi, i: (bi, jnp.maximum(i * hb - 1, 0), 0)),
                  _const_spec(pool_w.shape), _const_spec((1, POOL_WIDTH))],
        out_specs=pl.BlockSpec((1, ts, POOL_WIDTH), lambda bi, i: (bi, i, 0)),
        out_shape=jax.ShapeDtypeStruct((b, s, POOL_WIDTH), BF16),
        compiler_params=_params("parallel", "parallel"),
    )(z, z, pool_w, pool_scale.reshape(1, POOL_WIDTH).astype(F32))


DN_ROWS = DN_HEADS * DN_CHUNK
DN_CHUNKS_PER_STEP = 4
BETA_LANE = 0
ALPHA_LANE = DN_HEADS


def _stack_heads(x):
    return jnp.concatenate([x[:, h * DN_HEAD_DIM:(h + 1) * DN_HEAD_DIM] for h in range(DN_HEADS)],
                           axis=0)


def _stack_cols(x, lane0):
    return jnp.concatenate([x[:, lane0 + h:lane0 + h + 1] for h in range(DN_HEADS)], axis=0)


def _pick_head_block(wide, row_head):
    out = jnp.zeros((DN_ROWS, DN_HEAD_DIM), F32)
    for h in range(DN_HEADS):
        out = jnp.where(row_head == h, wide[:, h * DN_HEAD_DIM:(h + 1) * DN_HEAD_DIM], out)
    return out


def _deltanet_kernel(q_ref, k_ref, v_ref, gate_ref, ba_ref, cw_ref, alog_ref, dtb_ref, onorm_ref,
                     o_ref, state_ref, tail_ref):
    c = pl.program_id(1)

    @pl.when(c == 0)
    def _():
        state_ref[...] = jnp.zeros_like(state_ref)
        tail_ref[...] = jnp.zeros_like(tail_ref)

    blk_len = q_ref.shape[1]
    x3 = jnp.concatenate([q_ref[0], k_ref[0], v_ref[0]], axis=1)
    ext = jnp.concatenate([tail_ref[...], x3], axis=0)
    tail_ref[...] = x3[blk_len - 8:, :]
    cw = cw_ref[...]
    y = cw[DN_CONV - 1:DN_CONV, :] * ext
    for j in range(1, DN_CONV):
        y = y + cw[DN_CONV - 1 - j:DN_CONV - j, :] * pltpu.roll(ext, j, axis=0)
    y = y[8:, :]
    y = y * _sigmoid(y)

    def l2n(a):
        return a * lax.rsqrt(jnp.sum(a * a, axis=-1, keepdims=True) + EPS)

    ba = ba_ref[0]
    beta_all = _sigmoid(ba)
    sp_in = ba + dtb_ref[...]
    softplus = jnp.maximum(sp_in, 0.0) + jnp.log(1.0 + jnp.exp(-jnp.abs(sp_in)))
    g_all = -jnp.exp(alog_ref[...]) * softplus
    ri = lax.broadcasted_iota(jnp.int32, (blk_len, blk_len), 0)
    ci = lax.broadcasted_iota(jnp.int32, (blk_len, blk_len), 1)
    in_chunk_tril = (ri >= ci) & (_div_pow2(ri, DN_CHUNK) == _div_pow2(ci, DN_CHUNK))
    gc_blk = _dot(jnp.where(in_chunk_tril, 1.0, 0.0), g_all, precision=HIGHEST)

    rr = lax.broadcasted_iota(jnp.int32, (DN_ROWS, DN_ROWS), 0)
    cc = lax.broadcasted_iota(jnp.int32, (DN_ROWS, DN_ROWS), 1)
    same_head = _div_pow2(rr, DN_CHUNK) == _div_pow2(cc, DN_CHUNK)
    causal = same_head & (rr >= cc)
    strict = same_head & (rr > cc)
    lane_head = _div_pow2(lax.broadcasted_iota(jnp.int32, (DN_CHUNK, DN_ROWS), 1), DN_CHUNK)

    def block_diag(w):
        return jnp.concatenate([jnp.where(lane_head == h, w, 0.0) for h in range(DN_HEADS)], axis=0)

    def hi_lo(x):
        hi = x.astype(BF16).astype(F32)
        return hi, x - hi

    def times_p(x, p_wide):
        p_hi, p_lo = hi_lo(p_wide)
        d_hi, d_lo = block_diag(p_hi).astype(BF16), block_diag(p_lo).astype(BF16)
        x_hi, x_lo = hi_lo(x)
        n = x.shape[0]
        top = _dot(jnp.concatenate([x_hi, x_lo], axis=0).astype(BF16), d_hi)
        return top[:n] + top[n:] + _dot(x_hi.astype(BF16), d_lo)

    wr = lax.broadcasted_iota(jnp.int32, (DN_CHUNK, DN_ROWS), 0)
    wc = lax.broadcasted_iota(jnp.int32, (DN_CHUNK, DN_ROWS), 1)
    eye_wide = jnp.where(wr == (wc & (DN_CHUNK - 1)), 1.0, 0.0)
    n_sq = int(math.log2(DN_CHUNK)) - 1

    def chunk_prep(ci):
        rows = slice(ci * DN_CHUNK, (ci + 1) * DN_CHUNK)
        yc = y[rows, :]
        q_st = l2n(_stack_heads(yc[:, :DN_WIDTH])) * (DN_HEAD_DIM ** -0.5)
        k_st = l2n(_stack_heads(yc[:, DN_WIDTH:2 * DN_WIDTH]))
        v_st = _stack_heads(yc[:, 2 * DN_WIDTH:])
        gc_all = gc_blk[rows, :]
        beta_st = _stack_cols(beta_all[rows, :], BETA_LANE)
        gc_st = _stack_cols(gc_all, ALPHA_LANE)
        g_last = [gc_all[DN_CHUNK - 1:DN_CHUNK, ALPHA_LANE + h:ALPHA_LANE + h + 1]
                  for h in range(DN_HEADS)]
        glast_st = jnp.concatenate([jnp.broadcast_to(g, (DN_CHUNK, 1)) for g in g_last], axis=0)
        gcb = jnp.broadcast_to(gc_st, (DN_ROWS, DN_ROWS))
        decay = jnp.where(causal, jnp.exp(jnp.where(causal, gcb - gcb.T, 0.0)), 0.0)
        kb_st = k_st * beta_st
        k_bf = k_st.astype(BF16)
        a_low = jnp.where(strict, _dot_nt(kb_st.astype(BF16), k_bf) * decay, 0.0)
        p = -(a_low[0:DN_CHUNK] + a_low[DN_CHUNK:2 * DN_CHUNK]
              + a_low[2 * DN_CHUNK:3 * DN_CHUNK] + a_low[3 * DN_CHUNK:])
        t_wide = eye_wide + p
        yield None
        for j in range(n_sq):
            if j == 0:
                p = times_p(p, p)
            else:
                both = times_p(jnp.concatenate([p, t_wide], axis=0), p)
                p, t_wide = both[:DN_CHUNK], t_wide + both[DN_CHUNK:]
            yield None
        t_wide = t_wide + times_p(t_wide, p)
        t_bf = block_diag(t_wide).astype(BF16)
        egc = jnp.exp(gc_st)
        wu = _dot(t_bf, jnp.concatenate([(kb_st * egc).astype(BF16),
                                         (v_st * beta_st).astype(BF16)], axis=1))
        attn = jnp.where(causal, _dot_nt(q_st.astype(BF16), k_bf) * decay, 0.0)
        yield dict(
            w=wu[:, :DN_HEAD_DIM].astype(BF16), u=wu[:, DN_HEAD_DIM:], attn=attn.astype(BF16),
            q_dec=(q_st * egc).astype(BF16), k_dec=(k_st * jnp.exp(glast_st - gc_st)).astype(BF16),
            state_scale=jnp.concatenate(
                [jnp.broadcast_to(jnp.exp(g), (1, DN_HEAD_DIM)) for g in g_last], axis=1))

    preps = [chunk_prep(ci) for ci in range(blk_len // DN_CHUNK)]
    for _ in range(n_sq + 1):
        for prep in preps:
            next(prep)
    preps = [next(prep) for prep in preps]

    state = state_ref[...]
    row_head = _div_pow2(lax.broadcasted_iota(jnp.int32, (DN_ROWS, 1), 0), DN_CHUNK)
    gate = gate_ref[0]
    out_rows = []
    for ci, pr in enumerate(preps):
        s_bf = state.astype(BF16)
        v_new = pr["u"] - _pick_head_block(_dot(pr["w"], s_bf), row_head)
        v_new_bf = v_new.astype(BF16)
        o_st = _pick_head_block(_dot(pr["q_dec"], s_bf), row_head) + _dot(pr["attn"], v_new_bf)
        zero = jnp.zeros_like(v_new_bf)
        v_wide = jnp.concatenate([jnp.where(row_head == h, v_new_bf, zero) for h in range(DN_HEADS)],
                                 axis=1)
        state = state * pr["state_scale"] + _dot_tn(pr["k_dec"], v_wide)
        outs = []
        for h in range(DN_HEADS):
            o_h = _rms(o_st[h * DN_CHUNK:(h + 1) * DN_CHUNK, :], onorm_ref[...])
            g_h = gate[ci * DN_CHUNK:(ci + 1) * DN_CHUNK, h * DN_HEAD_DIM:(h + 1) * DN_HEAD_DIM]
            outs.append(o_h * (g_h * _sigmoid(g_h)))
        out_rows.append(jnp.concatenate(outs, axis=1))
    state_ref[...] = state
    o_ref[0] = jnp.concatenate(out_rows, axis=0).astype(o_ref.dtype)


def gated_deltanet(z, ba_block, conv_w, a_log, dt_bias, o_norm):
    b, s, _ = z.shape
    lane_row = lambda vals, lane0: jnp.zeros((1, LANES), F32).at[0, lane0:lane0 + DN_HEADS].set(
        vals.astype(F32))
    blk_len = min(DN_CHUNKS_PER_STEP * DN_CHUNK, s)
    col = lambda j: pl.BlockSpec((1, blk_len, DN_WIDTH), lambda bi, ci: (bi, ci, j))
    return pl.pallas_call(
        _deltanet_kernel,
        grid=(b, s // blk_len),
        in_specs=[col(1), col(2), col(3), col(4),
                  pl.BlockSpec((1, blk_len, LANES), lambda bi, ci: (bi, ci, ba_block)),
                  _const_spec(conv_w.shape), _const_spec((1, LANES)), _const_spec((1, LANES)),
                  _const_spec((1, DN_HEAD_DIM))],
        out_specs=pl.BlockSpec((1, blk_len, DN_WIDTH), lambda bi, ci: (bi, ci, 0)),
        out_shape=jax.ShapeDtypeStruct((b, s, DN_WIDTH), BF16),
        scratch_shapes=[pltpu.VMEM((DN_HEAD_DIM, DN_WIDTH), F32),
                        pltpu.VMEM((8, 3 * DN_WIDTH), F32)],
        compiler_params=_params("parallel", "arbitrary"),
    )(z, z, z, z, z, conv_w.astype(F32), lane_row(a_log, ALPHA_LANE), lane_row(dt_bias, ALPHA_LANE),
      o_norm.reshape(1, DN_HEAD_DIM).astype(F32))


def _compress_kernel(rk_ref, rv_ref, w1k_ref, pek_ref, w2k_ref, w1v_ref, pev_ref, w2v_ref,
                     ck_ref, cv_ref):
    n = rk_ref.shape[2]
    row = lax.broadcasted_iota(jnp.int32, (n, 1), 0)

    def one(r_ref, w1_ref, pe_ref, w2_ref, o_ref):
        w1 = w1_ref[...]
        pb = _dot(pe_ref[...], w1)
        bias = pb[0:1, :CMP_HIDDEN] + pb[1:2, CMP_HIDDEN:]
        for g in range(NSA_GROUPS):
            y = _dot(r_ref[0, g], w1)
            h = y[:, :CMP_HIDDEN] + pltpu.roll(y[:, CMP_HIDDEN:], n - 1, axis=0) + bias
            a = (h * _sigmoid(h)).astype(BF16)
            o_ref[0, g] = jnp.where(row < n - 1, _dot(a, w2_ref[...]), 0.0).astype(o_ref.dtype)

    one(rk_ref, w1k_ref, pek_ref, w2k_ref, ck_ref)
    one(rv_ref, w1v_ref, pev_ref, w2v_ref, cv_ref)


def compress_kv(rk, rv, w1k, pek, w2k, w1v, pev, w2v):
    b, g, n, w = rk.shape
    blk = pl.BlockSpec((1, g, n, w), lambda bi: (bi, 0, 0, 0))
    oblk = pl.BlockSpec((1, g, n, NSA_HEAD_DIM), lambda bi: (bi, 0, 0, 0))
    return pl.pallas_call(
        _compress_kernel,
        grid=(b,),
        in_specs=[blk, blk, _const_spec(w1k.shape), _const_spec(pek.shape), _const_spec(w2k.shape),
                  _const_spec(w1v.shape), _const_spec(pev.shape), _const_spec(w2v.shape)],
        out_specs=[oblk, oblk],
        out_shape=[jax.ShapeDtypeStruct((b, g, n, NSA_HEAD_DIM), BF16)] * 2,
        compiler_params=_params("parallel"),
    )(rk, rv, w1k, pek, w2k, w1v, pev, w2v)


def _aug_keys(k, pos):
    n = k.shape[0]
    lane = lax.broadcasted_iota(jnp.int32, (n, NSA_HEAD_DIM), 1)
    hi = (_div_pow2(pos, POS_SPLIT) * POS_SPLIT).astype(F32)
    lo = (pos & (POS_SPLIT - 1)).astype(F32)
    cols = jnp.where(lane < SLOPE_PIECES, hi, jnp.where(lane < 2 * SLOPE_PIECES, lo, 0.0))
    return jnp.concatenate([k[:, :NSA_HEAD_DIM], cols.astype(BF16)], axis=1)


def _nsa_kernel(q_ref, gate_ref, slope_ref, ck_ref, cv_ref, ks_ref, vs_ref, kw_ref, vw_ref,
                ovl_ref, o_ref, kc_aug, ks_aug, kw_aug, vs_aug, vw_aug, s_buf, rank_ref, chunk_list,
                *, top_n):
    i = pl.program_id(2)
    tq = q_ref.shape[1]
    rows = NSA_REP * tq
    n_cmp = ck_ref.shape[2]
    n_slc = ovl_ref.shape[0]
    s_len = ks_ref.shape[1]
    q0 = i * tq

    @pl.when(i == 0)
    def _():
        cpos = lax.broadcasted_iota(jnp.int32, (n_cmp, 1), 0) * CMP_STRIDE + (CMP_LEN - 1)
        kc_aug[...] = _aug_keys(ck_ref[0, 0], cpos)

        def fill(c, carry):
            r0 = pl.multiple_of(c * SLC_CHUNK, SLC_CHUNK)
            pos = r0 + lax.broadcasted_iota(jnp.int32, (SLC_CHUNK, 1), 0)
            blk_lane = lax.broadcasted_iota(jnp.int32, (SLC_CHUNK, LANES), 1)
            onehot = jnp.where(blk_lane == _div_pow2(pos, SLC_LEN), 1.0, 0.0).astype(BF16)
            ks_aug[pl.ds(r0, SLC_CHUNK), :] = jnp.concatenate(
                [_aug_keys(ks_ref[0, pl.ds(r0, SLC_CHUNK), :], pos), onehot], axis=1)
            kw_aug[pl.ds(r0, SLC_CHUNK), :] = _aug_keys(kw_ref[0, pl.ds(r0, SLC_CHUNK), :], pos)
            return carry

        lax.fori_loop(0, s_len // SLC_CHUNK, fill, 0)
        ones_rows = jnp.where(lax.broadcasted_iota(jnp.int32, (V_PAD_ROWS, s_len), 0) == 0, 1.0, 0.0)
        for v_ref, v_aug in ((vs_ref, vs_aug), (vw_ref, vw_aug)):
            v_aug[0:NSA_HEAD_DIM, :] = v_ref[...]
            v_aug[NSA_HEAD_DIM:, :] = ones_rows.astype(BF16)

    q_t = (q_ref[0].astype(F32) * (NSA_HEAD_DIM ** -0.5 * LOG2E)).T
    q_t = jnp.concatenate([q_t[r * NSA_HEAD_DIM:(r + 1) * NSA_HEAD_DIM, :] for r in range(NSA_REP)],
                          axis=1).astype(BF16)
    q_aug = jnp.concatenate([q_t, slope_ref[0].astype(BF16)], axis=0)
    t_lane = q0 + lax.broadcasted_iota(jnp.int32, (1, tq), 1)

    def all_heads(x):
        return jnp.concatenate([x] * NSA_REP, axis=1)

    def col_max(x):
        return jnp.max(x, axis=0, keepdims=True)

    def normalized(acc):
        return acc[:NSA_HEAD_DIM] / acc[NSA_HEAD_DIM:NSA_HEAD_DIM + 1]

    span = min(WINDOW + tq, s_len)
    w0 = pl.multiple_of(jnp.maximum(q0 + tq - span, 0), tq)
    dist_w = t_lane - (w0 + lax.broadcasted_iota(jnp.int32, (span, 1), 0))
    band = jnp.where(dist_w >= 0, jnp.where(dist_w < WINDOW, 0.0, NEG_INF), NEG_INF)
    sw = _dot(kw_aug[pl.ds(w0, span), :], q_aug) + all_heads(band)

    cend = lax.broadcasted_iota(jnp.int32, (n_cmp, 1), 0) * CMP_STRIDE + (CMP_LEN - 1)
    sc = _dot(kc_aug[...], q_aug) + all_heads(jnp.where(cend <= t_lane, 0.0, NEG_INF))

    tile_k0 = pl.multiple_of(q0, tq)
    key_in_tile = lax.broadcasted_iota(jnp.int32, (tq, 1), 0)
    qry_in_tile = lax.broadcasted_iota(jnp.int32, (1, tq), 1)
    sd = (_dot(ks_aug[pl.ds(tile_k0, tq), 0:LANES], q_aug)
          + all_heads(jnp.where(key_in_tile <= qry_in_tile, 0.0, NEG_INF)))

    e = jnp.exp2(sc - col_max(sc))
    any_visible = all_heads(jnp.where(t_lane >= CMP_LEN - 1, 1.0, 0.0))
    p_cmp = e * (any_visible / jnp.sum(e, axis=0, keepdims=True))
    o_cmp = _dot(cv_ref[0, 0], p_cmp.astype(BF16))

    p_sum = p_cmp[:, 0:tq]
    for r in range(1, NSA_REP):
        p_sum = p_sum + p_cmp[:, r * tq:(r + 1) * tq]
    ovl = ovl_ref[...].astype(BF16)
    imp = jnp.zeros((n_slc, tq), F32)
    rest = p_sum
    for _ in range(3):
        piece = rest.astype(BF16)
        imp = imp + _dot(ovl, piece)
        rest = rest - piece.astype(F32)
    blk = lax.broadcasted_iota(jnp.int32, (n_slc, tq), 0)
    tl = q0 + lax.broadcasted_iota(jnp.int32, (n_slc, tq), 1)
    cur = _div_pow2(tl, SLC_LEN)
    forced = (blk == 0) | (blk == cur) | (blk == cur - 1)
    val = jnp.where(forced, FORCE_SCORE, jnp.where(blk * SLC_LEN <= tl, imp, -1.0))
    n_grp = n_slc // RANK_GROUP
    val_grp = [val[RANK_GROUP * g:RANK_GROUP * (g + 1)] for g in range(n_grp)]
    row_in_grp = lax.broadcasted_iota(jnp.int32, (RANK_GROUP, tq), 0)
    rank_ref[...] = jnp.zeros_like(rank_ref)
    for mg in range(n_grp):
        @pl.when(mg * RANK_GROUP * SLC_LEN < q0 + tq)
        def _():
            parts = [rank_ref[RANK_GROUP * g:RANK_GROUP * (g + 1), :] for g in range(n_grp)]
            for m in range(RANK_GROUP * mg, RANK_GROUP * (mg + 1)):
                vm = val[m:m + 1, :]
                for g in range(n_grp):
                    if g < mg:
                        beats = jnp.where(vm > val_grp[g], 1.0, 0.0)
                    elif g > mg:
                        beats = jnp.where(vm >= val_grp[g], 1.0, 0.0)
                    else:
                        beats = jnp.where(row_in_grp > m - RANK_GROUP * mg,
                                          jnp.where(vm >= val_grp[g], 1.0, 0.0),
                                          jnp.where(vm > val_grp[g], 1.0, 0.0))
                    parts[g] = parts[g] + beats
            for g in range(n_grp):
                rank_ref[RANK_GROUP * g:RANK_GROUP * (g + 1), :] = parts[g]
    rank = rank_ref[...]

    before_tile = blk * SLC_LEN < q0
    sel_mask = jnp.where(before_tile, jnp.where(rank < top_n, 0.0, NEG_INF), NEG_INF)
    sel_mask = jnp.concatenate([sel_mask, jnp.zeros((LANES - n_slc, tq), F32)], axis=0)
    q_sel = jnp.concatenate([q_aug, all_heads(sel_mask.astype(BF16))], axis=0)

    n_chunks = s_len // SLC_CHUNK
    blk_any = jnp.max(sel_mask[:n_slc], axis=1, keepdims=True)
    blocks_per_chunk = SLC_CHUNK // SLC_LEN
    needed = jnp.int32(0)
    for j in range(n_chunks):
        hit = jnp.max(blk_any[j * blocks_per_chunk:(j + 1) * blocks_per_chunk]) > 0.5 * NEG_INF
        needed = needed | (hit.astype(jnp.int32) << j)

    pw = jnp.exp2(sw - col_max(sw)).astype(BF16)
    m0 = col_max(sd)
    pd = jnp.exp2(sd - m0).astype(BF16)
    o_win = normalized(_dot(vw_aug[:, pl.ds(w0, span)], pw))
    acc0 = _dot(vs_aug[:, pl.ds(tile_k0, tq)], pd)

    n_needed = jnp.int32(0)
    for j in range(n_chunks):
        chunk_list[n_needed] = jnp.int32(j)
        n_needed = n_needed + ((needed >> j) & 1)

    n_kb = SLC_CHUNK // SLC_KEY_BLOCK

    def chunk_logits(j, kb):
        kk = pl.multiple_of(j * SLC_CHUNK, SLC_CHUNK) + kb * SLC_KEY_BLOCK
        return _dot(ks_aug[pl.ds(kk, SLC_KEY_BLOCK), :], q_sel)

    m1 = m0
    first = chunk_list[0]
    for kb in range(n_kb):
        s_new = chunk_logits(first, kb)
        s_buf[kb * SLC_KEY_BLOCK:(kb + 1) * SLC_KEY_BLOCK, :] = s_new
        m1 = jnp.maximum(m1, col_max(s_new))

    def slc_step(c, carry):
        m_prev, m_cur, acc = carry
        acc = jnp.exp2(m_prev - m_cur) * acc
        k0 = pl.multiple_of(chunk_list[c] * SLC_CHUNK, SLC_CHUNK)
        j_next = chunk_list[jnp.minimum(c + 1, n_needed - 1)]
        m_next = m_cur
        for kb in range(n_kb):
            blk_rows = slice(kb * SLC_KEY_BLOCK, (kb + 1) * SLC_KEY_BLOCK)
            s_new = chunk_logits(j_next, kb)
            p = jnp.exp2(s_buf[blk_rows, :] - m_cur).astype(BF16)
            acc = acc + _dot(vs_aug[:, pl.ds(k0 + kb * SLC_KEY_BLOCK, SLC_KEY_BLOCK)], p)
            s_buf[blk_rows, :] = s_new
            m_next = jnp.maximum(m_next, col_max(s_new))
        return m_cur, m_next, acc

    _, _, acc = lax.fori_loop(0, n_needed, slc_step, (m0, m1, acc0))
    o_slc = normalized(acc)

    gates = _sigmoid(gate_ref[0]).T

    def gate_rows(br):
        return jnp.concatenate([gates[3 * r + br:3 * r + br + 1, :] for r in range(NSA_REP)], axis=1)

    o = gate_rows(0) * o_cmp + gate_rows(1) * o_slc + gate_rows(2) * o_win
    o = jnp.concatenate([o[:, r * tq:(r + 1) * tq] for r in range(NSA_REP)], axis=0)
    o_ref[0] = o.T.astype(o_ref.dtype)


def nsa_attention(z, zg, ck, cv, zk, zvt, tq=128):
    b, s, _ = z.shape
    n_cmp = ck.shape[2]
    n_slc = s // SLC_LEN
    top_n = min(SLC_TOP, n_slc)
    assert tq == 2 * SLC_LEN and top_n >= 3 and n_slc <= LANES and s % SLC_CHUNK == 0
    c_lo = jnp.arange(n_cmp) * CMP_STRIDE
    s_lo = jnp.arange(n_slc) * SLC_LEN
    ovl = (jnp.clip(jnp.minimum(c_lo[None, :] + CMP_LEN, s_lo[:, None] + SLC_LEN)
                    - jnp.maximum(c_lo[None, :], s_lo[:, None]), 0, None).astype(F32) / CMP_LEN)
    ovl = ovl * (jnp.arange(n_cmp) < n_cmp - 1)[None, :]
    hd = jnp.arange(1, NSA_HEADS + 1, dtype=F32)
    rest = jnp.exp2(-8.0 * hd / NSA_HEADS) * LOG2E
    pieces = []
    for _ in range(SLOPE_PIECES):
        piece = rest.astype(BF16).astype(F32)
        pieces.append(piece)
        rest = rest - piece
    pieces = jnp.stack(pieces * 2, axis=0).reshape(2 * SLOPE_PIECES, NSA_GROUPS, NSA_REP)
    slope_tab = jnp.zeros((NSA_GROUPS, NSA_HEAD_DIM, NSA_REP, tq), F32)
    slope_tab = slope_tab.at[:, :2 * SLOPE_PIECES].set(
        jnp.transpose(pieces, (1, 0, 2))[..., None]).reshape(NSA_GROUPS, NSA_HEAD_DIM, NSA_REP * tq)
    def k_spec(branch):
        return pl.BlockSpec((1, s, LANES), lambda bi, g, i: (bi, 0, branch * NSA_GROUPS + g))

    def v_spec(branch):
        return pl.BlockSpec((NSA_HEAD_DIM, s), lambda bi, g, i: (branch * NSA_GROUPS + g, bi))

    ck_spec = pl.BlockSpec((1, 1, n_cmp, NSA_HEAD_DIM), lambda bi, g, i: (bi, g, 0, 0))
    cv_spec = pl.BlockSpec((1, 1, NSA_HEAD_DIM, n_cmp), lambda bi, g, i: (bi, g, 0, 0))
    gw = NSA_REP * NSA_HEAD_DIM
    return pl.pallas_call(
        functools.partial(_nsa_kernel, top_n=top_n),
        grid=(b, NSA_GROUPS, s // tq),
        in_specs=[pl.BlockSpec((1, tq, gw), lambda bi, g, i: (bi, i, g)),
                  pl.BlockSpec((1, tq, LANES), lambda bi, g, i: (bi, i, g)),
                  pl.BlockSpec((1, NSA_HEAD_DIM, NSA_REP * tq), lambda bi, g, i: (g, 0, 0)),
                  ck_spec, cv_spec, k_spec(0), v_spec(0), k_spec(1), v_spec(1),
                  _const_spec(ovl.shape)],
        out_specs=pl.BlockSpec((1, tq, gw), lambda bi, g, i: (bi, i, g)),
        out_shape=jax.ShapeDtypeStruct((b, s, NSA_GROUPS * gw), BF16),
        scratch_shapes=[pltpu.VMEM((n_cmp, 2 * NSA_HEAD_DIM), BF16),
                        pltpu.VMEM((s, 2 * NSA_HEAD_DIM + LANES), BF16),
                        pltpu.VMEM((s, 2 * NSA_HEAD_DIM), BF16),
                        pltpu.VMEM((NSA_HEAD_DIM + V_PAD_ROWS, s), BF16),
                        pltpu.VMEM((NSA_HEAD_DIM + V_PAD_ROWS, s), BF16),
                        pltpu.VMEM((SLC_CHUNK, NSA_REP * tq), F32),
                        pltpu.VMEM((n_slc, tq), F32),
                        pltpu.SMEM((s // SLC_CHUNK,), jnp.int32)],
        compiler_params=_params("parallel", "parallel", "arbitrary"),
    )(z, zg, slope_tab, ck, cv, zk, zvt, zk, zvt, ovl)


def _pad_cols(w, n):
    return jnp.pad(w, ((0, 0), (0, n - w.shape[1])))


def pool_delta_layer(x, ln, w_in, pool_w, pool_scale, conv_w, a_log, dt_bias, o_norm, w_out):
    b, s, d = x.shape
    main = POOL_WIDTH + 4 * DN_WIDTH
    w_all = jnp.concatenate([w_in[:, :main], _pad_cols(w_in[:, main:], LANES)], axis=1).astype(BF16)
    (z,) = norm_matmul(x.reshape(b * s, d), ln, [w_all], [F32])
    z = z.reshape(b, s, main + LANES)
    y_pool = pool_mixer(z, pool_w.astype(BF16), pool_scale)
    y_dn = gated_deltanet(z, main // LANES, conv_w, a_log, dt_bias, o_norm)
    w_out = w_out.astype(BF16)
    return [y_pool, y_dn], [w_out[:POOL_WIDTH], w_out[POOL_WIDTH:]]


def _cmp_weights(pe, w1, w2):
    half = (CMP_LEN // 2) * NSA_HEAD_DIM
    w1_pair = jnp.concatenate([w1[:half], w1[half:]], axis=1).astype(BF16)
    pe_rows = jnp.zeros((8, half), F32).at[0:2].set(pe.reshape(2, half)).astype(BF16)
    return w1_pair, pe_rows, w2.astype(BF16)


def nsa_layer(x, ln, w_in, pe_k, w1_k, w2_k, pe_v, w1_v, w2_v, w_out):
    b, s, d = x.shape
    g_, r_, hd = NSA_GROUPS, NSA_REP, NSA_HEAD_DIM
    kvw = g_ * hd
    main = d + 6 * kvw
    wg = w_in[:, main:].reshape(d, g_, r_ * 3)
    wg = jnp.pad(wg, ((0, 0), (0, 0), (0, LANES - r_ * 3))).reshape(d, g_ * LANES)
    cols = lambda j: w_in[:, d + j * kvw:d + (j + 1) * kvw]

    def lane_block_per_group(w):
        return jnp.pad(w.reshape(d, g_, hd), ((0, 0), (0, 0), (0, LANES - hd))).reshape(d, g_ * LANES)

    w_k = jnp.concatenate([lane_block_per_group(cols(2)), lane_block_per_group(cols(4))], axis=1)
    w_vt = jnp.concatenate([cols(3), cols(5)], axis=1).T
    z, zg, zk, zvt = norm_matmul(
        x.reshape(b * s, d), ln,
        [w_in[:, :d + 2 * kvw].astype(BF16), wg.astype(BF16), w_k.astype(BF16)], [BF16, F32, BF16],
        wts=[w_vt.astype(BF16)])
    z = z.reshape(b, s, d + 2 * kvw)
    zg = zg.reshape(b, s, g_ * LANES)
    zk = zk.reshape(b, s, 2 * g_ * LANES)

    def group_major(j):
        return jnp.transpose(z[..., d + j * kvw:d + (j + 1) * kvw].reshape(b, s, g_, hd), (0, 2, 1, 3))

    n_str = s // CMP_STRIDE
    rk = group_major(0).reshape(b, g_, n_str, CMP_STRIDE * hd)
    rv = group_major(1).reshape(b, g_, n_str, CMP_STRIDE * hd)
    ck, cv = compress_kv(rk, rv, *_cmp_weights(pe_k, w1_k, w2_k), *_cmp_weights(pe_v, w1_v, w2_v))
    o = nsa_attention(z, zg, ck, jnp.swapaxes(cv, 2, 3), zk, zvt)
    return [o], [w_out.astype(BF16)]


def kernel(x, mem, a_ln, a_w_in, a_pool_w, a_pool_scale, a_conv_w, a_a_log, a_dt_bias, a_o_norm, a_w_out, c_ln, c_w_in, c_pe_k, c_w1_k, c_w2_k, c_pe_v, c_w1_v, c_w2_v, c_w_out, xa_ln, xa_mem_ln, xa_wq, xa_wk, xa_wv, xa_wo, ff_ln, ff_w1, ff_w2, final_ln):
    b, s, d = x.shape
    depth = xa_ln.shape[0]
    mem2d = mem.reshape(b * mem.shape[1], d)
    for l in range(depth):
        i = l // 2
        if l % 2 == 0:
            mix_a, mix_w = pool_delta_layer(x, a_ln[i], a_w_in[i], a_pool_w[i], a_pool_scale[i],
                                            a_conv_w[i], a_a_log[i], a_dt_bias[i], a_o_norm[i],
                                            a_w_out[i])
        else:
            mix_a, mix_w = nsa_layer(x, c_ln[i], c_w_in[i], c_pe_k[i], c_w1_k[i], c_w2_k[i],
                                     c_pe_v[i], c_w1_v[i], c_w2_v[i], c_w_out[i])
        mk, mv = norm_matmul(mem2d, xa_mem_ln[l], [xa_wk[l].astype(BF16), xa_wv[l].astype(BF16)],
                             [BF16, BF16])
        x = xattn_sublayer(x, mix_a, mix_w, xa_ln[l], xa_wq[l].astype(BF16), mk, mv,
                           xa_wo[l].astype(BF16))
        x = mlp_sublayer(x.reshape(b * s, d), ff_ln[l], ff_w1[l].astype(BF16), ff_w2[l].astype(BF16),
                         final_ln, final_norm=(l == depth - 1)).reshape(b, s, d)
    return x
```

```python
import functools
import math

import jax
import jax.numpy as jnp
from jax import lax
from jax.experimental import pallas as pl
from jax.experimental.pallas import tpu as pltpu

F32 = jnp.float32
BF16 = jnp.bfloat16
HIGHEST = lax.Precision.HIGHEST

LANES = 128
VMEM_LIMIT = 56 * 1024 * 1024

EPS = 1e-6
NEG_INF = -1e30
FORCE_SCORE = 1e4

POOL_WINDOWS = (2, 4, 8, 16)
POOL_GROUP_DIM = 128
POOL_WIDTH = 512
DN_HEADS = 4
DN_HEAD_DIM = 128
DN_WIDTH = 512
DN_CONV = 4
DN_CHUNK = 64

NSA_HEAD_DIM = 64
NSA_GROUPS = 4
NSA_REP = 4
NSA_HEADS = 16
CMP_LEN = 32
CMP_STRIDE = 16
CMP_HIDDEN = 128
SLC_LEN = 64
SLC_TOP = 16
WINDOW = 512
SLC_CHUNK = 512
SLC_KEY_BLOCK = 256
RANK_GROUP = 8
V_PAD_ROWS = 16
LOG2E = 1.4426950408889634
SLOPE_PIECES = 3
POS_SPLIT = 64

XA_HEADS = 4
XA_HEAD_DIM = 256


def _params(*sem):
    return pltpu.CompilerParams(dimension_semantics=sem, vmem_limit_bytes=VMEM_LIMIT)


def _const_spec(shape):
    nd = len(shape)
    return pl.BlockSpec(shape, lambda *_: (0,) * nd)


def _rms(x, g):
    return x * lax.rsqrt(jnp.mean(x * x, axis=-1, keepdims=True) + EPS) * g


def _sigmoid(x):
    return 1.0 / (1.0 + jnp.exp(-x))


def _div_pow2(x, n):
    return lax.shift_right_logical(x, jnp.int32(int(math.log2(n))))


def _dot(a, b, precision=None):
    return jnp.dot(a, b, preferred_element_type=F32, precision=precision)


def _dot_nt(a, b, precision=None):
    return lax.dot_general(a, b, (((1,), (1,)), ((), ())), preferred_element_type=F32,
                           precision=precision)


def _dot_tn(a, b):
    return lax.dot_general(a, b, (((0,), (0,)), ((), ())), preferred_element_type=F32)


def _norm_matmul_kernel(x_ref, g_ref, *refs, n_plain):
    n = len(refs) // 2
    hb = _rms(x_ref[...], g_ref[...]).astype(BF16)
    for j, (w_ref, o_ref) in enumerate(zip(refs[:n], refs[n:])):
        if j < n_plain:
            o_ref[...] = _dot(hb, w_ref[...]).astype(o_ref.dtype)
        else:
            o_ref[...] = _dot_nt(w_ref[...], hb).astype(o_ref.dtype)


def norm_matmul(x2d, g, ws, out_dtypes, wts=(), tm=512):
    t, d = x2d.shape
    tm = min(tm, t)
    return pl.pallas_call(
        functools.partial(_norm_matmul_kernel, n_plain=len(ws)),
        grid=(t // tm,),
        in_specs=[pl.BlockSpec((tm, d), lambda i: (i, 0)), _const_spec((1, d))]
        + [_const_spec(w.shape) for w in (*ws, *wts)],
        out_specs=[pl.BlockSpec((tm, w.shape[1]), lambda i: (i, 0)) for w in ws]
        + [pl.BlockSpec((w.shape[0], tm), lambda i: (0, i)) for w in wts],
        out_shape=[jax.ShapeDtypeStruct((t, w.shape[1]), dt) for w, dt in zip(ws, out_dtypes)]
        + [jax.ShapeDtypeStruct((w.shape[0], t), BF16) for w in wts],
        compiler_params=_params("parallel"),
    )(x2d, g.reshape(1, d).astype(F32), *ws, *wts)


def _mlp_kernel(x_ref, g_ref, w1_ref, w2_ref, gf_ref, o_ref, *, ff_chunk, final_norm):
    x = x_ref[...]
    hb = _rms(x, g_ref[...]).astype(BF16)
    acc = x
    for c in range(w1_ref.shape[1] // ff_chunk):
        a = jnp.maximum(_dot(hb, w1_ref[:, c * ff_chunk:(c + 1) * ff_chunk]), 0.0)
        acc = acc + _dot((a * a).astype(BF16), w2_ref[c * ff_chunk:(c + 1) * ff_chunk, :])
    if final_norm:
        acc = _rms(acc, gf_ref[...])
    o_ref[...] = acc


def mlp_sublayer(x2d, g, w1, w2, gf, final_norm, tm=512, ff_chunk=1024):
    t, d = x2d.shape
    tm = min(tm, t)
    return pl.pallas_call(
        functools.partial(_mlp_kernel, ff_chunk=ff_chunk, final_norm=final_norm),
        grid=(t // tm,),
        in_specs=[pl.BlockSpec((tm, d), lambda i: (i, 0)), _const_spec((1, d)),
                  _const_spec(w1.shape), _const_spec(w2.shape), _const_spec((1, d))],
        out_specs=pl.BlockSpec((tm, d), lambda i: (i, 0)),
        out_shape=jax.ShapeDtypeStruct((t, d), F32),
        compiler_params=_params("parallel"),
    )(x2d, g.reshape(1, d).astype(F32), w1, w2, gf.reshape(1, d).astype(F32))


def _xattn_kernel(x_ref, *refs, mix_transposed):
    n_mix = len(mix_transposed)
    g_ref, wq_ref, k_ref, v_ref, wo_ref, o_ref = refs[2 * n_mix:]
    x = x_ref[0]
    for a_ref, w_ref, transposed in zip(refs[:n_mix], refs[n_mix:2 * n_mix], mix_transposed):
        x = x + (_dot_tn(a_ref[...], w_ref[...]) if transposed else _dot(a_ref[0], w_ref[...]))
    hb = _rms(x, g_ref[...]).astype(BF16)
    q = (_dot(hb, wq_ref[...]) * (XA_HEAD_DIM ** -0.5)).astype(BF16)
    heads = []
    for h in range(XA_HEADS):
        sl = slice(h * XA_HEAD_DIM, (h + 1) * XA_HEAD_DIM)
        s = _dot_nt(q[:, sl], k_ref[:, sl])
        p = jnp.exp(s - jnp.max(s, axis=-1, keepdims=True))
        l = jnp.sum(p, axis=-1, keepdims=True)
        heads.append((_dot(p.astype(BF16), v_ref[:, sl]) / l).astype(BF16))
    o_ref[0] = x + _dot(jnp.concatenate(heads, axis=1), wo_ref[...])


def xattn_sublayer(x, mix_a, mix_w, g, wq, k2d, v2d, wo, tq=512):
    b, s, d = x.shape
    tq = min(tq, s)
    m = k2d.shape[0] // b
    nq = s // tq

    def mix_spec(a):
        if a.ndim == 2:
            return pl.BlockSpec((a.shape[0], tq), lambda bi, i: (0, bi * nq + i))
        return pl.BlockSpec((1, tq, a.shape[2]), lambda bi, i: (bi, i, 0))

    return pl.pallas_call(
        functools.partial(_xattn_kernel, mix_transposed=tuple(a.ndim == 2 for a in mix_a)),
        grid=(b, nq),
        in_specs=[pl.BlockSpec((1, tq, d), lambda bi, i: (bi, i, 0))]
        + [mix_spec(a) for a in mix_a]
        + [_const_spec(w.shape) for w in mix_w]
        + [_const_spec((1, d)), _const_spec(wq.shape),
           pl.BlockSpec((m, d), lambda bi, i: (bi, 0)),
           pl.BlockSpec((m, d), lambda bi, i: (bi, 0)),
           _const_spec(wo.shape)],
        out_specs=pl.BlockSpec((1, tq, d), lambda bi, i: (bi, i, 0)),
        out_shape=jax.ShapeDtypeStruct((b, s, d), F32),
        compiler_params=_params("parallel", "parallel"),
    )(x, *mix_a, *mix_w, g.reshape(1, d).astype(F32), wq, k2d, v2d, wo)


POOL_HALO = 16


def _pool_kernel(u_ref, halo_ref, w_ref, scale_ref, o_ref):
    i = pl.program_id(1)
    ts = u_ref.shape[1]
    u = u_ref[0]
    halo = jnp.where(i == 0, 0.0, halo_ref[0])
    ext = jnp.concatenate([halo, u], axis=0)
    sums = [None] * len(POOL_WINDOWS)
    cur = ext
    for gi, win in enumerate(POOL_WINDOWS):
        cur = cur[:, (POOL_GROUP_DIM if gi else 0):]
        cur = cur + pltpu.roll(cur, win // 2, axis=0)
        sums[gi] = cur[POOL_HALO:, :POOL_GROUP_DIM]
    pos1 = (i * ts + 1 + lax.broadcasted_iota(jnp.int32, (ts, 1), 0)).astype(F32)
    outs = []
    for gi, win in enumerate(POOL_WINDOWS):
        ug = u[:, gi * POOL_GROUP_DIM:(gi + 1) * POOL_GROUP_DIM]
        y = sums[gi] / jnp.minimum(pos1, float(win)) - ug
        outs.append(_dot(y.astype(BF16), w_ref[gi]))
    o_ref[0] = (jnp.concatenate(outs, axis=1) * scale_ref[...]).astype(o_ref.dtype)


def pool_mixer(z, pool_w, pool_scale, ts=512):
    b, s, _ = z.shape
    ts = min(ts, s)
    hb = ts // POOL_HALO
    return pl.pallas_call(
        _pool_kernel,
        grid=(b, s // ts),
        in_specs=[pl.BlockSpec((1, ts, POOL_WIDTH), lambda bi, i: (bi, i, 0)),
                  pl.BlockSpec((1, POOL_HALO, POOL_WIDTH),
                               lambda bi, i: (bi, jnp.maximum(i * hb - 1, 0), 0)),
                  _const_spec(pool_w.shape), _const_spec((1, POOL_WIDTH))],
        out_specs=pl.BlockSpec((1, ts, POOL_WIDTH), lambda bi, i: (bi, i, 0)),
        out_shape=jax.ShapeDtypeStruct((b, s, POOL_WIDTH), BF16),
        compiler_params=_params("parallel", "parallel"),
    )(z, z, pool_w, pool_scale.reshape(1, POOL_WIDTH).astype(F32))


DN_ROWS = DN_HEADS * DN_CHUNK
DN_CHUNKS_PER_STEP = 4
BETA_LANE = 0
ALPHA_LANE = DN_HEADS


def _stack_heads(x):
    return jnp.concatenate([x[:, h * DN_HEAD_DIM:(h + 1) * DN_HEAD_DIM] for h in range(DN_HEADS)],
                           axis=0)


def _stack_cols(x, lane0):
    return jnp.concatenate([x[:, lane0 + h:lane0 + h + 1] for h in range(DN_HEADS)], axis=0)


def _pick_head_block(wide, row_head):
    out = jnp.zeros((DN_ROWS, DN_HEAD_DIM), F32)
    for h in range(DN_HEADS):
        out = jnp.where(row_head == h, wide[:, h * DN_HEAD_DIM:(h + 1) * DN_HEAD_DIM], out)
    return out


def _deltanet_kernel(q_ref, k_ref, v_ref, gate_ref, ba_ref, cw_ref, alog_ref, dtb_ref, onorm_ref,
                     o_ref, state_ref, tail_ref):
    c = pl.program_id(1)

    @pl.when(c == 0)
    def _():
        state_ref[...] = jnp.zeros_like(state_ref)
        tail_ref[...] = jnp.zeros_like(tail_ref)

    blk_len = q_ref.shape[1]
    x3 = jnp.concatenate([q_ref[0], k_ref[0], v_ref[0]], axis=1)
    ext = jnp.concatenate([tail_ref[...], x3], axis=0)
    tail_ref[...] = x3[blk_len - 8:, :]
    cw = cw_ref[...]
    y = cw[DN_CONV - 1:DN_CONV, :] * ext
    for j in range(1, DN_CONV):
        y = y + cw[DN_CONV - 1 - j:DN_CONV - j, :] * pltpu.roll(ext, j, axis=0)
    y = y[8:, :]
    y = y * _sigmoid(y)

    def l2n(a):
        return a * lax.rsqrt(jnp.sum(a * a, axis=-1, keepdims=True) + EPS)

    ba = ba_ref[0]
    beta_all = _sigmoid(ba)
    sp_in = ba + dtb_ref[...]
    softplus = jnp.maximum(sp_in, 0.0) + jnp.log(1.0 + jnp.exp(-jnp.abs(sp_in)))
    g_all = -jnp.exp(alog_ref[...]) * softplus
    ri = lax.broadcasted_iota(jnp.int32, (blk_len, blk_len), 0)
    ci = lax.broadcasted_iota(jnp.int32, (blk_len, blk_len), 1)
    in_chunk_tril = (ri >= ci) & (_div_pow2(ri, DN_CHUNK) == _div_pow2(ci, DN_CHUNK))
    gc_blk = _dot(jnp.where(in_chunk_tril, 1.0, 0.0), g_all, precision=HIGHEST)

    rr = lax.broadcasted_iota(jnp.int32, (DN_ROWS, DN_ROWS), 0)
    cc = lax.broadcasted_iota(jnp.int32, (DN_ROWS, DN_ROWS), 1)
    same_head = _div_pow2(rr, DN_CHUNK) == _div_pow2(cc, DN_CHUNK)
    causal = same_head & (rr >= cc)
    strict = same_head & (rr > cc)
    lane_head = _div_pow2(lax.broadcasted_iota(jnp.int32, (DN_CHUNK, DN_ROWS), 1), DN_CHUNK)

    def block_diag(w):
        return jnp.concatenate([jnp.where(lane_head == h, w, 0.0) for h in range(DN_HEADS)], axis=0)

    def hi_lo(x):
        hi = x.astype(BF16).astype(F32)
        return hi, x - hi

    def times_p(x, p_wide):
        p_hi, p_lo = hi_lo(p_wide)
        d_hi, d_lo = block_diag(p_hi).astype(BF16), block_diag(p_lo).astype(BF16)
        x_hi, x_lo = hi_lo(x)
        n = x.shape[0]
        top = _dot(jnp.concatenate([x_hi, x_lo], axis=0).astype(BF16), d_hi)
        return top[:n] + top[n:] + _dot(x_hi.astype(BF16), d_lo)

    wr = lax.broadcasted_iota(jnp.int32, (DN_CHUNK, DN_ROWS), 0)
    wc = lax.broadcasted_iota(jnp.int32, (DN_CHUNK, DN_ROWS), 1)
    eye_wide = jnp.where(wr == (wc & (DN_CHUNK - 1)), 1.0, 0.0)
    n_sq = int(math.log2(DN_CHUNK)) - 1

    def chunk_prep(ci):
        rows = slice(ci * DN_CHUNK, (ci + 1) * DN_CHUNK)
        yc = y[rows, :]
        q_st = l2n(_stack_heads(yc[:, :DN_WIDTH])) * (DN_HEAD_DIM ** -0.5)
        k_st = l2n(_stack_heads(yc[:, DN_WIDTH:2 * DN_WIDTH]))
        v_st = _stack_heads(yc[:, 2 * DN_WIDTH:])
        gc_all = gc_blk[rows, :]
        beta_st = _stack_cols(beta_all[rows, :], BETA_LANE)
        gc_st = _stack_cols(gc_all, ALPHA_LANE)
        g_last = [gc_all[DN_CHUNK - 1:DN_CHUNK, ALPHA_LANE + h:ALPHA_LANE + h + 1]
                  for h in range(DN_HEADS)]
        glast_st = jnp.concatenate([jnp.broadcast_to(g, (DN_CHUNK, 1)) for g in g_last], axis=0)
        gcb = jnp.broadcast_to(gc_st, (DN_ROWS, DN_ROWS))
        decay = jnp.where(causal, jnp.exp(jnp.where(causal, gcb - gcb.T, 0.0)), 0.0)
        kb_st = k_st * beta_st
        k_bf = k_st.astype(BF16)
        a_low = jnp.where(strict, _dot_nt(kb_st.astype(BF16), k_bf) * decay, 0.0)
        p = -(a_low[0:DN_CHUNK] + a_low[DN_CHUNK:2 * DN_CHUNK]
              + a_low[2 * DN_CHUNK:3 * DN_CHUNK] + a_low[3 * DN_CHUNK:])
        t_wide = eye_wide + p
        yield None
        for j in range(n_sq):
            if j == 0:
                p = times_p(p, p)
            else:
                both = times_p(jnp.concatenate([p, t_wide], axis=0), p)
                p, t_wide = both[:DN_CHUNK], t_wide + both[DN_CHUNK:]
            yield None
        t_wide = t_wide + times_p(t_wide, p)
        t_bf = block_diag(t_wide).astype(BF16)
        egc = jnp.exp(gc_st)
        wu = _dot(t_bf, jnp.concatenate([(kb_st * egc).astype(BF16),
                                         (v_st * beta_st).astype(BF16)], axis=1))
        attn = jnp.where(causal, _dot_nt(q_st.astype(BF16), k_bf) * decay, 0.0)
        yield dict(
            w=wu[:, :DN_HEAD_DIM].astype(BF16), u=wu[:, DN_HEAD_DIM:], attn=attn.astype(BF16),
            q_dec=(q_st * egc).astype(BF16), k_dec=(k_st * jnp.exp(glast_st - gc_st)).astype(BF16),
            state_scale=jnp.concatenate(
                [jnp.broadcast_to(jnp.exp(g), (1, DN_HEAD_DIM)) for g in g_last], axis=1))

    preps = [chunk_prep(ci) for ci in range(blk_len // DN_CHUNK)]
    for _ in range(n_sq + 1):
        for prep in preps:
            next(prep)
    preps = [next(prep) for prep in preps]

    state = state_ref[...]
    row_head = _div_pow2(lax.broadcasted_iota(jnp.int32, (DN_ROWS, 1), 0), DN_CHUNK)
    gate = gate_ref[0]
    out_rows = []
    for ci, pr in enumerate(preps):
        s_bf = state.astype(BF16)
        v_new = pr["u"] - _pick_head_block(_dot(pr["w"], s_bf), row_head)
        v_new_bf = v_new.astype(BF16)
        o_st = _pick_head_block(_dot(pr["q_dec"], s_bf), row_head) + _dot(pr["attn"], v_new_bf)
        zero = jnp.zeros_like(v_new_bf)
        v_wide = jnp.concatenate([jnp.where(row_head == h, v_new_bf, zero) for h in range(DN_HEADS)],
                                 axis=1)
        state = state * pr["state_scale"] + _dot_tn(pr["k_dec"], v_wide)
        outs = []
        for h in range(DN_HEADS):
            o_h = _rms(o_st[h * DN_CHUNK:(h + 1) * DN_CHUNK, :], onorm_ref[...])
            g_h = gate[ci * DN_CHUNK:(ci + 1) * DN_CHUNK, h * DN_HEAD_DIM:(h + 1) * DN_HEAD_DIM]
            outs.append(o_h * (g_h * _sigmoid(g_h)))
        out_rows.append(jnp.concatenate(outs, axis=1))
    state_ref[...] = state
    o_ref[0] = jnp.concatenate(out_rows, axis=0).astype(o_ref.dtype)


def gated_deltanet(z, ba_block, conv_w, a_log, dt_bias, o_norm):
    b, s, _ = z.shape
    lane_row = lambda vals, lane0: jnp.zeros((1, LANES), F32).at[0, lane0:lane0 + DN_HEADS].set(
        vals.astype(F32))
    blk_len = min(DN_CHUNKS_PER_STEP * DN_CHUNK, s)
    col = lambda j: pl.BlockSpec((1, blk_len, DN_WIDTH), lambda bi, ci: (bi, ci, j))
    return pl.pallas_call(
        _deltanet_kernel,
        grid=(b, s // blk_len),
        in_specs=[col(1), col(2), col(3), col(4),
                  pl.BlockSpec((1, blk_len, LANES), lambda bi, ci: (bi, ci, ba_block)),
                  _const_spec(conv_w.shape), _const_spec((1, LANES)), _const_spec((1, LANES)),
                  _const_spec((1, DN_HEAD_DIM))],
        out_specs=pl.BlockSpec((1, blk_len, DN_WIDTH), lambda bi, ci: (bi, ci, 0)),
        out_shape=jax.ShapeDtypeStruct((b, s, DN_WIDTH), BF16),
        scratch_shapes=[pltpu.VMEM((DN_HEAD_DIM, DN_WIDTH), F32),
                        pltpu.VMEM((8, 3 * DN_WIDTH), F32)],
        compiler_params=_params("parallel", "arbitrary"),
    )(z, z, z, z, z, conv_w.astype(F32), lane_row(a_log, ALPHA_LANE), lane_row(dt_bias, ALPHA_LANE),
      o_norm.reshape(1, DN_HEAD_DIM).astype(F32))


def _compress_kernel(rk_ref, rv_ref, w1k_ref, pek_ref, w2k_ref, w1v_ref, pev_ref, w2v_ref,
                     ck_ref, cv_ref):
    n = rk_ref.shape[2]
    row = lax.broadcasted_iota(jnp.int32, (n, 1), 0)

    def one(r_ref, w1_ref, pe_ref, w2_ref, o_ref):
        w1 = w1_ref[...]
        pb = _dot(pe_ref[...], w1)
        bias = pb[0:1, :CMP_HIDDEN] + pb[1:2, CMP_HIDDEN:]
        for g in range(NSA_GROUPS):
            y = _dot(r_ref[0, g], w1)
            h = y[:, :CMP_HIDDEN] + pltpu.roll(y[:, CMP_HIDDEN:], n - 1, axis=0) + bias
            a = (h * _sigmoid(h)).astype(BF16)
            o_ref[0, g] = jnp.where(row < n - 1, _dot(a, w2_ref[...]), 0.0).astype(o_ref.dtype)

    one(rk_ref, w1k_ref, pek_ref, w2k_ref, ck_ref)
    one(rv_ref, w1v_ref, pev_ref, w2v_ref, cv_ref)


def compress_kv(rk, rv, w1k, pek, w2k, w1v, pev, w2v):
    b, g, n, w = rk.shape
    blk = pl.BlockSpec((1, g, n, w), lambda bi: (bi, 0, 0, 0))
    oblk = pl.BlockSpec((1, g, n, NSA_HEAD_DIM), lambda bi: (bi, 0, 0, 0))
    return pl.pallas_call(
        _compress_kernel,
        grid=(b,),
        in_specs=[blk, blk, _const_spec(w1k.shape), _const_spec(pek.shape), _const_spec(w2k.shape),
                  _const_spec(w1v.shape), _const_spec(pev.shape), _const_spec(w2v.shape)],
        out_specs=[oblk, oblk],
        out_shape=[jax.ShapeDtypeStruct((b, g, n, NSA_HEAD_DIM), BF16)] * 2,
        compiler_params=_params("parallel"),
    )(rk, rv, w1k, pek, w2k, w1v, pev, w2v)


def _aug_keys(k, pos):
    n = k.shape[0]
    lane = lax.broadcasted_iota(jnp.int32, (n, NSA_HEAD_DIM), 1)
    hi = (_div_pow2(pos, POS_SPLIT) * POS_SPLIT).astype(F32)
    lo = (pos & (POS_SPLIT - 1)).astype(F32)
    cols = jnp.where(lane < SLOPE_PIECES, hi, jnp.where(lane < 2 * SLOPE_PIECES, lo, 0.0))
    return jnp.concatenate([k[:, :NSA_HEAD_DIM], cols.astype(BF16)], axis=1)


def _nsa_kernel(q_ref, gate_ref, slope_ref, ck_ref, cv_ref, ks_ref, vs_ref, kw_ref, vw_ref,
                ovl_ref, o_ref, kc_aug, ks_aug, kw_aug, vs_aug, vw_aug, s_buf, rank_ref, chunk_list,
                *, top_n):
    i = pl.program_id(2)
    tq = q_ref.shape[1]
    rows = NSA_REP * tq
    n_cmp = ck_ref.shape[2]
    assert q_ref.shape[0] == NSA_REP * NSA_HEAD_DIM
    n_slc = ovl_ref.shape[0]
    s_len = ks_ref.shape[1]
    q0 = i * tq

    @pl.when(i == 0)
    def _():
        cpos = lax.broadcasted_iota(jnp.int32, (n_cmp, 1), 0) * CMP_STRIDE + (CMP_LEN - 1)
        kc_aug[...] = _aug_keys(ck_ref[0, 0], cpos)

        def fill(c, carry):
            r0 = pl.multiple_of(c * SLC_CHUNK, SLC_CHUNK)
            pos = r0 + lax.broadcasted_iota(jnp.int32, (SLC_CHUNK, 1), 0)
            blk_lane = lax.broadcasted_iota(jnp.int32, (SLC_CHUNK, LANES), 1)
            onehot = jnp.where(blk_lane == _div_pow2(pos, SLC_LEN), 1.0, 0.0).astype(BF16)
            ks_aug[pl.ds(r0, SLC_CHUNK), :] = jnp.concatenate(
                [_aug_keys(ks_ref[0, pl.ds(r0, SLC_CHUNK), :], pos), onehot], axis=1)
            kw_aug[pl.ds(r0, SLC_CHUNK), :] = _aug_keys(kw_ref[0, pl.ds(r0, SLC_CHUNK), :], pos)
            return carry

        lax.fori_loop(0, s_len // SLC_CHUNK, fill, 0)
        ones_rows = jnp.where(lax.broadcasted_iota(jnp.int32, (V_PAD_ROWS, s_len), 0) == 0, 1.0, 0.0)
        for v_ref, v_aug in ((vs_ref, vs_aug), (vw_ref, vw_aug)):
            v_aug[0:NSA_HEAD_DIM, :] = v_ref[...]
            v_aug[NSA_HEAD_DIM:, :] = ones_rows.astype(BF16)

    q_t = q_ref[...].astype(F32) * (NSA_HEAD_DIM ** -0.5 * LOG2E)
    q_t = jnp.concatenate([q_t[r * NSA_HEAD_DIM:(r + 1) * NSA_HEAD_DIM, :] for r in range(NSA_REP)],
                          axis=1).astype(BF16)
    q_aug = jnp.concatenate([q_t, slope_ref[0].astype(BF16)], axis=0)
    t_lane = q0 + lax.broadcasted_iota(jnp.int32, (1, tq), 1)

    def all_heads(x):
        return jnp.concatenate([x] * NSA_REP, axis=1)

    def col_max(x):
        return jnp.max(x, axis=0, keepdims=True)

    def normalized(acc):
        return acc[:NSA_HEAD_DIM] / acc[NSA_HEAD_DIM:NSA_HEAD_DIM + 1]

    span = min(WINDOW + tq, s_len)
    w0 = pl.multiple_of(jnp.maximum(q0 + tq - span, 0), tq)
    dist_w = t_lane - (w0 + lax.broadcasted_iota(jnp.int32, (span, 1), 0))
    band = jnp.where(dist_w >= 0, jnp.where(dist_w < WINDOW, 0.0, NEG_INF), NEG_INF)
    sw = _dot(kw_aug[pl.ds(w0, span), :], q_aug) + all_heads(band)

    cend = lax.broadcasted_iota(jnp.int32, (n_cmp, 1), 0) * CMP_STRIDE + (CMP_LEN - 1)
    sc = _dot(kc_aug[...], q_aug) + all_heads(jnp.where(cend <= t_lane, 0.0, NEG_INF))

    tile_k0 = pl.multiple_of(q0, tq)
    key_in_tile = lax.broadcasted_iota(jnp.int32, (tq, 1), 0)
    qry_in_tile = lax.broadcasted_iota(jnp.int32, (1, tq), 1)
    sd = (_dot(ks_aug[pl.ds(tile_k0, tq), 0:LANES], q_aug)
          + all_heads(jnp.where(key_in_tile <= qry_in_tile, 0.0, NEG_INF)))

    e = jnp.exp2(sc - col_max(sc))
    any_visible = all_heads(jnp.where(t_lane >= CMP_LEN - 1, 1.0, 0.0))
    p_cmp = e * (any_visible / jnp.sum(e, axis=0, keepdims=True))
    o_cmp = _dot(cv_ref[0, 0], p_cmp.astype(BF16))

    p_sum = p_cmp[:, 0:tq]
    for r in range(1, NSA_REP):
        p_sum = p_sum + p_cmp[:, r * tq:(r + 1) * tq]
    ovl = ovl_ref[...].astype(BF16)
    imp = jnp.zeros((n_slc, tq), F32)
    rest = p_sum
    for _ in range(3):
        piece = rest.astype(BF16)
        imp = imp + _dot(ovl, piece)
        rest = rest - piece.astype(F32)
    blk = lax.broadcasted_iota(jnp.int32, (n_slc, tq), 0)
    tl = q0 + lax.broadcasted_iota(jnp.int32, (n_slc, tq), 1)
    cur = _div_pow2(tl, SLC_LEN)
    forced = (blk == 0) | (blk == cur) | (blk == cur - 1)
    val = jnp.where(forced, FORCE_SCORE, jnp.where(blk * SLC_LEN <= tl, imp, -1.0))
    n_grp = n_slc // RANK_GROUP
    val_grp = [val[RANK_GROUP * g:RANK_GROUP * (g + 1)] for g in range(n_grp)]
    row_in_grp = lax.broadcasted_iota(jnp.int32, (RANK_GROUP, tq), 0)
    rank_ref[...] = jnp.zeros_like(rank_ref)
    for mg in range(n_grp):
        @pl.when(mg * RANK_GROUP * SLC_LEN < q0 + tq)
        def _():
            parts = [rank_ref[RANK_GROUP * g:RANK_GROUP * (g + 1), :] for g in range(n_grp)]
            for m in range(RANK_GROUP * mg, RANK_GROUP * (mg + 1)):
                vm = val[m:m + 1, :]
                for g in range(n_grp):
                    if g < mg:
                        beats = jnp.where(vm > val_grp[g], 1.0, 0.0)
                    elif g > mg:
                        beats = jnp.where(vm >= val_grp[g], 1.0, 0.0)
                    else:
                        beats = jnp.where(row_in_grp > m - RANK_GROUP * mg,
                                          jnp.where(vm >= val_grp[g], 1.0, 0.0),
                                          jnp.where(vm > val_grp[g], 1.0, 0.0))
                    parts[g] = parts[g] + beats
            for g in range(n_grp):
                rank_ref[RANK_GROUP * g:RANK_GROUP * (g + 1), :] = parts[g]
    rank = rank_ref[...]

    before_tile = blk * SLC_LEN < q0
    sel_mask = jnp.where(before_tile, jnp.where(rank < top_n, 0.0, NEG_INF), NEG_INF)
    sel_mask = jnp.concatenate([sel_mask, jnp.zeros((LANES - n_slc, tq), F32)], axis=0)
    q_sel = jnp.concatenate([q_aug, all_heads(sel_mask.astype(BF16))], axis=0)

    n_chunks = s_len // SLC_CHUNK
    blk_any = jnp.max(sel_mask[:n_slc], axis=1, keepdims=True)
    blocks_per_chunk = SLC_CHUNK // SLC_LEN
    needed = jnp.int32(0)
    for j in range(n_chunks):
        hit = jnp.max(blk_any[j * blocks_per_chunk:(j + 1) * blocks_per_chunk]) > 0.5 * NEG_INF
        needed = needed | (hit.astype(jnp.int32) << j)

    pw = jnp.exp2(sw - col_max(sw)).astype(BF16)
    m0 = col_max(sd)
    pd = jnp.exp2(sd - m0).astype(BF16)
    o_win = normalized(_dot(vw_aug[:, pl.ds(w0, span)], pw))
    acc0 = _dot(vs_aug[:, pl.ds(tile_k0, tq)], pd)

    n_needed = jnp.int32(0)
    for j in range(n_chunks):
        chunk_list[n_needed] = jnp.int32(j)
        n_needed = n_needed + ((needed >> j) & 1)

    n_kb = SLC_CHUNK // SLC_KEY_BLOCK

    def chunk_logits(j, kb):
        kk = pl.multiple_of(j * SLC_CHUNK, SLC_CHUNK) + kb * SLC_KEY_BLOCK
        return _dot(ks_aug[pl.ds(kk, SLC_KEY_BLOCK), :], q_sel)

    m1 = m0
    first = chunk_list[0]
    for kb in range(n_kb):
        s_new = chunk_logits(first, kb)
        s_buf[kb * SLC_KEY_BLOCK:(kb + 1) * SLC_KEY_BLOCK, :] = s_new
        m1 = jnp.maximum(m1, col_max(s_new))

    def slc_step(c, carry):
        m_prev, m_cur, acc = carry
        acc = jnp.exp2(m_prev - m_cur) * acc
        k0 = pl.multiple_of(chunk_list[c] * SLC_CHUNK, SLC_CHUNK)
        j_next = chunk_list[jnp.minimum(c + 1, n_needed - 1)]
        m_next = m_cur
        for kb in range(n_kb):
            blk_rows = slice(kb * SLC_KEY_BLOCK, (kb + 1) * SLC_KEY_BLOCK)
            s_new = chunk_logits(j_next, kb)
            p = jnp.exp2(s_buf[blk_rows, :] - m_cur).astype(BF16)
            acc = acc + _dot(vs_aug[:, pl.ds(k0 + kb * SLC_KEY_BLOCK, SLC_KEY_BLOCK)], p)
            s_buf[blk_rows, :] = s_new
            m_next = jnp.maximum(m_next, col_max(s_new))
        return m_cur, m_next, acc

    _, _, acc = lax.fori_loop(0, n_needed, slc_step, (m0, m1, acc0))
    o_slc = normalized(acc)

    gates = _sigmoid(gate_ref[0]).T

    def gate_rows(br):
        return jnp.concatenate([gates[3 * r + br:3 * r + br + 1, :] for r in range(NSA_REP)], axis=1)

    o = gate_rows(0) * o_cmp + gate_rows(1) * o_slc + gate_rows(2) * o_win
    o_ref[...] = jnp.concatenate([o[:, r * tq:(r + 1) * tq] for r in range(NSA_REP)],
                                 axis=0).astype(o_ref.dtype)


def nsa_attention(zqt, zg, ck, cv, zk, zvt, tq=128):
    b, s, _ = zg.shape
    n_cmp = ck.shape[2]
    n_slc = s // SLC_LEN
    top_n = min(SLC_TOP, n_slc)
    assert tq == 2 * SLC_LEN and top_n >= 3 and n_slc <= LANES and s % SLC_CHUNK == 0
    c_lo = jnp.arange(n_cmp) * CMP_STRIDE
    s_lo = jnp.arange(n_slc) * SLC_LEN
    ovl = (jnp.clip(jnp.minimum(c_lo[None, :] + CMP_LEN, s_lo[:, None] + SLC_LEN)
                    - jnp.maximum(c_lo[None, :], s_lo[:, None]), 0, None).astype(F32) / CMP_LEN)
    ovl = ovl * (jnp.arange(n_cmp) < n_cmp - 1)[None, :]
    hd = jnp.arange(1, NSA_HEADS + 1, dtype=F32)
    rest = jnp.exp2(-8.0 * hd / NSA_HEADS) * LOG2E
    pieces = []
    for _ in range(SLOPE_PIECES):
        piece = rest.astype(BF16).astype(F32)
        pieces.append(piece)
        rest = rest - piece
    pieces = jnp.stack(pieces * 2, axis=0).reshape(2 * SLOPE_PIECES, NSA_GROUPS, NSA_REP)
    slope_tab = jnp.zeros((NSA_GROUPS, NSA_HEAD_DIM, NSA_REP, tq), F32)
    slope_tab = slope_tab.at[:, :2 * SLOPE_PIECES].set(
        jnp.transpose(pieces, (1, 0, 2))[..., None]).reshape(NSA_GROUPS, NSA_HEAD_DIM, NSA_REP * tq)
    def k_spec(branch):
        return pl.BlockSpec((1, s, LANES), lambda bi, g, i: (bi, 0, branch * NSA_GROUPS + g))

    def v_spec(branch):
        return pl.BlockSpec((NSA_HEAD_DIM, s), lambda bi, g, i: (branch * NSA_GROUPS + g, bi))

    ck_spec = pl.BlockSpec((1, 1, n_cmp, NSA_HEAD_DIM), lambda bi, g, i: (bi, g, 0, 0))
    cv_spec = pl.BlockSpec((1, 1, NSA_HEAD_DIM, n_cmp), lambda bi, g, i: (bi, g, 0, 0))
    gw = NSA_REP * NSA_HEAD_DIM
    nq = s // tq
    return pl.pallas_call(
        functools.partial(_nsa_kernel, top_n=top_n),
        grid=(b, NSA_GROUPS, s // tq),
        in_specs=[pl.BlockSpec((gw, tq), lambda bi, g, i: (g, bi * nq + i)),
                  pl.BlockSpec((1, tq, LANES), lambda bi, g, i: (bi, i, g)),
                  pl.BlockSpec((1, NSA_HEAD_DIM, NSA_REP * tq), lambda bi, g, i: (g, 0, 0)),
                  ck_spec, cv_spec, k_spec(0), v_spec(0), k_spec(1), v_spec(1),
                  _const_spec(ovl.shape)],
        out_specs=pl.BlockSpec((gw, tq), lambda bi, g, i: (g, bi * nq + i)),
        out_shape=jax.ShapeDtypeStruct((NSA_GROUPS * gw, b * s), BF16),
        scratch_shapes=[pltpu.VMEM((n_cmp, 2 * NSA_HEAD_DIM), BF16),
                        pltpu.VMEM((s, 2 * NSA_HEAD_DIM + LANES), BF16),
                        pltpu.VMEM((s, 2 * NSA_HEAD_DIM), BF16),
                        pltpu.VMEM((NSA_HEAD_DIM + V_PAD_ROWS, s), BF16),
                        pltpu.VMEM((NSA_HEAD_DIM + V_PAD_ROWS, s), BF16),
                        pltpu.VMEM((SLC_CHUNK, NSA_REP * tq), F32),
                        pltpu.VMEM((n_slc, tq), F32),
                        pltpu.SMEM((s // SLC_CHUNK,), jnp.int32)],
        compiler_params=_params("parallel", "parallel", "arbitrary"),
    )(zqt, zg, slope_tab, ck, cv, zk, zvt, zk, zvt, ovl)


def _pad_cols(w, n):
    return jnp.pad(w, ((0, 0), (0, n - w.shape[1])))


def pool_delta_layer(x, ln, w_in, pool_w, pool_scale, conv_w, a_log, dt_bias, o_norm, w_out):
    b, s, d = x.shape
    main = POOL_WIDTH + 4 * DN_WIDTH
    w_all = jnp.concatenate([w_in[:, :main], _pad_cols(w_in[:, main:], LANES)], axis=1).astype(BF16)
    (z,) = norm_matmul(x.reshape(b * s, d), ln, [w_all], [F32])
    z = z.reshape(b, s, main + LANES)
    y_pool = pool_mixer(z, pool_w.astype(BF16), pool_scale)
    y_dn = gated_deltanet(z, main // LANES, conv_w, a_log, dt_bias, o_norm)
    w_out = w_out.astype(BF16)
    return [y_pool, y_dn], [w_out[:POOL_WIDTH], w_out[POOL_WIDTH:]]


def _cmp_weights(pe, w1, w2):
    half = (CMP_LEN // 2) * NSA_HEAD_DIM
    w1_pair = jnp.concatenate([w1[:half], w1[half:]], axis=1).astype(BF16)
    pe_rows = jnp.zeros((8, half), F32).at[0:2].set(pe.reshape(2, half)).astype(BF16)
    return w1_pair, pe_rows, w2.astype(BF16)


def nsa_layer(x, ln, w_in, pe_k, w1_k, w2_k, pe_v, w1_v, w2_v, w_out):
    b, s, d = x.shape
    g_, r_, hd = NSA_GROUPS, NSA_REP, NSA_HEAD_DIM
    kvw = g_ * hd
    main = d + 6 * kvw
    wg = w_in[:, main:].reshape(d, g_, r_ * 3)
    wg = jnp.pad(wg, ((0, 0), (0, 0), (0, LANES - r_ * 3))).reshape(d, g_ * LANES)
    cols = lambda j: w_in[:, d + j * kvw:d + (j + 1) * kvw]

    def lane_block_per_group(w):
        return jnp.pad(w.reshape(d, g_, hd), ((0, 0), (0, 0), (0, LANES - hd))).reshape(d, g_ * LANES)

    w_k = jnp.concatenate([lane_block_per_group(cols(2)), lane_block_per_group(cols(4))], axis=1)
    w_vt = jnp.concatenate([cols(3), cols(5)], axis=1).T
    z, zg, zk, zvt, zqt = norm_matmul(
        x.reshape(b * s, d), ln,
        [w_in[:, d:d + 2 * kvw].astype(BF16), wg.astype(BF16), w_k.astype(BF16)], [BF16, F32, BF16],
        wts=[w_vt.astype(BF16), w_in[:, :d].T.astype(BF16)])
    z = z.reshape(b, s, 2 * kvw)
    zg = zg.reshape(b, s, g_ * LANES)
    zk = zk.reshape(b, s, 2 * g_ * LANES)

    def group_major(j):
        return jnp.transpose(z[..., j * kvw:(j + 1) * kvw].reshape(b, s, g_, hd), (0, 2, 1, 3))

    n_str = s // CMP_STRIDE
    rk = group_major(0).reshape(b, g_, n_str, CMP_STRIDE * hd)
    rv = group_major(1).reshape(b, g_, n_str, CMP_STRIDE * hd)
    ck, cv = compress_kv(rk, rv, *_cmp_weights(pe_k, w1_k, w2_k), *_cmp_weights(pe_v, w1_v, w2_v))
    o_t = nsa_attention(zqt, zg, ck, jnp.swapaxes(cv, 2, 3), zk, zvt)
    return [o_t], [w_out.astype(BF16)]


def kernel(x, mem, a_ln, a_w_in, a_pool_w, a_pool_scale, a_conv_w, a_a_log, a_dt_bias, a_o_norm, a_w_out, c_ln, c_w_in, c_pe_k, c_w1_k, c_w2_k, c_pe_v, c_w1_v, c_w2_v, c_w_out, xa_ln, xa_mem_ln, xa_wq, xa_wk, xa_wv, xa_wo, ff_ln, ff_w1, ff_w2, final_ln):
    b, s, d = x.shape
    depth = xa_ln.shape[0]
    mem2d = mem.reshape(b * mem.shape[1], d)
    for l in range(depth):
        i = l // 2
        if l % 2 == 0:
            mix_a, mix_w = pool_delta_layer(x, a_ln[i], a_w_in[i], a_pool_w[i], a_pool_scale[i],
                                            a_conv_w[i], a_a_log[i], a_dt_bias[i], a_o_norm[i],
                                            a_w_out[i])
        else:
            mix_a, mix_w = nsa_layer(x, c_ln[i], c_w_in[i], c_pe_k[i], c_w1_k[i], c_w2_k[i],
                                     c_pe_v[i], c_w1_v[i], c_w2_v[i], c_w_out[i])
        mk, mv = norm_matmul(mem2d, xa_mem_ln[l], [xa_wk[l].astype(BF16), xa_wv[l].astype(BF16)],
                             [BF16, BF16])
        x = xattn_sublayer(x, mix_a, mix_w, xa_ln[l], xa_wq[l].astype(BF16), mk, mv,
                           xa_wo[l].astype(BF16))
        x = mlp_sublayer(x.reshape(b * s, d), ff_ln[l], ff_w1[l].astype(BF16), ff_w2[l].astype(BF16),
                         final_ln, final_norm=(l == depth - 1)).reshape(b, s, d)
    return x
```

```python
import functools
import math

import jax
import jax.numpy as jnp
from jax import lax
from jax.experimental import pallas as pl
from jax.experimental.pallas import tpu as pltpu

F32 = jnp.float32
BF16 = jnp.bfloat16
HIGHEST = lax.Precision.HIGHEST

LANES = 128
VMEM_LIMIT = 56 * 1024 * 1024

EPS = 1e-6
NEG_INF = -1e30
FORCE_SCORE = 1e4

POOL_WINDOWS = (2, 4, 8, 16)
POOL_GROUP_DIM = 128
POOL_WIDTH = 512
DN_HEADS = 4
DN_HEAD_DIM = 128
DN_WIDTH = 512
DN_CONV = 4
DN_CHUNK = 64

NSA_HEAD_DIM = 64
NSA_GROUPS = 4
NSA_REP = 4
NSA_HEADS = 16
CMP_LEN = 32
CMP_STRIDE = 16
CMP_HIDDEN = 128
SLC_LEN = 64
SLC_TOP = 16
WINDOW = 512
SLC_CHUNK = 512
SLC_KEY_BLOCK = 256
RANK_GROUP = 8
V_PAD_ROWS = 16
LOG2E = 1.4426950408889634
SLOPE_PIECES = 3
POS_SPLIT = 64

XA_HEADS = 4
XA_HEAD_DIM = 256


def _params(*sem):
    return pltpu.CompilerParams(dimension_semantics=sem, vmem_limit_bytes=VMEM_LIMIT)


def _const_spec(shape):
    nd = len(shape)
    return pl.BlockSpec(shape, lambda *_: (0,) * nd)


def _rms(x, g):
    return x * lax.rsqrt(jnp.mean(x * x, axis=-1, keepdims=True) + EPS) * g


def _sigmoid(x):
    return 1.0 / (1.0 + jnp.exp(-x))


def _div_pow2(x, n):
    return lax.shift_right_logical(x, jnp.int32(int(math.log2(n))))


def _dot(a, b, precision=None):
    return jnp.dot(a, b, preferred_element_type=F32, precision=precision)


def _dot_nt(a, b, precision=None):
    return lax.dot_general(a, b, (((1,), (1,)), ((), ())), preferred_element_type=F32,
                           precision=precision)


def _dot_tn(a, b):
    return lax.dot_general(a, b, (((0,), (0,)), ((), ())), preferred_element_type=F32)


def _norm_matmul_kernel(x_ref, g_ref, *refs, n_plain):
    n = len(refs) // 2
    hb = _rms(x_ref[...], g_ref[...]).astype(BF16)
    for j, (w_ref, o_ref) in enumerate(zip(refs[:n], refs[n:])):
        if j < n_plain:
            o_ref[...] = _dot(hb, w_ref[...]).astype(o_ref.dtype)
        else:
            o_ref[...] = _dot_nt(w_ref[...], hb).astype(o_ref.dtype)


def norm_matmul(x2d, g, ws, out_dtypes, wts=(), tm=512):
    t, d = x2d.shape
    tm = min(tm, t)
    return pl.pallas_call(
        functools.partial(_norm_matmul_kernel, n_plain=len(ws)),
        grid=(t // tm,),
        in_specs=[pl.BlockSpec((tm, d), lambda i: (i, 0)), _const_spec((1, d))]
        + [_const_spec(w.shape) for w in (*ws, *wts)],
        out_specs=[pl.BlockSpec((tm, w.shape[1]), lambda i: (i, 0)) for w in ws]
        + [pl.BlockSpec((w.shape[0], tm), lambda i: (0, i)) for w in wts],
        out_shape=[jax.ShapeDtypeStruct((t, w.shape[1]), dt) for w, dt in zip(ws, out_dtypes)]
        + [jax.ShapeDtypeStruct((w.shape[0], t), BF16) for w in wts],
        compiler_params=_params("parallel"),
    )(x2d, g.reshape(1, d).astype(F32), *ws, *wts)


def _mlp_kernel(x_ref, g_ref, w1_ref, w2_ref, gf_ref, o_ref, *, ff_chunk, final_norm):
    x = x_ref[...]
    hb = _rms(x, g_ref[...]).astype(BF16)
    acc = x
    for c in range(w1_ref.shape[1] // ff_chunk):
        a = jnp.maximum(_dot(hb, w1_ref[:, c * ff_chunk:(c + 1) * ff_chunk]), 0.0)
        acc = acc + _dot((a * a).astype(BF16), w2_ref[c * ff_chunk:(c + 1) * ff_chunk, :])
    if final_norm:
        acc = _rms(acc, gf_ref[...])
    o_ref[...] = acc


def mlp_sublayer(x2d, g, w1, w2, gf, final_norm, tm=512, ff_chunk=1024):
    t, d = x2d.shape
    tm = min(tm, t)
    return pl.pallas_call(
        functools.partial(_mlp_kernel, ff_chunk=ff_chunk, final_norm=final_norm),
        grid=(t // tm,),
        in_specs=[pl.BlockSpec((tm, d), lambda i: (i, 0)), _const_spec((1, d)),
                  _const_spec(w1.shape), _const_spec(w2.shape), _const_spec((1, d))],
        out_specs=pl.BlockSpec((tm, d), lambda i: (i, 0)),
        out_shape=jax.ShapeDtypeStruct((t, d), F32),
        compiler_params=_params("parallel"),
    )(x2d, g.reshape(1, d).astype(F32), w1, w2, gf.reshape(1, d).astype(F32))


def _xattn_kernel(x_ref, *refs, mix_transposed):
    n_mix = len(mix_transposed)
    g_ref, wq_ref, k_ref, v_ref, wo_ref, o_ref = refs[2 * n_mix:]
    x = x_ref[0]
    for a_ref, w_ref, transposed in zip(refs[:n_mix], refs[n_mix:2 * n_mix], mix_transposed):
        x = x + (_dot_tn(a_ref[...], w_ref[...]) if transposed else _dot(a_ref[0], w_ref[...]))
    hb = _rms(x, g_ref[...]).astype(BF16)
    q = (_dot(hb, wq_ref[...]) * (XA_HEAD_DIM ** -0.5)).astype(BF16)
    heads = []
    for h in range(XA_HEADS):
        sl = slice(h * XA_HEAD_DIM, (h + 1) * XA_HEAD_DIM)
        s = _dot_nt(q[:, sl], k_ref[:, sl])
        p = jnp.exp(s - jnp.max(s, axis=-1, keepdims=True))
        l = jnp.sum(p, axis=-1, keepdims=True)
        heads.append((_dot(p.astype(BF16), v_ref[:, sl]) / l).astype(BF16))
    o_ref[0] = x + _dot(jnp.concatenate(heads, axis=1), wo_ref[...])


def xattn_sublayer(x, mix_a, mix_w, g, wq, k2d, v2d, wo, tq=512):
    b, s, d = x.shape
    tq = min(tq, s)
    m = k2d.shape[0] // b
    nq = s // tq

    def mix_spec(a):
        if a.ndim == 2:
            return pl.BlockSpec((a.shape[0], tq), lambda bi, i: (0, bi * nq + i))
        return pl.BlockSpec((1, tq, a.shape[2]), lambda bi, i: (bi, i, 0))

    return pl.pallas_call(
        functools.partial(_xattn_kernel, mix_transposed=tuple(a.ndim == 2 for a in mix_a)),
        grid=(b, nq),
        in_specs=[pl.BlockSpec((1, tq, d), lambda bi, i: (bi, i, 0))]
        + [mix_spec(a) for a in mix_a]
        + [_const_spec(w.shape) for w in mix_w]
        + [_const_spec((1, d)), _const_spec(wq.shape),
           pl.BlockSpec((m, d), lambda bi, i: (bi, 0)),
           pl.BlockSpec((m, d), lambda bi, i: (bi, 0)),
           _const_spec(wo.shape)],
        out_specs=pl.BlockSpec((1, tq, d), lambda bi, i: (bi, i, 0)),
        out_shape=jax.ShapeDtypeStruct((b, s, d), F32),
        compiler_params=_params("parallel", "parallel"),
    )(x, *mix_a, *mix_w, g.reshape(1, d).astype(F32), wq, k2d, v2d, wo)


POOL_HALO = 16


def _pool_kernel(u_ref, halo_ref, w_ref, scale_ref, o_ref):
    i = pl.program_id(1)
    ts = u_ref.shape[1]
    u = u_ref[0]
    halo = jnp.where(i == 0, 0.0, halo_ref[0])
    ext = jnp.concatenate([halo, u], axis=0)
    sums = [None] * len(POOL_WINDOWS)
    cur = ext
    for gi, win in enumerate(POOL_WINDOWS):
        cur = cur[:, (POOL_GROUP_DIM if gi else 0):]
        cur = cur + pltpu.roll(cur, win // 2, axis=0)
        sums[gi] = cur[POOL_HALO:, :POOL_GROUP_DIM]
    pos1 = (i * ts + 1 + lax.broadcasted_iota(jnp.int32, (ts, 1), 0)).astype(F32)
    outs = []
    for gi, win in enumerate(POOL_WINDOWS):
        ug = u[:, gi * POOL_GROUP_DIM:(gi + 1) * POOL_GROUP_DIM]
        y = sums[gi] / jnp.minimum(pos1, float(win)) - ug
        outs.append(_dot(y.astype(BF16), w_ref[gi]))
    o_ref[0] = (jnp.concatenate(outs, axis=1) * scale_ref[...]).astype(o_ref.dtype)


def pool_mixer(z, pool_w, pool_scale, ts=512):
    b, s, _ = z.shape
    ts = min(ts, s)
    hb = ts // POOL_HALO
    return pl.pallas_call(
        _pool_kernel,
        grid=(b, s // ts),
        in_specs=[pl.BlockSpec((1, ts, POOL_WIDTH), lambda bi, i: (bi, i, 0)),
                  pl.BlockSpec((1, POOL_HALO, POOL_WIDTH),
                               lambda bi, i: (bi, jnp.maximum(i * hb - 1, 0), 0)),
                  _const_spec(pool_w.shape), _const_spec((1, POOL_WIDTH))],
        out_specs=pl.BlockSpec((1, ts, POOL_WIDTH), lambda bi, i: (bi, i, 0)),
        out_shape=jax.ShapeDtypeStruct((b, s, POOL_WIDTH), BF16),
        compiler_params=_params("parallel", "parallel"),
    )(z, z, pool_w, pool_scale.reshape(1, POOL_WIDTH).astype(F32))


DN_ROWS = DN_HEADS * DN_CHUNK
DN_CHUNKS_PER_STEP = 4
BETA_LANE = 0
ALPHA_LANE = DN_HEADS


def _stack_heads(x):
    return jnp.concatenate([x[:, h * DN_HEAD_DIM:(h + 1) * DN_HEAD_DIM] for h in range(DN_HEADS)],
                           axis=0)


def _stack_cols(x, lane0):
    return jnp.concatenate([x[:, lane0 + h:lane0 + h + 1] for h in range(DN_HEADS)], axis=0)


def _pick_head_block(wide, row_head):
    out = jnp.zeros((DN_ROWS, DN_HEAD_DIM), F32)
    for h in range(DN_HEADS):
        out = jnp.where(row_head == h, wide[:, h * DN_HEAD_DIM:(h + 1) * DN_HEAD_DIM], out)
    return out


def _deltanet_kernel(q_ref, k_ref, v_ref, gate_ref, ba_ref, cw_ref, alog_ref, dtb_ref, onorm_ref,
                     o_ref, state_ref, tail_ref):
    c = pl.program_id(1)

    @pl.when(c == 0)
    def _():
        state_ref[...] = jnp.zeros_like(state_ref)
        tail_ref[...] = jnp.zeros_like(tail_ref)

    blk_len = q_ref.shape[1]
    x3 = jnp.concatenate([q_ref[0], k_ref[0], v_ref[0]], axis=1)
    ext = jnp.concatenate([tail_ref[...], x3], axis=0)
    tail_ref[...] = x3[blk_len - 8:, :]
    cw = cw_ref[...]
    y = cw[DN_CONV - 1:DN_CONV, :] * ext
    for j in range(1, DN_CONV):
        y = y + cw[DN_CONV - 1 - j:DN_CONV - j, :] * pltpu.roll(ext, j, axis=0)
    y = y[8:, :]
    y = y * _sigmoid(y)

    def l2n(a):
        return a * lax.rsqrt(jnp.sum(a * a, axis=-1, keepdims=True) + EPS)

    ba = ba_ref[0]
    beta_all = _sigmoid(ba)
    sp_in = ba + dtb_ref[...]
    softplus = jnp.maximum(sp_in, 0.0) + jnp.log(1.0 + jnp.exp(-jnp.abs(sp_in)))
    g_all = -jnp.exp(alog_ref[...]) * softplus
    ri = lax.broadcasted_iota(jnp.int32, (blk_len, blk_len), 0)
    ci = lax.broadcasted_iota(jnp.int32, (blk_len, blk_len), 1)
    in_chunk_tril = (ri >= ci) & (_div_pow2(ri, DN_CHUNK) == _div_pow2(ci, DN_CHUNK))
    gc_blk = _dot(jnp.where(in_chunk_tril, 1.0, 0.0), g_all, precision=HIGHEST)

    rr = lax.broadcasted_iota(jnp.int32, (DN_ROWS, DN_ROWS), 0)
    cc = lax.broadcasted_iota(jnp.int32, (DN_ROWS, DN_ROWS), 1)
    same_head = _div_pow2(rr, DN_CHUNK) == _div_pow2(cc, DN_CHUNK)
    causal = same_head & (rr >= cc)
    strict = same_head & (rr > cc)
    lane_head = _div_pow2(lax.broadcasted_iota(jnp.int32, (DN_CHUNK, DN_ROWS), 1), DN_CHUNK)

    def block_diag(w):
        return jnp.concatenate([jnp.where(lane_head == h, w, 0.0) for h in range(DN_HEADS)], axis=0)

    def hi_lo(x):
        hi = x.astype(BF16).astype(F32)
        return hi, x - hi

    def times_p(x, p_wide):
        p_hi, p_lo = hi_lo(p_wide)
        d_hi, d_lo = block_diag(p_hi).astype(BF16), block_diag(p_lo).astype(BF16)
        x_hi, x_lo = hi_lo(x)
        n = x.shape[0]
        top = _dot(jnp.concatenate([x_hi, x_lo], axis=0).astype(BF16), d_hi)
        return top[:n] + top[n:] + _dot(x_hi.astype(BF16), d_lo)

    wr = lax.broadcasted_iota(jnp.int32, (DN_CHUNK, DN_ROWS), 0)
    wc = lax.broadcasted_iota(jnp.int32, (DN_CHUNK, DN_ROWS), 1)
    eye_wide = jnp.where(wr == (wc & (DN_CHUNK - 1)), 1.0, 0.0)
    n_sq = int(math.log2(DN_CHUNK)) - 1

    def chunk_prep(ci):
        rows = slice(ci * DN_CHUNK, (ci + 1) * DN_CHUNK)
        yc = y[rows, :]
        q_st = l2n(_stack_heads(yc[:, :DN_WIDTH])) * (DN_HEAD_DIM ** -0.5)
        k_st = l2n(_stack_heads(yc[:, DN_WIDTH:2 * DN_WIDTH]))
        v_st = _stack_heads(yc[:, 2 * DN_WIDTH:])
        gc_all = gc_blk[rows, :]
        beta_st = _stack_cols(beta_all[rows, :], BETA_LANE)
        gc_st = _stack_cols(gc_all, ALPHA_LANE)
        g_last = [gc_all[DN_CHUNK - 1:DN_CHUNK, ALPHA_LANE + h:ALPHA_LANE + h + 1]
                  for h in range(DN_HEADS)]
        glast_st = jnp.concatenate([jnp.broadcast_to(g, (DN_CHUNK, 1)) for g in g_last], axis=0)
        gcb = jnp.broadcast_to(gc_st, (DN_ROWS, DN_ROWS))
        decay = jnp.where(causal, jnp.exp(jnp.where(causal, gcb - gcb.T, 0.0)), 0.0)
        kb_st = k_st * beta_st
        k_bf = k_st.astype(BF16)
        a_low = jnp.where(strict, _dot_nt(kb_st.astype(BF16), k_bf) * decay, 0.0)
        p = -(a_low[0:DN_CHUNK] + a_low[DN_CHUNK:2 * DN_CHUNK]
              + a_low[2 * DN_CHUNK:3 * DN_CHUNK] + a_low[3 * DN_CHUNK:])
        t_wide = eye_wide + p
        yield None
        for j in range(n_sq):
            if j == 0:
                p = times_p(p, p)
            else:
                both = times_p(jnp.concatenate([p, t_wide], axis=0), p)
                p, t_wide = both[:DN_CHUNK], t_wide + both[DN_CHUNK:]
            yield None
        t_wide = t_wide + times_p(t_wide, p)
        t_bf = block_diag(t_wide).astype(BF16)
        egc = jnp.exp(gc_st)
        wu = _dot(t_bf, jnp.concatenate([(kb_st * egc).astype(BF16),
                                         (v_st * beta_st).astype(BF16)], axis=1))
        attn = jnp.where(causal, _dot_nt(q_st.astype(BF16), k_bf) * decay, 0.0)
        yield dict(
            w=wu[:, :DN_HEAD_DIM].astype(BF16), u=wu[:, DN_HEAD_DIM:], attn=attn.astype(BF16),
            q_dec=(q_st * egc).astype(BF16), k_dec=(k_st * jnp.exp(glast_st - gc_st)).astype(BF16),
            state_scale=jnp.concatenate(
                [jnp.broadcast_to(jnp.exp(g), (1, DN_HEAD_DIM)) for g in g_last], axis=1))

    preps = [chunk_prep(ci) for ci in range(blk_len // DN_CHUNK)]
    for _ in range(n_sq + 1):
        for prep in preps:
            next(prep)
    preps = [next(prep) for prep in preps]

    state = state_ref[...]
    row_head = _div_pow2(lax.broadcasted_iota(jnp.int32, (DN_ROWS, 1), 0), DN_CHUNK)
    gate = gate_ref[0]
    out_rows = []
    for ci, pr in enumerate(preps):
        s_bf = state.astype(BF16)
        v_new = pr["u"] - _pick_head_block(_dot(pr["w"], s_bf), row_head)
        v_new_bf = v_new.astype(BF16)
        o_st = _pick_head_block(_dot(pr["q_dec"], s_bf), row_head) + _dot(pr["attn"], v_new_bf)
        zero = jnp.zeros_like(v_new_bf)
        v_wide = jnp.concatenate([jnp.where(row_head == h, v_new_bf, zero) for h in range(DN_HEADS)],
                                 axis=1)
        state = state * pr["state_scale"] + _dot_tn(pr["k_dec"], v_wide)
        outs = []
        for h in range(DN_HEADS):
            o_h = _rms(o_st[h * DN_CHUNK:(h + 1) * DN_CHUNK, :], onorm_ref[...])
            g_h = gate[ci * DN_CHUNK:(ci + 1) * DN_CHUNK, h * DN_HEAD_DIM:(h + 1) * DN_HEAD_DIM]
            outs.append(o_h * (g_h * _sigmoid(g_h)))
        out_rows.append(jnp.concatenate(outs, axis=1))
    state_ref[...] = state
    o_ref[0] = jnp.concatenate(out_rows, axis=0).astype(o_ref.dtype)


def gated_deltanet(z, ba_block, conv_w, a_log, dt_bias, o_norm):
    b, s, _ = z.shape
    lane_row = lambda vals, lane0: jnp.zeros((1, LANES), F32).at[0, lane0:lane0 + DN_HEADS].set(
        vals.astype(F32))
    blk_len = min(DN_CHUNKS_PER_STEP * DN_CHUNK, s)
    col = lambda j: pl.BlockSpec((1, blk_len, DN_WIDTH), lambda bi, ci: (bi, ci, j))
    return pl.pallas_call(
        _deltanet_kernel,
        grid=(b, s // blk_len),
        in_specs=[col(1), col(2), col(3), col(4),
                  pl.BlockSpec((1, blk_len, LANES), lambda bi, ci: (bi, ci, ba_block)),
                  _const_spec(conv_w.shape), _const_spec((1, LANES)), _const_spec((1, LANES)),
                  _const_spec((1, DN_HEAD_DIM))],
        out_specs=pl.BlockSpec((1, blk_len, DN_WIDTH), lambda bi, ci: (bi, ci, 0)),
        out_shape=jax.ShapeDtypeStruct((b, s, DN_WIDTH), BF16),
        scratch_shapes=[pltpu.VMEM((DN_HEAD_DIM, DN_WIDTH), F32),
                        pltpu.VMEM((8, 3 * DN_WIDTH), F32)],
        compiler_params=_params("parallel", "arbitrary"),
    )(z, z, z, z, z, conv_w.astype(F32), lane_row(a_log, ALPHA_LANE), lane_row(dt_bias, ALPHA_LANE),
      o_norm.reshape(1, DN_HEAD_DIM).astype(F32))


def _compress_kernel(rk_ref, rv_ref, w1k_ref, pek_ref, w2k_ref, w1v_ref, pev_ref, w2v_ref,
                     ck_ref, cv_ref):
    n = rk_ref.shape[2]
    row = lax.broadcasted_iota(jnp.int32, (n, 1), 0)

    def one(r_ref, w1_ref, pe_ref, w2_ref, o_ref):
        w1 = w1_ref[...]
        pb = _dot(pe_ref[...], w1)
        bias = pb[0:1, :CMP_HIDDEN] + pb[1:2, CMP_HIDDEN:]
        for g in range(NSA_GROUPS):
            y = _dot(r_ref[0, g], w1)
            h = y[:, :CMP_HIDDEN] + pltpu.roll(y[:, CMP_HIDDEN:], n - 1, axis=0) + bias
            a = (h * _sigmoid(h)).astype(BF16)
            o_ref[0, g] = jnp.where(row < n - 1, _dot(a, w2_ref[...]), 0.0).astype(o_ref.dtype)

    one(rk_ref, w1k_ref, pek_ref, w2k_ref, ck_ref)
    one(rv_ref, w1v_ref, pev_ref, w2v_ref, cv_ref)


def compress_kv(rk, rv, w1k, pek, w2k, w1v, pev, w2v):
    b, g, n, w = rk.shape
    blk = pl.BlockSpec((1, g, n, w), lambda bi: (bi, 0, 0, 0))
    oblk = pl.BlockSpec((1, g, n, NSA_HEAD_DIM), lambda bi: (bi, 0, 0, 0))
    return pl.pallas_call(
        _compress_kernel,
        grid=(b,),
        in_specs=[blk, blk, _const_spec(w1k.shape), _const_spec(pek.shape), _const_spec(w2k.shape),
                  _const_spec(w1v.shape), _const_spec(pev.shape), _const_spec(w2v.shape)],
        out_specs=[oblk, oblk],
        out_shape=[jax.ShapeDtypeStruct((b, g, n, NSA_HEAD_DIM), BF16)] * 2,
        compiler_params=_params("parallel"),
    )(rk, rv, w1k, pek, w2k, w1v, pev, w2v)


def _aug_keys(k, pos):
    n = k.shape[0]
    lane = lax.broadcasted_iota(jnp.int32, (n, NSA_HEAD_DIM), 1)
    hi = (_div_pow2(pos, POS_SPLIT) * POS_SPLIT).astype(F32)
    lo = (pos & (POS_SPLIT - 1)).astype(F32)
    cols = jnp.where(lane < SLOPE_PIECES, hi, jnp.where(lane < 2 * SLOPE_PIECES, lo, 0.0))
    return jnp.concatenate([k[:, :NSA_HEAD_DIM], cols.astype(BF16)], axis=1)


def _nsa_kernel(q_ref, gate_ref, slope_ref, ck_ref, cv_ref, ks_ref, vs_ref, kw_ref, vw_ref,
                ovl_ref, o_ref, kc_aug, ks_aug, kw_aug, vs_aug, vw_aug, s_buf, rank_ref, chunk_list,
                *, top_n):
    i = pl.program_id(2)
    tq = q_ref.shape[1]
    rows = NSA_REP * tq
    n_cmp = ck_ref.shape[2]
    assert q_ref.shape[0] == NSA_REP * NSA_HEAD_DIM
    n_slc = ovl_ref.shape[0]
    s_len = ks_ref.shape[1]
    q0 = i * tq

    @pl.when(i == 0)
    def _():
        cpos = lax.broadcasted_iota(jnp.int32, (n_cmp, 1), 0) * CMP_STRIDE + (CMP_LEN - 1)
        kc_aug[...] = _aug_keys(ck_ref[0, 0], cpos)

        def fill(c, carry):
            r0 = pl.multiple_of(c * SLC_CHUNK, SLC_CHUNK)
            pos = r0 + lax.broadcasted_iota(jnp.int32, (SLC_CHUNK, 1), 0)
            blk_lane = lax.broadcasted_iota(jnp.int32, (SLC_CHUNK, LANES), 1)
            onehot = jnp.where(blk_lane == _div_pow2(pos, SLC_LEN), 1.0, 0.0).astype(BF16)
            ks_aug[pl.ds(r0, SLC_CHUNK), :] = jnp.concatenate(
                [_aug_keys(ks_ref[0, pl.ds(r0, SLC_CHUNK), :], pos), onehot], axis=1)
            kw_aug[pl.ds(r0, SLC_CHUNK), :] = _aug_keys(kw_ref[0, pl.ds(r0, SLC_CHUNK), :], pos)
            return carry

        lax.fori_loop(0, s_len // SLC_CHUNK, fill, 0)
        ones_rows = jnp.where(lax.broadcasted_iota(jnp.int32, (V_PAD_ROWS, s_len), 0) == 0, 1.0, 0.0)
        for v_ref, v_aug in ((vs_ref, vs_aug), (vw_ref, vw_aug)):
            v_aug[0:NSA_HEAD_DIM, :] = v_ref[...]
            v_aug[NSA_HEAD_DIM:, :] = ones_rows.astype(BF16)

    q_t = q_ref[...].astype(F32) * (NSA_HEAD_DIM ** -0.5 * LOG2E)
    q_t = jnp.concatenate([q_t[r * NSA_HEAD_DIM:(r + 1) * NSA_HEAD_DIM, :] for r in range(NSA_REP)],
                          axis=1).astype(BF16)
    q_aug = jnp.concatenate([q_t, slope_ref[0].astype(BF16)], axis=0)
    t_lane = q0 + lax.broadcasted_iota(jnp.int32, (1, tq), 1)

    def all_heads(x):
        return jnp.concatenate([x] * NSA_REP, axis=1)

    def col_max(x):
        return jnp.max(x, axis=0, keepdims=True)

    def normalized(acc):
        return acc[:NSA_HEAD_DIM] / acc[NSA_HEAD_DIM:NSA_HEAD_DIM + 1]

    cend = lax.broadcasted_iota(jnp.int32, (n_cmp, 1), 0) * CMP_STRIDE + (CMP_LEN - 1)
    sc = _dot(kc_aug[...], q_aug) + all_heads(jnp.where(cend <= t_lane, 0.0, NEG_INF))

    tile_k0 = pl.multiple_of(q0, tq)
    key_in_tile = lax.broadcasted_iota(jnp.int32, (tq, 1), 0)
    qry_in_tile = lax.broadcasted_iota(jnp.int32, (1, tq), 1)
    sd = (_dot(ks_aug[pl.ds(tile_k0, tq), 0:LANES], q_aug)
          + all_heads(jnp.where(key_in_tile <= qry_in_tile, 0.0, NEG_INF)))

    e = jnp.exp2(sc - col_max(sc))
    any_visible = all_heads(jnp.where(t_lane >= CMP_LEN - 1, 1.0, 0.0))
    p_cmp = e * (any_visible / jnp.sum(e, axis=0, keepdims=True))
    o_cmp = _dot(cv_ref[0, 0], p_cmp.astype(BF16))

    p_sum = p_cmp[:, 0:tq]
    for r in range(1, NSA_REP):
        p_sum = p_sum + p_cmp[:, r * tq:(r + 1) * tq]
    ovl = ovl_ref[...].astype(BF16)
    imp = jnp.zeros((n_slc, tq), F32)
    rest = p_sum
    for _ in range(3):
        piece = rest.astype(BF16)
        imp = imp + _dot(ovl, piece)
        rest = rest - piece.astype(F32)

    span = min(WINDOW + tq, s_len)
    w0 = pl.multiple_of(jnp.maximum(q0 + tq - span, 0), tq)
    dist_w = t_lane - (w0 + lax.broadcasted_iota(jnp.int32, (span, 1), 0))
    band = jnp.where(dist_w >= 0, jnp.where(dist_w < WINDOW, 0.0, NEG_INF), NEG_INF)
    sw = _dot(kw_aug[pl.ds(w0, span), :], q_aug) + all_heads(band)

    blk = lax.broadcasted_iota(jnp.int32, (n_slc, tq), 0)
    tl = q0 + lax.broadcasted_iota(jnp.int32, (n_slc, tq), 1)
    cur = _div_pow2(tl, SLC_LEN)
    forced = (blk == 0) | (blk == cur) | (blk == cur - 1)
    val = jnp.where(forced, FORCE_SCORE, jnp.where(blk * SLC_LEN <= tl, imp, -1.0))
    n_grp = n_slc // RANK_GROUP
    val_grp = [val[RANK_GROUP * g:RANK_GROUP * (g + 1)] for g in range(n_grp)]
    row_in_grp = lax.broadcasted_iota(jnp.int32, (RANK_GROUP, tq), 0)
    rank_ref[...] = jnp.zeros_like(rank_ref)
    for mg in range(n_grp):
        @pl.when(mg * RANK_GROUP * SLC_LEN < q0 + tq)
        def _():
            parts = [rank_ref[RANK_GROUP * g:RANK_GROUP * (g + 1), :] for g in range(n_grp)]
            for m in range(RANK_GROUP * mg, RANK_GROUP * (mg + 1)):
                vm = val[m:m + 1, :]
                for g in range(n_grp):
                    if g < mg:
                        beats = jnp.where(vm > val_grp[g], 1.0, 0.0)
                    elif g > mg:
                        beats = jnp.where(vm >= val_grp[g], 1.0, 0.0)
                    else:
                        beats = jnp.where(row_in_grp > m - RANK_GROUP * mg,
                                          jnp.where(vm >= val_grp[g], 1.0, 0.0),
                                          jnp.where(vm > val_grp[g], 1.0, 0.0))
                    parts[g] = parts[g] + beats
            for g in range(n_grp):
                rank_ref[RANK_GROUP * g:RANK_GROUP * (g + 1), :] = parts[g]
    rank = rank_ref[...]

    before_tile = blk * SLC_LEN < q0
    sel_mask = jnp.where(before_tile, jnp.where(rank < top_n, 0.0, NEG_INF), NEG_INF)
    sel_mask = jnp.concatenate([sel_mask, jnp.zeros((LANES - n_slc, tq), F32)], axis=0)
    q_sel = jnp.concatenate([q_aug, all_heads(sel_mask.astype(BF16))], axis=0)

    n_chunks = s_len // SLC_CHUNK
    blk_any = jnp.max(sel_mask[:n_slc], axis=1, keepdims=True)
    blocks_per_chunk = SLC_CHUNK // SLC_LEN
    needed = jnp.int32(0)
    for j in range(n_chunks):
        hit = jnp.max(blk_any[j * blocks_per_chunk:(j + 1) * blocks_per_chunk]) > 0.5 * NEG_INF
        needed = needed | (hit.astype(jnp.int32) << j)

    m0 = col_max(sd)
    pd = jnp.exp2(sd - m0).astype(BF16)
    acc0 = _dot(vs_aug[:, pl.ds(tile_k0, tq)], pd)

    n_needed = jnp.int32(0)
    for j in range(n_chunks):
        chunk_list[n_needed] = jnp.int32(j)
        n_needed = n_needed + ((needed >> j) & 1)

    n_kb = SLC_CHUNK // SLC_KEY_BLOCK

    def chunk_logits(j, kb):
        kk = pl.multiple_of(j * SLC_CHUNK, SLC_CHUNK) + kb * SLC_KEY_BLOCK
        return _dot(ks_aug[pl.ds(kk, SLC_KEY_BLOCK), :], q_sel)

    first = chunk_list[0]
    first_logits = [chunk_logits(first, kb) for kb in range(n_kb)]
    pw = jnp.exp2(sw - col_max(sw)).astype(BF16)
    o_win = normalized(_dot(vw_aug[:, pl.ds(w0, span)], pw))
    m1 = m0
    for kb, s_new in enumerate(first_logits):
        s_buf[kb * SLC_KEY_BLOCK:(kb + 1) * SLC_KEY_BLOCK, :] = s_new
        m1 = jnp.maximum(m1, col_max(s_new))

    def slc_step(c, carry):
        m_prev, m_cur, acc = carry
        acc = jnp.exp2(m_prev - m_cur) * acc
        k0 = pl.multiple_of(chunk_list[c] * SLC_CHUNK, SLC_CHUNK)
        j_next = chunk_list[jnp.minimum(c + 1, n_needed - 1)]
        m_next = m_cur
        for kb in range(n_kb):
            blk_rows = slice(kb * SLC_KEY_BLOCK, (kb + 1) * SLC_KEY_BLOCK)
            s_new = chunk_logits(j_next, kb)
            p = jnp.exp2(s_buf[blk_rows, :] - m_cur).astype(BF16)
            acc = acc + _dot(vs_aug[:, pl.ds(k0 + kb * SLC_KEY_BLOCK, SLC_KEY_BLOCK)], p)
            s_buf[blk_rows, :] = s_new
            m_next = jnp.maximum(m_next, col_max(s_new))
        return m_cur, m_next, acc

    gates = _sigmoid(gate_ref[0]).T

    def gate_rows(br):
        return jnp.concatenate([gates[3 * r + br:3 * r + br + 1, :] for r in range(NSA_REP)], axis=1)

    o_cmp_win = gate_rows(0) * o_cmp + gate_rows(2) * o_win
    gate_slc = gate_rows(1)

    _, _, acc = lax.fori_loop(0, n_needed, slc_step, (m0, m1, acc0))
    o = o_cmp_win + gate_slc * normalized(acc)
    o_ref[...] = jnp.concatenate([o[:, r * tq:(r + 1) * tq] for r in range(NSA_REP)],
                                 axis=0).astype(o_ref.dtype)


def nsa_attention(zqt, zg, ck, cv, zk, zvt, tq=128):
    b, s, _ = zg.shape
    n_cmp = ck.shape[2]
    n_slc = s // SLC_LEN
    top_n = min(SLC_TOP, n_slc)
    assert tq == 2 * SLC_LEN and top_n >= 3 and n_slc <= LANES and s % SLC_CHUNK == 0
    c_lo = jnp.arange(n_cmp) * CMP_STRIDE
    s_lo = jnp.arange(n_slc) * SLC_LEN
    ovl = (jnp.clip(jnp.minimum(c_lo[None, :] + CMP_LEN, s_lo[:, None] + SLC_LEN)
                    - jnp.maximum(c_lo[None, :], s_lo[:, None]), 0, None).astype(F32) / CMP_LEN)
    ovl = ovl * (jnp.arange(n_cmp) < n_cmp - 1)[None, :]
    hd = jnp.arange(1, NSA_HEADS + 1, dtype=F32)
    rest = jnp.exp2(-8.0 * hd / NSA_HEADS) * LOG2E
    pieces = []
    for _ in range(SLOPE_PIECES):
        piece = rest.astype(BF16).astype(F32)
        pieces.append(piece)
        rest = rest - piece
    pieces = jnp.stack(pieces * 2, axis=0).reshape(2 * SLOPE_PIECES, NSA_GROUPS, NSA_REP)
    slope_tab = jnp.zeros((NSA_GROUPS, NSA_HEAD_DIM, NSA_REP, tq), F32)
    slope_tab = slope_tab.at[:, :2 * SLOPE_PIECES].set(
        jnp.transpose(pieces, (1, 0, 2))[..., None]).reshape(NSA_GROUPS, NSA_HEAD_DIM, NSA_REP * tq)
    def k_spec(branch):
        return pl.BlockSpec((1, s, LANES), lambda bi, g, i: (bi, 0, branch * NSA_GROUPS + g))

    def v_spec(branch):
        return pl.BlockSpec((NSA_HEAD_DIM, s), lambda bi, g, i: (branch * NSA_GROUPS + g, bi))

    ck_spec = pl.BlockSpec((1, 1, n_cmp, NSA_HEAD_DIM), lambda bi, g, i: (bi, g, 0, 0))
    cv_spec = pl.BlockSpec((1, 1, NSA_HEAD_DIM, n_cmp), lambda bi, g, i: (bi, g, 0, 0))
    gw = NSA_REP * NSA_HEAD_DIM
    nq = s // tq
    return pl.pallas_call(
        functools.partial(_nsa_kernel, top_n=top_n),
        grid=(b, NSA_GROUPS, s // tq),
        in_specs=[pl.BlockSpec((gw, tq), lambda bi, g, i: (g, bi * nq + i)),
                  pl.BlockSpec((1, tq, LANES), lambda bi, g, i: (bi, i, g)),
                  pl.BlockSpec((1, NSA_HEAD_DIM, NSA_REP * tq), lambda bi, g, i: (g, 0, 0)),
                  ck_spec, cv_spec, k_spec(0), v_spec(0), k_spec(1), v_spec(1),
                  _const_spec(ovl.shape)],
        out_specs=pl.BlockSpec((gw, tq), lambda bi, g, i: (g, bi * nq + i)),
        out_shape=jax.ShapeDtypeStruct((NSA_GROUPS * gw, b * s), BF16),
        scratch_shapes=[pltpu.VMEM((n_cmp, 2 * NSA_HEAD_DIM), BF16),
                        pltpu.VMEM((s, 2 * NSA_HEAD_DIM + LANES), BF16),
                        pltpu.VMEM((s, 2 * NSA_HEAD_DIM), BF16),
                        pltpu.VMEM((NSA_HEAD_DIM + V_PAD_ROWS, s), BF16),
                        pltpu.VMEM((NSA_HEAD_DIM + V_PAD_ROWS, s), BF16),
                        pltpu.VMEM((SLC_CHUNK, NSA_REP * tq), F32),
                        pltpu.VMEM((n_slc, tq), F32),
                        pltpu.SMEM((s // SLC_CHUNK,), jnp.int32)],
        compiler_params=_params("parallel", "parallel", "arbitrary"),
    )(zqt, zg, slope_tab, ck, cv, zk, zvt, zk, zvt, ovl)


def _pad_cols(w, n):
    return jnp.pad(w, ((0, 0), (0, n - w.shape[1])))


def pool_delta_layer(x, ln, w_in, pool_w, pool_scale, conv_w, a_log, dt_bias, o_norm, w_out):
    b, s, d = x.shape
    main = POOL_WIDTH + 4 * DN_WIDTH
    w_all = jnp.concatenate([w_in[:, :main], _pad_cols(w_in[:, main:], LANES)], axis=1).astype(BF16)
    (z,) = norm_matmul(x.reshape(b * s, d), ln, [w_all], [F32])
    z = z.reshape(b, s, main + LANES)
    y_pool = pool_mixer(z, pool_w.astype(BF16), pool_scale)
    y_dn = gated_deltanet(z, main // LANES, conv_w, a_log, dt_bias, o_norm)
    w_out = w_out.astype(BF16)
    return [y_pool, y_dn], [w_out[:POOL_WIDTH], w_out[POOL_WIDTH:]]


def _cmp_weights(pe, w1, w2):
    half = (CMP_LEN // 2) * NSA_HEAD_DIM
    w1_pair = jnp.concatenate([w1[:half], w1[half:]], axis=1).astype(BF16)
    pe_rows = jnp.zeros((8, half), F32).at[0:2].set(pe.reshape(2, half)).astype(BF16)
    return w1_pair, pe_rows, w2.astype(BF16)


def nsa_layer(x, ln, w_in, pe_k, w1_k, w2_k, pe_v, w1_v, w2_v, w_out):
    b, s, d = x.shape
    g_, r_, hd = NSA_GROUPS, NSA_REP, NSA_HEAD_DIM
    kvw = g_ * hd
    main = d + 6 * kvw
    wg = w_in[:, main:].reshape(d, g_, r_ * 3)
    wg = jnp.pad(wg, ((0, 0), (0, 0), (0, LANES - r_ * 3))).reshape(d, g_ * LANES)
    cols = lambda j: w_in[:, d + j * kvw:d + (j + 1) * kvw]

    def lane_block_per_group(w):
        return jnp.pad(w.reshape(d, g_, hd), ((0, 0), (0, 0), (0, LANES - hd))).reshape(d, g_ * LANES)

    w_k = jnp.concatenate([lane_block_per_group(cols(2)), lane_block_per_group(cols(4))], axis=1)
    w_vt = jnp.concatenate([cols(3), cols(5)], axis=1).T
    z, zg, zk, zvt, zqt = norm_matmul(
        x.reshape(b * s, d), ln,
        [w_in[:, d:d + 2 * kvw].astype(BF16), wg.astype(BF16), w_k.astype(BF16)], [BF16, F32, BF16],
        wts=[w_vt.astype(BF16), w_in[:, :d].T.astype(BF16)])
    z = z.reshape(b, s, 2 * kvw)
    zg = zg.reshape(b, s, g_ * LANES)
    zk = zk.reshape(b, s, 2 * g_ * LANES)

    def group_major(j):
        return jnp.transpose(z[..., j * kvw:(j + 1) * kvw].reshape(b, s, g_, hd), (0, 2, 1, 3))

    n_str = s // CMP_STRIDE
    rk = group_major(0).reshape(b, g_, n_str, CMP_STRIDE * hd)
    rv = group_major(1).reshape(b, g_, n_str, CMP_STRIDE * hd)
    ck, cv = compress_kv(rk, rv, *_cmp_weights(pe_k, w1_k, w2_k), *_cmp_weights(pe_v, w1_v, w2_v))
    o_t = nsa_attention(zqt, zg, ck, jnp.swapaxes(cv, 2, 3), zk, zvt)
    return [o_t], [w_out.astype(BF16)]


def kernel(x, mem, a_ln, a_w_in, a_pool_w, a_pool_scale, a_conv_w, a_a_log, a_dt_bias, a_o_norm, a_w_out, c_ln, c_w_in, c_pe_k, c_w1_k, c_w2_k, c_pe_v, c_w1_v, c_w2_v, c_w_out, xa_ln, xa_mem_ln, xa_wq, xa_wk, xa_wv, xa_wo, ff_ln, ff_w1, ff_w2, final_ln):
    b, s, d = x.shape
    depth = xa_ln.shape[0]
    mem2d = mem.reshape(b * mem.shape[1], d)
    for l in range(depth):
        i = l // 2
        if l % 2 == 0:
            mix_a, mix_w = pool_delta_layer(x, a_ln[i], a_w_in[i], a_pool_w[i], a_pool_scale[i],
                                            a_conv_w[i], a_a_log[i], a_dt_bias[i], a_o_norm[i],
                                            a_w_out[i])
        else:
            mix_a, mix_w = nsa_layer(x, c_ln[i], c_w_in[i], c_pe_k[i], c_w1_k[i], c_w2_k[i],
                                     c_pe_v[i], c_w1_v[i], c_w2_v[i], c_w_out[i])
        mk, mv = norm_matmul(mem2d, xa_mem_ln[l], [xa_wk[l].astype(BF16), xa_wv[l].astype(BF16)],
                             [BF16, BF16])
        x = xattn_sublayer(x, mix_a, mix_w, xa_ln[l], xa_wq[l].astype(BF16), mk, mv,
                           xa_wo[l].astype(BF16))
        x = mlp_sublayer(x.reshape(b * s, d), ff_ln[l], ff_w1[l].astype(BF16), ff_w2[l].astype(BF16),
                         final_ln, final_norm=(l == depth - 1)).reshape(b, s, d)
    return x
```

```python
import functools
import math

import jax
import jax.numpy as jnp
from jax import lax
from jax.experimental import pallas as pl
from jax.experimental.pallas import tpu as pltpu

F32 = jnp.float32
BF16 = jnp.bfloat16

LANES = 128
VMEM_LIMIT = 56 * 1024 * 1024

EPS = 1e-6
NEG_INF = -1e30
FORCE_SCORE = 1e4

POOL_WINDOWS = (2, 4, 8, 16)
POOL_GROUP_DIM = 128
POOL_WIDTH = 512
DN_HEADS = 4
DN_HEAD_DIM = 128
DN_WIDTH = 512
DN_CONV = 4
DN_CHUNK = 64

NSA_HEAD_DIM = 64
NSA_GROUPS = 4
NSA_REP = 4
NSA_HEADS = 16
CMP_LEN = 32
CMP_STRIDE = 16
CMP_HIDDEN = 128
SLC_LEN = 64
SLC_TOP = 16
WINDOW = 512
SLC_CHUNK = 512
SLC_KEY_BLOCK = 256
RANK_GROUP = 8
V_PAD_ROWS = 16
LOG2E = 1.4426950408889634
SLOPE_PIECES = 3
POS_SPLIT = 64

XA_HEADS = 4
XA_HEAD_DIM = 256


def _params(*sem):
    return pltpu.CompilerParams(dimension_semantics=sem, vmem_limit_bytes=VMEM_LIMIT)


def _const_spec(shape):
    nd = len(shape)
    return pl.BlockSpec(shape, lambda *_: (0,) * nd)


def _rms(x, g):
    return x * lax.rsqrt(jnp.mean(x * x, axis=-1, keepdims=True) + EPS) * g


def _sigmoid(x):
    return 1.0 / (1.0 + jnp.exp(-x))


def _div_pow2(x, n):
    return lax.shift_right_logical(x, jnp.int32(int(math.log2(n))))


def _dot(a, b, precision=None):
    return jnp.dot(a, b, preferred_element_type=F32, precision=precision)


def _dot_nt(a, b, precision=None):
    return lax.dot_general(a, b, (((1,), (1,)), ((), ())), preferred_element_type=F32,
                           precision=precision)


def _dot_tn(a, b):
    return lax.dot_general(a, b, (((0,), (0,)), ((), ())), preferred_element_type=F32)


def _norm_matmul_kernel(x_ref, g_ref, *refs, n_plain):
    n = len(refs) // 2
    hb = _rms(x_ref[...], g_ref[...]).astype(BF16)
    for j, (w_ref, o_ref) in enumerate(zip(refs[:n], refs[n:])):
        if j < n_plain:
            o_ref[...] = _dot(hb, w_ref[...]).astype(o_ref.dtype)
        else:
            o_ref[...] = _dot_nt(w_ref[...], hb).astype(o_ref.dtype)


def norm_matmul(x2d, g, ws, out_dtypes, wts=(), tm=512):
    t, d = x2d.shape
    tm = min(tm, t)
    return pl.pallas_call(
        functools.partial(_norm_matmul_kernel, n_plain=len(ws)),
        grid=(t // tm,),
        in_specs=[pl.BlockSpec((tm, d), lambda i: (i, 0)), _const_spec((1, d))]
        + [_const_spec(w.shape) for w in (*ws, *wts)],
        out_specs=[pl.BlockSpec((tm, w.shape[1]), lambda i: (i, 0)) for w in ws]
        + [pl.BlockSpec((w.shape[0], tm), lambda i: (0, i)) for w in wts],
        out_shape=[jax.ShapeDtypeStruct((t, w.shape[1]), dt) for w, dt in zip(ws, out_dtypes)]
        + [jax.ShapeDtypeStruct((w.shape[0], t), BF16) for w in wts],
        compiler_params=_params("parallel"),
    )(x2d, g.reshape(1, d).astype(F32), *ws, *wts)


def _mlp_kernel(x_ref, g_ref, w1_ref, w2_ref, gf_ref, o_ref, *, ff_chunk, final_norm):
    x = x_ref[...]
    hb = _rms(x, g_ref[...]).astype(BF16)
    acc = x
    for c in range(w1_ref.shape[1] // ff_chunk):
        a = jnp.maximum(_dot(hb, w1_ref[:, c * ff_chunk:(c + 1) * ff_chunk]), 0.0)
        acc = acc + _dot((a * a).astype(BF16), w2_ref[c * ff_chunk:(c + 1) * ff_chunk, :])
    if final_norm:
        acc = _rms(acc, gf_ref[...])
    o_ref[...] = acc


def mlp_sublayer(x2d, g, w1, w2, gf, final_norm, tm=512, ff_chunk=1024):
    t, d = x2d.shape
    tm = min(tm, t)
    return pl.pallas_call(
        functools.partial(_mlp_kernel, ff_chunk=ff_chunk, final_norm=final_norm),
        grid=(t // tm,),
        in_specs=[pl.BlockSpec((tm, d), lambda i: (i, 0)), _const_spec((1, d)),
                  _const_spec(w1.shape), _const_spec(w2.shape), _const_spec((1, d))],
        out_specs=pl.BlockSpec((tm, d), lambda i: (i, 0)),
        out_shape=jax.ShapeDtypeStruct((t, d), F32),
        compiler_params=_params("parallel"),
    )(x2d, g.reshape(1, d).astype(F32), w1, w2, gf.reshape(1, d).astype(F32))


def _xattn_kernel(x_ref, *refs, mix_transposed):
    n_mix = len(mix_transposed)
    g_ref, wq_ref, k_ref, v_ref, wo_ref, o_ref = refs[2 * n_mix:]
    x = x_ref[0]
    for a_ref, w_ref, transposed in zip(refs[:n_mix], refs[n_mix:2 * n_mix], mix_transposed):
        x = x + (_dot_tn(a_ref[...], w_ref[...]) if transposed else _dot(a_ref[0], w_ref[...]))
    hb = _rms(x, g_ref[...]).astype(BF16)
    q = (_dot(hb, wq_ref[...]) * (XA_HEAD_DIM ** -0.5)).astype(BF16)
    heads = []
    for h in range(XA_HEADS):
        sl = slice(h * XA_HEAD_DIM, (h + 1) * XA_HEAD_DIM)
        s = _dot_nt(q[:, sl], k_ref[:, sl])
        p = jnp.exp(s - jnp.max(s, axis=-1, keepdims=True))
        l = jnp.sum(p, axis=-1, keepdims=True)
        heads.append((_dot(p.astype(BF16), v_ref[:, sl]) / l).astype(BF16))
    o_ref[0] = x + _dot(jnp.concatenate(heads, axis=1), wo_ref[...])


def xattn_sublayer(x, mix_a, mix_w, g, wq, k2d, v2d, wo, tq=512):
    b, s, d = x.shape
    tq = min(tq, s)
    m = k2d.shape[0] // b
    nq = s // tq

    def mix_spec(a):
        if a.ndim == 2:
            return pl.BlockSpec((a.shape[0], tq), lambda bi, i: (0, bi * nq + i))
        return pl.BlockSpec((1, tq, a.shape[2]), lambda bi, i: (bi, i, 0))

    return pl.pallas_call(
        functools.partial(_xattn_kernel, mix_transposed=tuple(a.ndim == 2 for a in mix_a)),
        grid=(b, nq),
        in_specs=[pl.BlockSpec((1, tq, d), lambda bi, i: (bi, i, 0))]
        + [mix_spec(a) for a in mix_a]
        + [_const_spec(w.shape) for w in mix_w]
        + [_const_spec((1, d)), _const_spec(wq.shape),
           pl.BlockSpec((m, d), lambda bi, i: (bi, 0)),
           pl.BlockSpec((m, d), lambda bi, i: (bi, 0)),
           _const_spec(wo.shape)],
        out_specs=pl.BlockSpec((1, tq, d), lambda bi, i: (bi, i, 0)),
        out_shape=jax.ShapeDtypeStruct((b, s, d), F32),
        compiler_params=_params("parallel", "parallel"),
    )(x, *mix_a, *mix_w, g.reshape(1, d).astype(F32), wq, k2d, v2d, wo)


POOL_HALO = 16


def _pool_kernel(u_ref, halo_ref, w_ref, scale_ref, o_ref):
    i = pl.program_id(1)
    ts = u_ref.shape[1]
    u = u_ref[0]
    halo = jnp.where(i == 0, 0.0, halo_ref[0])
    ext = jnp.concatenate([halo, u], axis=0)
    sums = [None] * len(POOL_WINDOWS)
    cur = ext
    for gi, win in enumerate(POOL_WINDOWS):
        cur = cur[:, (POOL_GROUP_DIM if gi else 0):]
        cur = cur + pltpu.roll(cur, win // 2, axis=0)
        sums[gi] = cur[POOL_HALO:, :POOL_GROUP_DIM]
    pos1 = (i * ts + 1 + lax.broadcasted_iota(jnp.int32, (ts, 1), 0)).astype(F32)
    outs = []
    for gi, win in enumerate(POOL_WINDOWS):
        ug = u[:, gi * POOL_GROUP_DIM:(gi + 1) * POOL_GROUP_DIM]
        y = sums[gi] / jnp.minimum(pos1, float(win)) - ug
        outs.append(_dot(y.astype(BF16), w_ref[gi]))
    o_ref[0] = (jnp.concatenate(outs, axis=1) * scale_ref[...]).astype(o_ref.dtype)


def pool_mixer(z, pool_w, pool_scale, ts=512):
    b, s, _ = z.shape
    ts = min(ts, s)
    hb = ts // POOL_HALO
    return pl.pallas_call(
        _pool_kernel,
        grid=(b, s // ts),
        in_specs=[pl.BlockSpec((1, ts, POOL_WIDTH), lambda bi, i: (bi, i, 0)),
                  pl.BlockSpec((1, POOL_HALO, POOL_WIDTH),
                               lambda bi, i: (bi, jnp.maximum(i * hb - 1, 0), 0)),
                  _const_spec(pool_w.shape), _const_spec((1, POOL_WIDTH))],
        out_specs=pl.BlockSpec((1, ts, POOL_WIDTH), lambda bi, i: (bi, i, 0)),
        out_shape=jax.ShapeDtypeStruct((b, s, POOL_WIDTH), BF16),
        compiler_params=_params("parallel", "parallel"),
    )(z, z, pool_w, pool_scale.reshape(1, POOL_WIDTH).astype(F32))


DN_ROWS = DN_HEADS * DN_CHUNK
DN_CHUNKS_PER_STEP = 8
DN_GROUP = 4
BETA_LANE = 0
ALPHA_LANE = DN_HEADS


def _stack_heads(x):
    return jnp.concatenate([x[:, h * DN_HEAD_DIM:(h + 1) * DN_HEAD_DIM] for h in range(DN_HEADS)],
                           axis=0)


def _stack_cols(x, lane0):
    return jnp.concatenate([x[:, lane0 + h:lane0 + h + 1] for h in range(DN_HEADS)], axis=0)


def _pick_head_block(wide, row_head):
    out = jnp.zeros((DN_ROWS, DN_HEAD_DIM), F32)
    for h in range(DN_HEADS):
        out = jnp.where(row_head == h, wide[:, h * DN_HEAD_DIM:(h + 1) * DN_HEAD_DIM], out)
    return out


def _deltanet_kernel(q_ref, k_ref, v_ref, gate_ref, ba_ref, cw_ref, alog_ref, dtb_ref, onorm_ref,
                     o_ref, state_ref, tail_ref):
    c = pl.program_id(1)

    @pl.when(c == 0)
    def _():
        state_ref[...] = jnp.zeros_like(state_ref)
        tail_ref[...] = jnp.zeros_like(tail_ref)

    blk_len = q_ref.shape[1]
    x3 = jnp.concatenate([q_ref[0], k_ref[0], v_ref[0]], axis=1)
    ext = jnp.concatenate([tail_ref[...], x3], axis=0)
    tail_ref[...] = x3[blk_len - 8:, :]
    cw = cw_ref[...]
    y = cw[DN_CONV - 1:DN_CONV, :] * ext
    for j in range(1, DN_CONV):
        y = y + cw[DN_CONV - 1 - j:DN_CONV - j, :] * pltpu.roll(ext, j, axis=0)
    y = y[8:, :]
    y = y * _sigmoid(y)

    def l2n(a):
        return a * lax.rsqrt(jnp.sum(a * a, axis=-1, keepdims=True) + EPS)

    ba = ba_ref[0]
    beta_all = _sigmoid(ba)
    sp_in = ba + dtb_ref[...]
    softplus = jnp.maximum(sp_in, 0.0) + jnp.log(1.0 + jnp.exp(-jnp.abs(sp_in)))
    g_all = -jnp.exp(alog_ref[...]) * softplus
    ri = lax.broadcasted_iota(jnp.int32, (DN_CHUNK, DN_CHUNK), 0)
    ci = lax.broadcasted_iota(jnp.int32, (DN_CHUNK, DN_CHUNK), 1)
    tril_ones = jnp.where(ri >= ci, 1.0, 0.0).astype(BF16)

    def chunk_cumsum(g):
        total = jnp.zeros_like(g)
        rest = g
        for _ in range(3):
            piece = rest.astype(BF16)
            total = total + _dot(tril_ones, piece)
            rest = rest - piece.astype(F32)
        return total

    rr = lax.broadcasted_iota(jnp.int32, (DN_ROWS, DN_ROWS), 0)
    cc = lax.broadcasted_iota(jnp.int32, (DN_ROWS, DN_ROWS), 1)
    same_head = _div_pow2(rr, DN_CHUNK) == _div_pow2(cc, DN_CHUNK)
    causal = same_head & (rr >= cc)
    strict = same_head & (rr > cc)
    lane_head = _div_pow2(lax.broadcasted_iota(jnp.int32, (DN_CHUNK, DN_ROWS), 1), DN_CHUNK)

    def block_diag(w):
        return jnp.concatenate([jnp.where(lane_head == h, w, 0.0) for h in range(DN_HEADS)], axis=0)

    def hi_lo(x):
        hi = x.astype(BF16).astype(F32)
        return hi, x - hi

    def times_p(x, p_wide):
        p_hi, p_lo = hi_lo(p_wide)
        d_hi, d_lo = block_diag(p_hi).astype(BF16), block_diag(p_lo).astype(BF16)
        x_hi, x_lo = hi_lo(x)
        n = x.shape[0]
        top = _dot(jnp.concatenate([x_hi, x_lo], axis=0).astype(BF16), d_hi)
        return top[:n] + top[n:] + _dot(x_hi.astype(BF16), d_lo)

    wr = lax.broadcasted_iota(jnp.int32, (DN_CHUNK, DN_ROWS), 0)
    wc = lax.broadcasted_iota(jnp.int32, (DN_CHUNK, DN_ROWS), 1)
    eye_wide = jnp.where(wr == (wc & (DN_CHUNK - 1)), 1.0, 0.0)
    n_sq = int(math.log2(DN_CHUNK)) - 1

    def chunk_prep(ci):
        rows = slice(ci * DN_CHUNK, (ci + 1) * DN_CHUNK)
        yc = y[rows, :]
        q_st = l2n(_stack_heads(yc[:, :DN_WIDTH])) * (DN_HEAD_DIM ** -0.5)
        k_st = l2n(_stack_heads(yc[:, DN_WIDTH:2 * DN_WIDTH]))
        v_st = _stack_heads(yc[:, 2 * DN_WIDTH:])
        gc_all = chunk_cumsum(g_all[rows, :])
        beta_st = _stack_cols(beta_all[rows, :], BETA_LANE)
        gc_st = _stack_cols(gc_all, ALPHA_LANE)
        g_last = [gc_all[DN_CHUNK - 1:DN_CHUNK, ALPHA_LANE + h:ALPHA_LANE + h + 1]
                  for h in range(DN_HEADS)]
        glast_st = jnp.concatenate([jnp.broadcast_to(g, (DN_CHUNK, 1)) for g in g_last], axis=0)
        gcb = jnp.broadcast_to(gc_st, (DN_ROWS, DN_ROWS))
        decay = jnp.where(causal, jnp.exp(jnp.where(causal, gcb - gcb.T, 0.0)), 0.0)
        kb_st = k_st * beta_st
        k_bf = k_st.astype(BF16)
        a_low = jnp.where(strict, _dot_nt(kb_st.astype(BF16), k_bf) * decay, 0.0)
        p = -(a_low[0:DN_CHUNK] + a_low[DN_CHUNK:2 * DN_CHUNK]
              + a_low[2 * DN_CHUNK:3 * DN_CHUNK] + a_low[3 * DN_CHUNK:])
        t_wide = eye_wide + p
        yield None
        for j in range(n_sq):
            if j == 0:
                p = times_p(p, p)
            else:
                both = times_p(jnp.concatenate([p, t_wide], axis=0), p)
                p, t_wide = both[:DN_CHUNK], t_wide + both[DN_CHUNK:]
            yield None
        t_wide = t_wide + times_p(t_wide, p)
        t_bf = block_diag(t_wide).astype(BF16)
        egc = jnp.exp(gc_st)
        wu = _dot(t_bf, jnp.concatenate([(kb_st * egc).astype(BF16),
                                         (v_st * beta_st).astype(BF16)], axis=1))
        attn = jnp.where(causal, _dot_nt(q_st.astype(BF16), k_bf) * decay, 0.0)
        yield dict(
            w=wu[:, :DN_HEAD_DIM].astype(BF16), u=wu[:, DN_HEAD_DIM:], attn=attn.astype(BF16),
            q_dec=(q_st * egc).astype(BF16), k_dec=(k_st * jnp.exp(glast_st - gc_st)).astype(BF16),
            state_scale=jnp.concatenate(
                [jnp.broadcast_to(jnp.exp(g), (1, DN_HEAD_DIM)) for g in g_last], axis=1))

    row_head = _div_pow2(lax.broadcasted_iota(jnp.int32, (DN_ROWS, 1), 0), DN_CHUNK)
    gate = gate_ref[0]
    n_chunks = blk_len // DN_CHUNK
    carry = {"state": state_ref[...]}
    prepared = [None] * n_chunks
    out_rows = [None] * n_chunks

    def recurrence(chunks):
        for ci in chunks:
            pr = prepared[ci]
            s_bf = carry["state"].astype(BF16)
            w_s = _dot(pr["w"], s_bf)
            q_s = _dot(pr["q_dec"], s_bf)
            yield
            v_new_bf = (pr["u"] - _pick_head_block(w_s, row_head)).astype(BF16)
            zero = jnp.zeros_like(v_new_bf)
            v_wide = jnp.concatenate(
                [jnp.where(row_head == h, v_new_bf, zero) for h in range(DN_HEADS)], axis=1)
            carry["state"] = carry["state"] * pr["state_scale"] + _dot_tn(pr["k_dec"], v_wide)
            o_st = _pick_head_block(q_s, row_head) + _dot(pr["attn"], v_new_bf)
            yield
            outs = []
            for h in range(DN_HEADS):
                o_h = _rms(o_st[h * DN_CHUNK:(h + 1) * DN_CHUNK, :], onorm_ref[...])
                g_h = gate[ci * DN_CHUNK:(ci + 1) * DN_CHUNK, h * DN_HEAD_DIM:(h + 1) * DN_HEAD_DIM]
                outs.append(o_h * (g_h * _sigmoid(g_h)))
            out_rows[ci] = jnp.concatenate(outs, axis=1)
            yield

    def advance(gen, n):
        for _ in range(n):
            next(gen, None)

    groups = [list(range(g0, min(g0 + DN_GROUP, n_chunks))) for g0 in range(0, n_chunks, DN_GROUP)]
    pending = iter(())
    for group in groups:
        preps = [chunk_prep(ci) for ci in group]
        rec_pieces = 3 * DN_GROUP
        for stage in range(n_sq + 2):
            for ci, prep in zip(group, preps):
                result = next(prep)
                if result is not None:
                    prepared[ci] = result
            advance(pending, -(-rec_pieces // (n_sq + 2)))
        advance(pending, rec_pieces)
        pending = recurrence(group)
    advance(pending, 3 * DN_GROUP)
    state_ref[...] = carry["state"]
    o_ref[0] = jnp.concatenate(out_rows, axis=0).astype(o_ref.dtype)


def gated_deltanet(z, ba_block, conv_w, a_log, dt_bias, o_norm):
    b, s, _ = z.shape
    lane_row = lambda vals, lane0: jnp.zeros((1, LANES), F32).at[0, lane0:lane0 + DN_HEADS].set(
        vals.astype(F32))
    blk_len = min(DN_CHUNKS_PER_STEP * DN_CHUNK, s)
    col = lambda j: pl.BlockSpec((1, blk_len, DN_WIDTH), lambda bi, ci: (bi, ci, j))
    return pl.pallas_call(
        _deltanet_kernel,
        grid=(b, s // blk_len),
        in_specs=[col(1), col(2), col(3), col(4),
                  pl.BlockSpec((1, blk_len, LANES), lambda bi, ci: (bi, ci, ba_block)),
                  _const_spec(conv_w.shape), _const_spec((1, LANES)), _const_spec((1, LANES)),
                  _const_spec((1, DN_HEAD_DIM))],
        out_specs=pl.BlockSpec((1, blk_len, DN_WIDTH), lambda bi, ci: (bi, ci, 0)),
        out_shape=jax.ShapeDtypeStruct((b, s, DN_WIDTH), BF16),
        scratch_shapes=[pltpu.VMEM((DN_HEAD_DIM, DN_WIDTH), F32),
                        pltpu.VMEM((8, 3 * DN_WIDTH), F32)],
        compiler_params=_params("parallel", "arbitrary"),
    )(z, z, z, z, z, conv_w.astype(F32), lane_row(a_log, ALPHA_LANE), lane_row(dt_bias, ALPHA_LANE),
      o_norm.reshape(1, DN_HEAD_DIM).astype(F32))


def _compress_kernel(rk_ref, rv_ref, w1k_ref, pek_ref, w2k_ref, w1v_ref, pev_ref, w2v_ref,
                     ck_ref, cv_ref):
    n = rk_ref.shape[2]
    row = lax.broadcasted_iota(jnp.int32, (n, 1), 0)

    def one(r_ref, w1_ref, pe_ref, w2_ref, o_ref):
        w1 = w1_ref[...]
        pb = _dot(pe_ref[...], w1)
        bias = pb[0:1, :CMP_HIDDEN] + pb[1:2, CMP_HIDDEN:]
        for g in range(NSA_GROUPS):
            y = _dot(r_ref[0, g], w1)
            h = y[:, :CMP_HIDDEN] + pltpu.roll(y[:, CMP_HIDDEN:], n - 1, axis=0) + bias
            a = (h * _sigmoid(h)).astype(BF16)
            o_ref[0, g] = jnp.where(row < n - 1, _dot(a, w2_ref[...]), 0.0).astype(o_ref.dtype)

    one(rk_ref, w1k_ref, pek_ref, w2k_ref, ck_ref)
    one(rv_ref, w1v_ref, pev_ref, w2v_ref, cv_ref)


def compress_kv(rk, rv, w1k, pek, w2k, w1v, pev, w2v):
    b, g, n, w = rk.shape
    blk = pl.BlockSpec((1, g, n, w), lambda bi: (bi, 0, 0, 0))
    oblk = pl.BlockSpec((1, g, n, NSA_HEAD_DIM), lambda bi: (bi, 0, 0, 0))
    return pl.pallas_call(
        _compress_kernel,
        grid=(b,),
        in_specs=[blk, blk, _const_spec(w1k.shape), _const_spec(pek.shape), _const_spec(w2k.shape),
                  _const_spec(w1v.shape), _const_spec(pev.shape), _const_spec(w2v.shape)],
        out_specs=[oblk, oblk],
        out_shape=[jax.ShapeDtypeStruct((b, g, n, NSA_HEAD_DIM), BF16)] * 2,
        compiler_params=_params("parallel"),
    )(rk, rv, w1k, pek, w2k, w1v, pev, w2v)


def _aug_keys(k, pos):
    n = k.shape[0]
    lane = lax.broadcasted_iota(jnp.int32, (n, NSA_HEAD_DIM), 1)
    hi = (_div_pow2(pos, POS_SPLIT) * POS_SPLIT).astype(F32)
    lo = (pos & (POS_SPLIT - 1)).astype(F32)
    cols = jnp.where(lane < SLOPE_PIECES, hi, jnp.where(lane < 2 * SLOPE_PIECES, lo, 0.0))
    return jnp.concatenate([k[:, :NSA_HEAD_DIM], cols.astype(BF16)], axis=1)


def _nsa_kernel(q_ref, gate_ref, slope_ref, ck_ref, cv_ref, ks_ref, vs_ref, kw_ref, vw_ref,
                ovl_ref, o_ref, kc_aug, ks_aug, kw_aug, vs_aug, vw_aug, s_buf, rank_ref, chunk_list,
                *, top_n):
    i = pl.program_id(2)
    tq = q_ref.shape[1]
    rows = NSA_REP * tq
    n_cmp = ck_ref.shape[2]
    assert q_ref.shape[0] == NSA_REP * NSA_HEAD_DIM
    n_slc = ovl_ref.shape[0]
    s_len = ks_ref.shape[1]
    q0 = i * tq

    @pl.when(i == 0)
    def _():
        cpos = lax.broadcasted_iota(jnp.int32, (n_cmp, 1), 0) * CMP_STRIDE + (CMP_LEN - 1)
        kc_aug[...] = _aug_keys(ck_ref[0, 0], cpos)

        def fill(c, carry):
            r0 = pl.multiple_of(c * SLC_CHUNK, SLC_CHUNK)
            pos = r0 + lax.broadcasted_iota(jnp.int32, (SLC_CHUNK, 1), 0)
            blk_lane = lax.broadcasted_iota(jnp.int32, (SLC_CHUNK, LANES), 1)
            onehot = jnp.where(blk_lane == _div_pow2(pos, SLC_LEN), 1.0, 0.0).astype(BF16)
            ks_aug[pl.ds(r0, SLC_CHUNK), :] = jnp.concatenate(
                [_aug_keys(ks_ref[0, pl.ds(r0, SLC_CHUNK), :], pos), onehot], axis=1)
            kw_aug[pl.ds(r0, SLC_CHUNK), :] = _aug_keys(kw_ref[0, pl.ds(r0, SLC_CHUNK), :], pos)
            return carry

        lax.fori_loop(0, s_len // SLC_CHUNK, fill, 0)
        ones_rows = jnp.where(lax.broadcasted_iota(jnp.int32, (V_PAD_ROWS, s_len), 0) == 0, 1.0, 0.0)
        for v_ref, v_aug in ((vs_ref, vs_aug), (vw_ref, vw_aug)):
            v_aug[0:NSA_HEAD_DIM, :] = v_ref[...]
            v_aug[NSA_HEAD_DIM:, :] = ones_rows.astype(BF16)

    q_t = q_ref[...].astype(F32) * (NSA_HEAD_DIM ** -0.5 * LOG2E)
    q_t = jnp.concatenate([q_t[r * NSA_HEAD_DIM:(r + 1) * NSA_HEAD_DIM, :] for r in range(NSA_REP)],
                          axis=1).astype(BF16)
    q_aug = jnp.concatenate([q_t, slope_ref[0].astype(BF16)], axis=0)
    t_lane = q0 + lax.broadcasted_iota(jnp.int32, (1, tq), 1)

    def all_heads(x):
        return jnp.concatenate([x] * NSA_REP, axis=1)

    def col_max(x):
        return jnp.max(x, axis=0, keepdims=True)

    def normalized(acc):
        return acc[:NSA_HEAD_DIM] / acc[NSA_HEAD_DIM:NSA_HEAD_DIM + 1]

    cend = lax.broadcasted_iota(jnp.int32, (n_cmp, 1), 0) * CMP_STRIDE + (CMP_LEN - 1)
    sc = _dot(kc_aug[...], q_aug) + all_heads(jnp.where(cend <= t_lane, 0.0, NEG_INF))

    tile_k0 = pl.multiple_of(q0, tq)
    key_in_tile = lax.broadcasted_iota(jnp.int32, (tq, 1), 0)
    qry_in_tile = lax.broadcasted_iota(jnp.int32, (1, tq), 1)
    sd = (_dot(ks_aug[pl.ds(tile_k0, tq), 0:LANES], q_aug)
          + all_heads(jnp.where(key_in_tile <= qry_in_tile, 0.0, NEG_INF)))

    e = jnp.exp2(sc - col_max(sc))
    any_visible = all_heads(jnp.where(t_lane >= CMP_LEN - 1, 1.0, 0.0))
    p_cmp = e * (any_visible / jnp.sum(e, axis=0, keepdims=True))
    o_cmp = _dot(cv_ref[0, 0], p_cmp.astype(BF16))

    p_sum = p_cmp[:, 0:tq]
    for r in range(1, NSA_REP):
        p_sum = p_sum + p_cmp[:, r * tq:(r + 1) * tq]
    ovl = ovl_ref[...].astype(BF16)
    imp = jnp.zeros((n_slc, tq), F32)
    rest = p_sum
    for _ in range(3):
        piece = rest.astype(BF16)
        imp = imp + _dot(ovl, piece)
        rest = rest - piece.astype(F32)

    span = min(WINDOW + tq, s_len)
    w0 = pl.multiple_of(jnp.maximum(q0 + tq - span, 0), tq)
    dist_w = t_lane - (w0 + lax.broadcasted_iota(jnp.int32, (span, 1), 0))
    band = jnp.where(dist_w >= 0, jnp.where(dist_w < WINDOW, 0.0, NEG_INF), NEG_INF)
    sw = _dot(kw_aug[pl.ds(w0, span), :], q_aug) + all_heads(band)

    blk = lax.broadcasted_iota(jnp.int32, (n_slc, tq), 0)
    tl = q0 + lax.broadcasted_iota(jnp.int32, (n_slc, tq), 1)
    cur = _div_pow2(tl, SLC_LEN)
    forced = (blk == 0) | (blk == cur) | (blk == cur - 1)
    val = jnp.where(forced, FORCE_SCORE, jnp.where(blk * SLC_LEN <= tl, imp, -1.0))
    n_grp = n_slc // RANK_GROUP
    val_grp = [val[RANK_GROUP * g:RANK_GROUP * (g + 1)] for g in range(n_grp)]
    row_in_grp = lax.broadcasted_iota(jnp.int32, (RANK_GROUP, tq), 0)
    rank_ref[...] = jnp.zeros_like(rank_ref)
    for mg in range(n_grp):
        @pl.when(mg * RANK_GROUP * SLC_LEN < q0 + tq)
        def _():
            parts = [rank_ref[RANK_GROUP * g:RANK_GROUP * (g + 1), :] for g in range(n_grp)]
            for m in range(RANK_GROUP * mg, RANK_GROUP * (mg + 1)):
                vm = val[m:m + 1, :]
                for g in range(n_grp):
                    if g < mg:
                        beats = jnp.where(vm > val_grp[g], 1.0, 0.0)
                    elif g > mg:
                        beats = jnp.where(vm >= val_grp[g], 1.0, 0.0)
                    else:
                        beats = jnp.where(row_in_grp > m - RANK_GROUP * mg,
                                          jnp.where(vm >= val_grp[g], 1.0, 0.0),
                                          jnp.where(vm > val_grp[g], 1.0, 0.0))
                    parts[g] = parts[g] + beats
            for g in range(n_grp):
                rank_ref[RANK_GROUP * g:RANK_GROUP * (g + 1), :] = parts[g]
    rank = rank_ref[...]

    before_tile = blk * SLC_LEN < q0
    sel_mask = jnp.where(before_tile, jnp.where(rank < top_n, 0.0, NEG_INF), NEG_INF)
    sel_mask = jnp.concatenate([sel_mask, jnp.zeros((LANES - n_slc, tq), F32)], axis=0)
    q_sel = jnp.concatenate([q_aug, all_heads(sel_mask.astype(BF16))], axis=0)

    n_chunks = s_len // SLC_CHUNK
    blk_any = jnp.max(sel_mask[:n_slc], axis=1, keepdims=True)
    blocks_per_chunk = SLC_CHUNK // SLC_LEN
    needed = jnp.int32(0)
    for j in range(n_chunks):
        hit = jnp.max(blk_any[j * blocks_per_chunk:(j + 1) * blocks_per_chunk]) > 0.5 * NEG_INF
        needed = needed | (hit.astype(jnp.int32) << j)

    m0 = col_max(sd)
    pd = jnp.exp2(sd - m0).astype(BF16)
    acc0 = _dot(vs_aug[:, pl.ds(tile_k0, tq)], pd)

    n_needed = jnp.int32(0)
    for j in range(n_chunks):
        chunk_list[n_needed] = jnp.int32(j)
        n_needed = n_needed + ((needed >> j) & 1)

    n_kb = SLC_CHUNK // SLC_KEY_BLOCK

    def chunk_logits(j, kb):
        kk = pl.multiple_of(j * SLC_CHUNK, SLC_CHUNK) + kb * SLC_KEY_BLOCK
        return _dot(ks_aug[pl.ds(kk, SLC_KEY_BLOCK), :], q_sel)

    first = chunk_list[0]
    first_logits = [chunk_logits(first, kb) for kb in range(n_kb)]
    pw = jnp.exp2(sw - col_max(sw)).astype(BF16)
    o_win = normalized(_dot(vw_aug[:, pl.ds(w0, span)], pw))
    m1 = m0
    for kb, s_new in enumerate(first_logits):
        s_buf[kb * SLC_KEY_BLOCK:(kb + 1) * SLC_KEY_BLOCK, :] = s_new
        m1 = jnp.maximum(m1, col_max(s_new))

    def slc_step(c, carry):
        m_prev, m_cur, acc = carry
        acc = jnp.exp2(m_prev - m_cur) * acc
        k0 = pl.multiple_of(chunk_list[c] * SLC_CHUNK, SLC_CHUNK)
        j_next = chunk_list[jnp.minimum(c + 1, n_needed - 1)]
        m_next = m_cur
        for kb in range(n_kb):
            blk_rows = slice(kb * SLC_KEY_BLOCK, (kb + 1) * SLC_KEY_BLOCK)
            s_new = chunk_logits(j_next, kb)
            p = jnp.exp2(s_buf[blk_rows, :] - m_cur).astype(BF16)
            acc = acc + _dot(vs_aug[:, pl.ds(k0 + kb * SLC_KEY_BLOCK, SLC_KEY_BLOCK)], p)
            s_buf[blk_rows, :] = s_new
            m_next = jnp.maximum(m_next, col_max(s_new))
        return m_cur, m_next, acc

    gates = _sigmoid(gate_ref[0]).T

    def gate_rows(br):
        return jnp.concatenate([gates[3 * r + br:3 * r + br + 1, :] for r in range(NSA_REP)], axis=1)

    o_cmp_win = gate_rows(0) * o_cmp + gate_rows(2) * o_win
    gate_slc = gate_rows(1)

    _, _, acc = lax.fori_loop(0, n_needed, slc_step, (m0, m1, acc0))
    o = o_cmp_win + gate_slc * normalized(acc)
    o_ref[...] = jnp.concatenate([o[:, r * tq:(r + 1) * tq] for r in range(NSA_REP)],
                                 axis=0).astype(o_ref.dtype)


def nsa_attention(zqt, zg, ck, cv, zk, zvt, tq=128):
    b, s, _ = zg.shape
    n_cmp = ck.shape[2]
    n_slc = s // SLC_LEN
    top_n = min(SLC_TOP, n_slc)
    assert tq == 2 * SLC_LEN and top_n >= 3 and n_slc <= LANES and s % SLC_CHUNK == 0
    c_lo = jnp.arange(n_cmp) * CMP_STRIDE
    s_lo = jnp.arange(n_slc) * SLC_LEN
    ovl = (jnp.clip(jnp.minimum(c_lo[None, :] + CMP_LEN, s_lo[:, None] + SLC_LEN)
                    - jnp.maximum(c_lo[None, :], s_lo[:, None]), 0, None).astype(F32) / CMP_LEN)
    ovl = ovl * (jnp.arange(n_cmp) < n_cmp - 1)[None, :]
    hd = jnp.arange(1, NSA_HEADS + 1, dtype=F32)
    rest = jnp.exp2(-8.0 * hd / NSA_HEADS) * LOG2E
    pieces = []
    for _ in range(SLOPE_PIECES):
        piece = rest.astype(BF16).astype(F32)
        pieces.append(piece)
        rest = rest - piece
    pieces = jnp.stack(pieces * 2, axis=0).reshape(2 * SLOPE_PIECES, NSA_GROUPS, NSA_REP)
    slope_tab = jnp.zeros((NSA_GROUPS, NSA_HEAD_DIM, NSA_REP, tq), F32)
    slope_tab = slope_tab.at[:, :2 * SLOPE_PIECES].set(
        jnp.transpose(pieces, (1, 0, 2))[..., None]).reshape(NSA_GROUPS, NSA_HEAD_DIM, NSA_REP * tq)
    def k_spec(branch):
        return pl.BlockSpec((1, s, LANES), lambda bi, g, i: (bi, 0, branch * NSA_GROUPS + g))

    def v_spec(branch):
        return pl.BlockSpec((NSA_HEAD_DIM, s), lambda bi, g, i: (branch * NSA_GROUPS + g, bi))

    ck_spec = pl.BlockSpec((1, 1, n_cmp, NSA_HEAD_DIM), lambda bi, g, i: (bi, g, 0, 0))
    cv_spec = pl.BlockSpec((1, 1, NSA_HEAD_DIM, n_cmp), lambda bi, g, i: (bi, g, 0, 0))
    gw = NSA_REP * NSA_HEAD_DIM
    nq = s // tq
    return pl.pallas_call(
        functools.partial(_nsa_kernel, top_n=top_n),
        grid=(b, NSA_GROUPS, s // tq),
        in_specs=[pl.BlockSpec((gw, tq), lambda bi, g, i: (g, bi * nq + i)),
                  pl.BlockSpec((1, tq, LANES), lambda bi, g, i: (bi, i, g)),
                  pl.BlockSpec((1, NSA_HEAD_DIM, NSA_REP * tq), lambda bi, g, i: (g, 0, 0)),
                  ck_spec, cv_spec, k_spec(0), v_spec(0), k_spec(1), v_spec(1),
                  _const_spec(ovl.shape)],
        out_specs=pl.BlockSpec((gw, tq), lambda bi, g, i: (g, bi * nq + i)),
        out_shape=jax.ShapeDtypeStruct((NSA_GROUPS * gw, b * s), BF16),
        scratch_shapes=[pltpu.VMEM((n_cmp, 2 * NSA_HEAD_DIM), BF16),
                        pltpu.VMEM((s, 2 * NSA_HEAD_DIM + LANES), BF16),
                        pltpu.VMEM((s, 2 * NSA_HEAD_DIM), BF16),
                        pltpu.VMEM((NSA_HEAD_DIM + V_PAD_ROWS, s), BF16),
                        pltpu.VMEM((NSA_HEAD_DIM + V_PAD_ROWS, s), BF16),
                        pltpu.VMEM((SLC_CHUNK, NSA_REP * tq), F32),
                        pltpu.VMEM((n_slc, tq), F32),
                        pltpu.SMEM((s // SLC_CHUNK,), jnp.int32)],
        compiler_params=_params("parallel", "parallel", "arbitrary"),
    )(zqt, zg, slope_tab, ck, cv, zk, zvt, zk, zvt, ovl)


def _pad_cols(w, n):
    return jnp.pad(w, ((0, 0), (0, n - w.shape[1])))


def pool_delta_layer(x, ln, w_in, pool_w, pool_scale, conv_w, a_log, dt_bias, o_norm, w_out):
    b, s, d = x.shape
    main = POOL_WIDTH + 4 * DN_WIDTH
    w_all = jnp.concatenate([w_in[:, :main], _pad_cols(w_in[:, main:], LANES)], axis=1).astype(BF16)
    (z,) = norm_matmul(x.reshape(b * s, d), ln, [w_all], [F32])
    z = z.reshape(b, s, main + LANES)
    y_pool = pool_mixer(z, pool_w.astype(BF16), pool_scale)
    y_dn = gated_deltanet(z, main // LANES, conv_w, a_log, dt_bias, o_norm)
    w_out = w_out.astype(BF16)
    return [y_pool, y_dn], [w_out[:POOL_WIDTH], w_out[POOL_WIDTH:]]


def _cmp_weights(pe, w1, w2):
    half = (CMP_LEN // 2) * NSA_HEAD_DIM
    w1_pair = jnp.concatenate([w1[:half], w1[half:]], axis=1).astype(BF16)
    pe_rows = jnp.zeros((8, half), F32).at[0:2].set(pe.reshape(2, half)).astype(BF16)
    return w1_pair, pe_rows, w2.astype(BF16)


def nsa_layer(x, ln, w_in, pe_k, w1_k, w2_k, pe_v, w1_v, w2_v, w_out):
    b, s, d = x.shape
    g_, r_, hd = NSA_GROUPS, NSA_REP, NSA_HEAD_DIM
    kvw = g_ * hd
    main = d + 6 * kvw
    wg = w_in[:, main:].reshape(d, g_, r_ * 3)
    wg = jnp.pad(wg, ((0, 0), (0, 0), (0, LANES - r_ * 3))).reshape(d, g_ * LANES)
    cols = lambda j: w_in[:, d + j * kvw:d + (j + 1) * kvw]

    def lane_block_per_group(w):
        return jnp.pad(w.reshape(d, g_, hd), ((0, 0), (0, 0), (0, LANES - hd))).reshape(d, g_ * LANES)

    w_k = jnp.concatenate([lane_block_per_group(cols(2)), lane_block_per_group(cols(4))], axis=1)
    w_vt = jnp.concatenate([cols(3), cols(5)], axis=1).T
    z, zg, zk, zvt, zqt = norm_matmul(
        x.reshape(b * s, d), ln,
        [w_in[:, d:d + 2 * kvw].astype(BF16), wg.astype(BF16), w_k.astype(BF16)], [BF16, F32, BF16],
        wts=[w_vt.astype(BF16), w_in[:, :d].T.astype(BF16)])
    z = z.reshape(b, s, 2 * kvw)
    zg = zg.reshape(b, s, g_ * LANES)
    zk = zk.reshape(b, s, 2 * g_ * LANES)

    def group_major(j):
        return jnp.transpose(z[..., j * kvw:(j + 1) * kvw].reshape(b, s, g_, hd), (0, 2, 1, 3))

    n_str = s // CMP_STRIDE
    rk = group_major(0).reshape(b, g_, n_str, CMP_STRIDE * hd)
    rv = group_major(1).reshape(b, g_, n_str, CMP_STRIDE * hd)
    ck, cv = compress_kv(rk, rv, *_cmp_weights(pe_k, w1_k, w2_k), *_cmp_weights(pe_v, w1_v, w2_v))
    o_t = nsa_attention(zqt, zg, ck, jnp.swapaxes(cv, 2, 3), zk, zvt)
    return [o_t], [w_out.astype(BF16)]


def kernel(x, mem, a_ln, a_w_in, a_pool_w, a_pool_scale, a_conv_w, a_a_log, a_dt_bias, a_o_norm, a_w_out, c_ln, c_w_in, c_pe_k, c_w1_k, c_w2_k, c_pe_v, c_w1_v, c_w2_v, c_w_out, xa_ln, xa_mem_ln, xa_wq, xa_wk, xa_wv, xa_wo, ff_ln, ff_w1, ff_w2, final_ln):
    b, s, d = x.shape
    depth = xa_ln.shape[0]
    mem2d = mem.reshape(b * mem.shape[1], d)
    for l in range(depth):
        i = l // 2
        if l % 2 == 0:
            mix_a, mix_w = pool_delta_layer(x, a_ln[i], a_w_in[i], a_pool_w[i], a_pool_scale[i],
                                            a_conv_w[i], a_a_log[i], a_dt_bias[i], a_o_norm[i],
                                            a_w_out[i])
        else:
            mix_a, mix_w = nsa_layer(x, c_ln[i], c_w_in[i], c_pe_k[i], c_w1_k[i], c_w2_k[i],
                                     c_pe_v[i], c_w1_v[i], c_w2_v[i], c_w_out[i])
        mk, mv = norm_matmul(mem2d, xa_mem_ln[l], [xa_wk[l].astype(BF16), xa_wv[l].astype(BF16)],
                             [BF16, BF16])
        x = xattn_sublayer(x, mix_a, mix_w, xa_ln[l], xa_wq[l].astype(BF16), mk, mv,
                           xa_wo[l].astype(BF16))
        x = mlp_sublayer(x.reshape(b * s, d), ff_ln[l], ff_w1[l].astype(BF16), ff_w2[l].astype(BF16),
                         final_ln, final_norm=(l == depth - 1)).reshape(b, s, d)
    return x
```

```python
import functools
import math

import jax
import jax.numpy as jnp
from jax import lax
from jax.experimental import pallas as pl
from jax.experimental.pallas import tpu as pltpu

F32 = jnp.float32
BF16 = jnp.bfloat16

LANES = 128
VMEM_LIMIT = 56 * 1024 * 1024

EPS = 1e-6
NEG_INF = -1e30
FORCE_SCORE = 1e4

POOL_WINDOWS = (2, 4, 8, 16)
POOL_GROUP_DIM = 128
POOL_WIDTH = 512
DN_HEADS = 4
DN_HEAD_DIM = 128
DN_WIDTH = 512
DN_CONV = 4
DN_CHUNK = 64

NSA_HEAD_DIM = 64
NSA_GROUPS = 4
NSA_REP = 4
NSA_HEADS = 16
CMP_LEN = 32
CMP_STRIDE = 16
CMP_HIDDEN = 128
SLC_LEN = 64
SLC_TOP = 16
WINDOW = 512
SLC_CHUNK = 512
SLC_KEY_BLOCK = 256
RANK_UNGUARDED_GROUPS = 2
RANK_GROUP = 8
V_PAD_ROWS = 16
LOG2E = 1.4426950408889634
SLOPE_PIECES = 3
POS_SPLIT = 64

XA_HEADS = 4
XA_HEAD_DIM = 256


def _params(*sem):
    return pltpu.CompilerParams(dimension_semantics=sem, vmem_limit_bytes=VMEM_LIMIT)


def _const_spec(shape):
    nd = len(shape)
    return pl.BlockSpec(shape, lambda *_: (0,) * nd)


def _rms(x, g):
    return x * lax.rsqrt(jnp.mean(x * x, axis=-1, keepdims=True) + EPS) * g


def _sigmoid(x):
    return 1.0 / (1.0 + jnp.exp(-x))


def _div_pow2(x, n):
    return lax.shift_right_logical(x, jnp.int32(int(math.log2(n))))


def _dot(a, b, precision=None):
    return jnp.dot(a, b, preferred_element_type=F32, precision=precision)


def _dot_nt(a, b, precision=None):
    return lax.dot_general(a, b, (((1,), (1,)), ((), ())), preferred_element_type=F32,
                           precision=precision)


def _dot_tn(a, b):
    return lax.dot_general(a, b, (((0,), (0,)), ((), ())), preferred_element_type=F32)


def _norm_matmul_kernel(x_ref, g_ref, *refs, n_plain):
    n = len(refs) // 2
    hb = _rms(x_ref[...], g_ref[...]).astype(BF16)
    for j, (w_ref, o_ref) in enumerate(zip(refs[:n], refs[n:])):
        if j < n_plain:
            o_ref[...] = _dot(hb, w_ref[...]).astype(o_ref.dtype)
        else:
            o_ref[...] = _dot_nt(w_ref[...], hb).astype(o_ref.dtype)


def norm_matmul(x2d, g, ws, out_dtypes, wts=(), tm=512):
    t, d = x2d.shape
    tm = min(tm, t)
    return pl.pallas_call(
        functools.partial(_norm_matmul_kernel, n_plain=len(ws)),
        grid=(t // tm,),
        in_specs=[pl.BlockSpec((tm, d), lambda i: (i, 0)), _const_spec((1, d))]
        + [_const_spec(w.shape) for w in (*ws, *wts)],
        out_specs=[pl.BlockSpec((tm, w.shape[1]), lambda i: (i, 0)) for w in ws]
        + [pl.BlockSpec((w.shape[0], tm), lambda i: (0, i)) for w in wts],
        out_shape=[jax.ShapeDtypeStruct((t, w.shape[1]), dt) for w, dt in zip(ws, out_dtypes)]
        + [jax.ShapeDtypeStruct((w.shape[0], t), BF16) for w in wts],
        compiler_params=_params("parallel"),
    )(x2d, g.reshape(1, d).astype(F32), *ws, *wts)


def _mlp_kernel(x_ref, g_ref, w1_ref, w2_ref, gf_ref, o_ref, *, ff_chunk, final_norm):
    x = x_ref[...]
    hb = _rms(x, g_ref[...]).astype(BF16)
    acc = x
    for c in range(w1_ref.shape[1] // ff_chunk):
        a = jnp.maximum(_dot(hb, w1_ref[:, c * ff_chunk:(c + 1) * ff_chunk]), 0.0)
        acc = acc + _dot((a * a).astype(BF16), w2_ref[c * ff_chunk:(c + 1) * ff_chunk, :])
    if final_norm:
        acc = _rms(acc, gf_ref[...])
    o_ref[...] = acc


def mlp_sublayer(x2d, g, w1, w2, gf, final_norm, tm=512, ff_chunk=1024):
    t, d = x2d.shape
    tm = min(tm, t)
    return pl.pallas_call(
        functools.partial(_mlp_kernel, ff_chunk=ff_chunk, final_norm=final_norm),
        grid=(t // tm,),
        in_specs=[pl.BlockSpec((tm, d), lambda i: (i, 0)), _const_spec((1, d)),
                  _const_spec(w1.shape), _const_spec(w2.shape), _const_spec((1, d))],
        out_specs=pl.BlockSpec((tm, d), lambda i: (i, 0)),
        out_shape=jax.ShapeDtypeStruct((t, d), F32),
        compiler_params=_params("parallel"),
    )(x2d, g.reshape(1, d).astype(F32), w1, w2, gf.reshape(1, d).astype(F32))


def _xattn_kernel(x_ref, *refs, mix_transposed):
    n_mix = len(mix_transposed)
    g_ref, wq_ref, k_ref, v_ref, wo_ref, o_ref = refs[2 * n_mix:]
    x = x_ref[0]
    for a_ref, w_ref, transposed in zip(refs[:n_mix], refs[n_mix:2 * n_mix], mix_transposed):
        x = x + (_dot_tn(a_ref[...], w_ref[...]) if transposed else _dot(a_ref[0], w_ref[...]))
    hb = _rms(x, g_ref[...]).astype(BF16)
    q = (_dot(hb, wq_ref[...]) * (XA_HEAD_DIM ** -0.5)).astype(BF16)
    heads = []
    for h in range(XA_HEADS):
        sl = slice(h * XA_HEAD_DIM, (h + 1) * XA_HEAD_DIM)
        s = _dot_nt(q[:, sl], k_ref[:, sl])
        p = jnp.exp(s - jnp.max(s, axis=-1, keepdims=True))
        l = jnp.sum(p, axis=-1, keepdims=True)
        heads.append((_dot(p.astype(BF16), v_ref[:, sl]) / l).astype(BF16))
    o_ref[0] = x + _dot(jnp.concatenate(heads, axis=1), wo_ref[...])


def xattn_sublayer(x, mix_a, mix_w, g, wq, k2d, v2d, wo, tq=512):
    b, s, d = x.shape
    tq = min(tq, s)
    m = k2d.shape[0] // b
    nq = s // tq

    def mix_spec(a):
        if a.ndim == 2:
            return pl.BlockSpec((a.shape[0], tq), lambda bi, i: (0, bi * nq + i))
        return pl.BlockSpec((1, tq, a.shape[2]), lambda bi, i: (bi, i, 0))

    return pl.pallas_call(
        functools.partial(_xattn_kernel, mix_transposed=tuple(a.ndim == 2 for a in mix_a)),
        grid=(b, nq),
        in_specs=[pl.BlockSpec((1, tq, d), lambda bi, i: (bi, i, 0))]
        + [mix_spec(a) for a in mix_a]
        + [_const_spec(w.shape) for w in mix_w]
        + [_const_spec((1, d)), _const_spec(wq.shape),
           pl.BlockSpec((m, d), lambda bi, i: (bi, 0)),
           pl.BlockSpec((m, d), lambda bi, i: (bi, 0)),
           _const_spec(wo.shape)],
        out_specs=pl.BlockSpec((1, tq, d), lambda bi, i: (bi, i, 0)),
        out_shape=jax.ShapeDtypeStruct((b, s, d), F32),
        compiler_params=_params("parallel", "parallel"),
    )(x, *mix_a, *mix_w, g.reshape(1, d).astype(F32), wq, k2d, v2d, wo)


POOL_HALO = 16


def _pool_kernel(u_ref, halo_ref, w_ref, scale_ref, o_ref):
    i = pl.program_id(1)
    ts = u_ref.shape[1]
    u = u_ref[0]
    halo = jnp.where(i == 0, 0.0, halo_ref[0])
    ext = jnp.concatenate([halo, u], axis=0)
    sums = [None] * len(POOL_WINDOWS)
    cur = ext
    for gi, win in enumerate(POOL_WINDOWS):
        cur = cur[:, (POOL_GROUP_DIM if gi else 0):]
        cur = cur + pltpu.roll(cur, win // 2, axis=0)
        sums[gi] = cur[POOL_HALO:, :POOL_GROUP_DIM]
    pos1 = (i * ts + 1 + lax.broadcasted_iota(jnp.int32, (ts, 1), 0)).astype(F32)
    outs = []
    for gi, win in enumerate(POOL_WINDOWS):
        ug = u[:, gi * POOL_GROUP_DIM:(gi + 1) * POOL_GROUP_DIM]
        y = sums[gi] / jnp.minimum(pos1, float(win)) - ug
        outs.append(_dot(y.astype(BF16), w_ref[gi]))
    o_ref[0] = (jnp.concatenate(outs, axis=1) * scale_ref[...]).astype(o_ref.dtype)


def pool_mixer(z, pool_w, pool_scale, ts=512):
    b, s, _ = z.shape
    ts = min(ts, s)
    hb = ts // POOL_HALO
    return pl.pallas_call(
        _pool_kernel,
        grid=(b, s // ts),
        in_specs=[pl.BlockSpec((1, ts, POOL_WIDTH), lambda bi, i: (bi, i, 0)),
                  pl.BlockSpec((1, POOL_HALO, POOL_WIDTH),
                               lambda bi, i: (bi, jnp.maximum(i * hb - 1, 0), 0)),
                  _const_spec(pool_w.shape), _const_spec((1, POOL_WIDTH))],
        out_specs=pl.BlockSpec((1, ts, POOL_WIDTH), lambda bi, i: (bi, i, 0)),
        out_shape=jax.ShapeDtypeStruct((b, s, POOL_WIDTH), BF16),
        compiler_params=_params("parallel", "parallel"),
    )(z, z, pool_w, pool_scale.reshape(1, POOL_WIDTH).astype(F32))


DN_ROWS = DN_HEADS * DN_CHUNK
DN_CHUNKS_PER_STEP = 8
DN_GROUP = 4
BETA_LANE = 0
ALPHA_LANE = DN_HEADS


def _stack_heads(x):
    return jnp.concatenate([x[:, h * DN_HEAD_DIM:(h + 1) * DN_HEAD_DIM] for h in range(DN_HEADS)],
                           axis=0)


def _stack_cols(x, lane0):
    return jnp.concatenate([x[:, lane0 + h:lane0 + h + 1] for h in range(DN_HEADS)], axis=0)


def _pick_head_block(wide, row_head):
    out = jnp.zeros((DN_ROWS, DN_HEAD_DIM), F32)
    for h in range(DN_HEADS):
        out = jnp.where(row_head == h, wide[:, h * DN_HEAD_DIM:(h + 1) * DN_HEAD_DIM], out)
    return out


def _deltanet_kernel(q_ref, k_ref, v_ref, gate_ref, ba_ref, cw_ref, alog_ref, dtb_ref, onorm_ref,
                     o_ref, state_ref, tail_ref):
    c = pl.program_id(1)

    @pl.when(c == 0)
    def _():
        state_ref[...] = jnp.zeros_like(state_ref)
        tail_ref[...] = jnp.zeros_like(tail_ref)

    blk_len = q_ref.shape[1]
    x3 = jnp.concatenate([q_ref[0], k_ref[0], v_ref[0]], axis=1)
    ext = jnp.concatenate([tail_ref[...], x3], axis=0)
    tail_ref[...] = x3[blk_len - 8:, :]
    cw = cw_ref[...]
    y = cw[DN_CONV - 1:DN_CONV, :] * ext
    for j in range(1, DN_CONV):
        y = y + cw[DN_CONV - 1 - j:DN_CONV - j, :] * pltpu.roll(ext, j, axis=0)
    y = y[8:, :]
    y = y * _sigmoid(y)

    def l2n(a):
        return a * lax.rsqrt(jnp.sum(a * a, axis=-1, keepdims=True) + EPS)

    ba = ba_ref[0]
    beta_all = _sigmoid(ba)
    sp_in = ba + dtb_ref[...]
    softplus = jnp.maximum(sp_in, 0.0) + jnp.log(1.0 + jnp.exp(-jnp.abs(sp_in)))
    g_all = -jnp.exp(alog_ref[...]) * softplus
    ri = lax.broadcasted_iota(jnp.int32, (DN_CHUNK, DN_CHUNK), 0)
    ci = lax.broadcasted_iota(jnp.int32, (DN_CHUNK, DN_CHUNK), 1)
    tril_ones = jnp.where(ri >= ci, 1.0, 0.0).astype(BF16)

    def chunk_cumsum(g):
        total = jnp.zeros_like(g)
        rest = g
        for _ in range(3):
            piece = rest.astype(BF16)
            total = total + _dot(tril_ones, piece)
            rest = rest - piece.astype(F32)
        return total

    rr = lax.broadcasted_iota(jnp.int32, (DN_ROWS, DN_ROWS), 0)
    cc = lax.broadcasted_iota(jnp.int32, (DN_ROWS, DN_ROWS), 1)
    same_head = _div_pow2(rr, DN_CHUNK) == _div_pow2(cc, DN_CHUNK)
    causal = same_head & (rr >= cc)
    strict = same_head & (rr > cc)
    lane_head = _div_pow2(lax.broadcasted_iota(jnp.int32, (DN_CHUNK, DN_ROWS), 1), DN_CHUNK)

    def block_diag(w):
        return jnp.concatenate([jnp.where(lane_head == h, w, 0.0) for h in range(DN_HEADS)], axis=0)

    def hi_lo(x):
        hi = x.astype(BF16).astype(F32)
        return hi, x - hi

    def times_p(x, p_wide):
        p_hi, p_lo = hi_lo(p_wide)
        d_hi, d_lo = block_diag(p_hi).astype(BF16), block_diag(p_lo).astype(BF16)
        x_hi, x_lo = hi_lo(x)
        n = x.shape[0]
        top = _dot(jnp.concatenate([x_hi, x_lo], axis=0).astype(BF16), d_hi)
        return top[:n] + top[n:] + _dot(x_hi.astype(BF16), d_lo)

    wr = lax.broadcasted_iota(jnp.int32, (DN_CHUNK, DN_ROWS), 0)
    wc = lax.broadcasted_iota(jnp.int32, (DN_CHUNK, DN_ROWS), 1)
    eye_wide = jnp.where(wr == (wc & (DN_CHUNK - 1)), 1.0, 0.0)
    n_sq = int(math.log2(DN_CHUNK)) - 1

    def chunk_prep(ci):
        rows = slice(ci * DN_CHUNK, (ci + 1) * DN_CHUNK)
        yc = y[rows, :]
        q_st = l2n(_stack_heads(yc[:, :DN_WIDTH])) * (DN_HEAD_DIM ** -0.5)
        k_st = l2n(_stack_heads(yc[:, DN_WIDTH:2 * DN_WIDTH]))
        v_st = _stack_heads(yc[:, 2 * DN_WIDTH:])
        gc_all = chunk_cumsum(g_all[rows, :])
        beta_st = _stack_cols(beta_all[rows, :], BETA_LANE)
        gc_st = _stack_cols(gc_all, ALPHA_LANE)
        g_last = [gc_all[DN_CHUNK - 1:DN_CHUNK, ALPHA_LANE + h:ALPHA_LANE + h + 1]
                  for h in range(DN_HEADS)]
        glast_st = jnp.concatenate([jnp.broadcast_to(g, (DN_CHUNK, 1)) for g in g_last], axis=0)
        gcb = jnp.broadcast_to(gc_st, (DN_ROWS, DN_ROWS))
        decay = jnp.where(causal, jnp.exp(jnp.where(causal, gcb - gcb.T, 0.0)), 0.0)
        kb_st = k_st * beta_st
        k_bf = k_st.astype(BF16)
        a_low = jnp.where(strict, _dot_nt(kb_st.astype(BF16), k_bf) * decay, 0.0)
        p = -(a_low[0:DN_CHUNK] + a_low[DN_CHUNK:2 * DN_CHUNK]
              + a_low[2 * DN_CHUNK:3 * DN_CHUNK] + a_low[3 * DN_CHUNK:])
        t_wide = eye_wide + p
        yield None
        for j in range(n_sq):
            if j == 0:
                p = times_p(p, p)
            else:
                both = times_p(jnp.concatenate([p, t_wide], axis=0), p)
                p, t_wide = both[:DN_CHUNK], t_wide + both[DN_CHUNK:]
            yield None
        t_wide = t_wide + times_p(t_wide, p)
        t_bf = block_diag(t_wide).astype(BF16)
        egc = jnp.exp(gc_st)
        wu = _dot(t_bf, jnp.concatenate([(kb_st * egc).astype(BF16),
                                         (v_st * beta_st).astype(BF16)], axis=1))
        attn = jnp.where(causal, _dot_nt(q_st.astype(BF16), k_bf) * decay, 0.0)
        yield dict(
            w=wu[:, :DN_HEAD_DIM].astype(BF16), u=wu[:, DN_HEAD_DIM:], attn=attn.astype(BF16),
            q_dec=(q_st * egc).astype(BF16), k_dec=(k_st * jnp.exp(glast_st - gc_st)).astype(BF16),
            state_scale=jnp.concatenate(
                [jnp.broadcast_to(jnp.exp(g), (1, DN_HEAD_DIM)) for g in g_last], axis=1))

    row_head = _div_pow2(lax.broadcasted_iota(jnp.int32, (DN_ROWS, 1), 0), DN_CHUNK)
    gate = gate_ref[0]
    n_chunks = blk_len // DN_CHUNK
    carry = {"state": state_ref[...]}
    prepared = [None] * n_chunks
    out_rows = [None] * n_chunks

    def recurrence(chunks):
        for ci in chunks:
            pr = prepared[ci]
            s_bf = carry["state"].astype(BF16)
            w_s = _dot(pr["w"], s_bf)
            q_s = _dot(pr["q_dec"], s_bf)
            yield
            v_new_bf = (pr["u"] - _pick_head_block(w_s, row_head)).astype(BF16)
            zero = jnp.zeros_like(v_new_bf)
            v_wide = jnp.concatenate(
                [jnp.where(row_head == h, v_new_bf, zero) for h in range(DN_HEADS)], axis=1)
            carry["state"] = carry["state"] * pr["state_scale"] + _dot_tn(pr["k_dec"], v_wide)
            o_st = _pick_head_block(q_s, row_head) + _dot(pr["attn"], v_new_bf)
            yield
            outs = []
            for h in range(DN_HEADS):
                o_h = _rms(o_st[h * DN_CHUNK:(h + 1) * DN_CHUNK, :], onorm_ref[...])
                g_h = gate[ci * DN_CHUNK:(ci + 1) * DN_CHUNK, h * DN_HEAD_DIM:(h + 1) * DN_HEAD_DIM]
                outs.append(o_h * (g_h * _sigmoid(g_h)))
            out_rows[ci] = jnp.concatenate(outs, axis=1)
            yield

    def advance(gen, n):
        for _ in range(n):
            next(gen, None)

    groups = [list(range(g0, min(g0 + DN_GROUP, n_chunks))) for g0 in range(0, n_chunks, DN_GROUP)]
    pending = iter(())
    for group in groups:
        preps = [chunk_prep(ci) for ci in group]
        rec_pieces = 3 * DN_GROUP
        for stage in range(n_sq + 2):
            for ci, prep in zip(group, preps):
                result = next(prep)
                if result is not None:
                    prepared[ci] = result
            advance(pending, -(-rec_pieces // (n_sq + 2)))
        advance(pending, rec_pieces)
        pending = recurrence(group)
    advance(pending, 3 * DN_GROUP)
    state_ref[...] = carry["state"]
    o_ref[0] = jnp.concatenate(out_rows, axis=0).astype(o_ref.dtype)


def gated_deltanet(z, ba_block, conv_w, a_log, dt_bias, o_norm):
    b, s, _ = z.shape
    lane_row = lambda vals, lane0: jnp.zeros((1, LANES), F32).at[0, lane0:lane0 + DN_HEADS].set(
        vals.astype(F32))
    blk_len = min(DN_CHUNKS_PER_STEP * DN_CHUNK, s)
    col = lambda j: pl.BlockSpec((1, blk_len, DN_WIDTH), lambda bi, ci: (bi, ci, j))
    return pl.pallas_call(
        _deltanet_kernel,
        grid=(b, s // blk_len),
        in_specs=[col(1), col(2), col(3), col(4),
                  pl.BlockSpec((1, blk_len, LANES), lambda bi, ci: (bi, ci, ba_block)),
                  _const_spec(conv_w.shape), _const_spec((1, LANES)), _const_spec((1, LANES)),
                  _const_spec((1, DN_HEAD_DIM))],
        out_specs=pl.BlockSpec((1, blk_len, DN_WIDTH), lambda bi, ci: (bi, ci, 0)),
        out_shape=jax.ShapeDtypeStruct((b, s, DN_WIDTH), BF16),
        scratch_shapes=[pltpu.VMEM((DN_HEAD_DIM, DN_WIDTH), F32),
                        pltpu.VMEM((8, 3 * DN_WIDTH), F32)],
        compiler_params=_params("parallel", "arbitrary"),
    )(z, z, z, z, z, conv_w.astype(F32), lane_row(a_log, ALPHA_LANE), lane_row(dt_bias, ALPHA_LANE),
      o_norm.reshape(1, DN_HEAD_DIM).astype(F32))


def _compress_kernel(rk_ref, rv_ref, w1k_ref, pek_ref, w2k_ref, w1v_ref, pev_ref, w2v_ref,
                     ck_ref, cv_ref):
    n = rk_ref.shape[2]
    row = lax.broadcasted_iota(jnp.int32, (n, 1), 0)

    def one(r_ref, w1_ref, pe_ref, w2_ref, o_ref):
        w1 = w1_ref[...]
        pb = _dot(pe_ref[...], w1)
        bias = pb[0:1, :CMP_HIDDEN] + pb[1:2, CMP_HIDDEN:]
        for g in range(NSA_GROUPS):
            y = _dot(r_ref[0, g], w1)
            h = y[:, :CMP_HIDDEN] + pltpu.roll(y[:, CMP_HIDDEN:], n - 1, axis=0) + bias
            a = (h * _sigmoid(h)).astype(BF16)
            o_ref[0, g] = jnp.where(row < n - 1, _dot(a, w2_ref[...]), 0.0).astype(o_ref.dtype)

    one(rk_ref, w1k_ref, pek_ref, w2k_ref, ck_ref)
    one(rv_ref, w1v_ref, pev_ref, w2v_ref, cv_ref)


def compress_kv(rk, rv, w1k, pek, w2k, w1v, pev, w2v):
    b, g, n, w = rk.shape
    blk = pl.BlockSpec((1, g, n, w), lambda bi: (bi, 0, 0, 0))
    oblk = pl.BlockSpec((1, g, n, NSA_HEAD_DIM), lambda bi: (bi, 0, 0, 0))
    return pl.pallas_call(
        _compress_kernel,
        grid=(b,),
        in_specs=[blk, blk, _const_spec(w1k.shape), _const_spec(pek.shape), _const_spec(w2k.shape),
                  _const_spec(w1v.shape), _const_spec(pev.shape), _const_spec(w2v.shape)],
        out_specs=[oblk, oblk],
        out_shape=[jax.ShapeDtypeStruct((b, g, n, NSA_HEAD_DIM), BF16)] * 2,
        compiler_params=_params("parallel"),
    )(rk, rv, w1k, pek, w2k, w1v, pev, w2v)


def _aug_keys(k, pos):
    n = k.shape[0]
    lane = lax.broadcasted_iota(jnp.int32, (n, NSA_HEAD_DIM), 1)
    hi = (_div_pow2(pos, POS_SPLIT) * POS_SPLIT).astype(F32)
    lo = (pos & (POS_SPLIT - 1)).astype(F32)
    cols = jnp.where(lane < SLOPE_PIECES, hi, jnp.where(lane < 2 * SLOPE_PIECES, lo, 0.0))
    return jnp.concatenate([k[:, :NSA_HEAD_DIM], cols.astype(BF16)], axis=1)


def _nsa_kernel(q_ref, gate_ref, slope_ref, ck_ref, cv_ref, ks_ref, vs_ref, kw_ref, vw_ref,
                ovl_ref, o_ref, kc_aug, ks_aug, kw_aug, vs_aug, vw_aug, s_buf, rank_ref, chunk_list,
                *, top_n):
    i = pl.program_id(2)
    tq = q_ref.shape[1]
    rows = NSA_REP * tq
    n_cmp = ck_ref.shape[2]
    assert q_ref.shape[0] == NSA_REP * NSA_HEAD_DIM
    n_slc = ovl_ref.shape[0]
    s_len = ks_ref.shape[1]
    q0 = i * tq

    @pl.when(i == 0)
    def _():
        cpos = lax.broadcasted_iota(jnp.int32, (n_cmp, 1), 0) * CMP_STRIDE + (CMP_LEN - 1)
        kc_aug[...] = _aug_keys(ck_ref[0, 0], cpos)

        def fill(c, carry):
            r0 = pl.multiple_of(c * SLC_CHUNK, SLC_CHUNK)
            pos = r0 + lax.broadcasted_iota(jnp.int32, (SLC_CHUNK, 1), 0)
            blk_lane = lax.broadcasted_iota(jnp.int32, (SLC_CHUNK, LANES), 1)
            onehot = jnp.where(blk_lane == _div_pow2(pos, SLC_LEN), 1.0, 0.0).astype(BF16)
            ks_aug[pl.ds(r0, SLC_CHUNK), :] = jnp.concatenate(
                [_aug_keys(ks_ref[0, pl.ds(r0, SLC_CHUNK), :], pos), onehot], axis=1)
            kw_aug[pl.ds(r0, SLC_CHUNK), :] = _aug_keys(kw_ref[0, pl.ds(r0, SLC_CHUNK), :], pos)
            return carry

        lax.fori_loop(0, s_len // SLC_CHUNK, fill, 0)
        ones_rows = jnp.where(lax.broadcasted_iota(jnp.int32, (V_PAD_ROWS, s_len), 0) == 0, 1.0, 0.0)
        for v_ref, v_aug in ((vs_ref, vs_aug), (vw_ref, vw_aug)):
            v_aug[0:NSA_HEAD_DIM, :] = v_ref[...]
            v_aug[NSA_HEAD_DIM:, :] = ones_rows.astype(BF16)

    q_t = q_ref[...].astype(F32) * (NSA_HEAD_DIM ** -0.5 * LOG2E)
    q_t = jnp.concatenate([q_t[r * NSA_HEAD_DIM:(r + 1) * NSA_HEAD_DIM, :] for r in range(NSA_REP)],
                          axis=1).astype(BF16)
    q_aug = jnp.concatenate([q_t, slope_ref[0].astype(BF16)], axis=0)
    t_lane = q0 + lax.broadcasted_iota(jnp.int32, (1, tq), 1)

    def all_heads(x):
        return jnp.concatenate([x] * NSA_REP, axis=1)

    def col_max(x):
        return jnp.max(x, axis=0, keepdims=True)

    def normalized(acc):
        return acc[:NSA_HEAD_DIM] / acc[NSA_HEAD_DIM:NSA_HEAD_DIM + 1]

    cend = lax.broadcasted_iota(jnp.int32, (n_cmp, 1), 0) * CMP_STRIDE + (CMP_LEN - 1)
    sc = _dot(kc_aug[...], q_aug)

    tile_k0 = pl.multiple_of(q0, tq)
    key_in_tile = lax.broadcasted_iota(jnp.int32, (tq, 1), 0)
    qry_in_tile = lax.broadcasted_iota(jnp.int32, (1, tq), 1)
    sd = _dot(ks_aug[pl.ds(tile_k0, tq), 0:LANES], q_aug)

    span = min(WINDOW + tq, s_len)
    w0 = pl.multiple_of(jnp.maximum(q0 + tq - span, 0), tq)
    dist_w = t_lane - (w0 + lax.broadcasted_iota(jnp.int32, (span, 1), 0))
    band = all_heads(jnp.where(dist_w >= 0, jnp.where(dist_w < WINDOW, 0.0, NEG_INF), NEG_INF))

    sc = sc + all_heads(jnp.where(cend <= t_lane, 0.0, NEG_INF))
    sd = sd + all_heads(jnp.where(key_in_tile <= qry_in_tile, 0.0, NEG_INF))

    e = jnp.exp2(sc - col_max(sc))
    any_visible = all_heads(jnp.where(t_lane >= CMP_LEN - 1, 1.0, 0.0))
    p_cmp = e * (any_visible / jnp.sum(e, axis=0, keepdims=True))
    o_cmp = _dot(cv_ref[0, 0], p_cmp.astype(BF16))

    p_sum = p_cmp[:, 0:tq]
    for r in range(1, NSA_REP):
        p_sum = p_sum + p_cmp[:, r * tq:(r + 1) * tq]
    ovl = ovl_ref[...].astype(BF16)
    imp = jnp.zeros((n_slc, tq), F32)
    rest = p_sum
    for _ in range(3):
        piece = rest.astype(BF16)
        imp = imp + _dot(ovl, piece)
        rest = rest - piece.astype(F32)

    sw = _dot(kw_aug[pl.ds(w0, span), :], q_aug) + band

    blk = lax.broadcasted_iota(jnp.int32, (n_slc, tq), 0)
    tl = q0 + lax.broadcasted_iota(jnp.int32, (n_slc, tq), 1)
    cur = _div_pow2(tl, SLC_LEN)
    forced = (blk == 0) | (blk == cur) | (blk == cur - 1)
    val = jnp.where(forced, FORCE_SCORE, jnp.where(blk * SLC_LEN <= tl, imp, -1.0))
    n_grp = n_slc // RANK_GROUP
    val_grp = [val[RANK_GROUP * g:RANK_GROUP * (g + 1)] for g in range(n_grp)]
    row_in_grp = lax.broadcasted_iota(jnp.int32, (RANK_GROUP, tq), 0)
    def count_group(mg, parts):
        for m in range(RANK_GROUP * mg, RANK_GROUP * (mg + 1)):
            vm = val[m:m + 1, :]
            for g in range(n_grp):
                if g < mg:
                    beats = jnp.where(vm > val_grp[g], 1.0, 0.0)
                elif g > mg:
                    beats = jnp.where(vm >= val_grp[g], 1.0, 0.0)
                else:
                    beats = jnp.where(row_in_grp > m - RANK_GROUP * mg,
                                      jnp.where(vm >= val_grp[g], 1.0, 0.0),
                                      jnp.where(vm > val_grp[g], 1.0, 0.0))
                parts[g] = parts[g] + beats
        return parts

    parts = [jnp.zeros((RANK_GROUP, tq), F32) for _ in range(n_grp)]
    for mg in range(min(RANK_UNGUARDED_GROUPS, n_grp)):
        parts = count_group(mg, parts)
    for g in range(n_grp):
        rank_ref[RANK_GROUP * g:RANK_GROUP * (g + 1), :] = parts[g]
    for mg in range(min(RANK_UNGUARDED_GROUPS, n_grp), n_grp):
        @pl.when(mg * RANK_GROUP * SLC_LEN < q0 + tq)
        def _():
            parts = count_group(
                mg, [rank_ref[RANK_GROUP * g:RANK_GROUP * (g + 1), :] for g in range(n_grp)])
            for g in range(n_grp):
                rank_ref[RANK_GROUP * g:RANK_GROUP * (g + 1), :] = parts[g]
    rank = rank_ref[...]

    before_tile = blk * SLC_LEN < q0
    sel_mask = jnp.where(before_tile, jnp.where(rank < top_n, 0.0, NEG_INF), NEG_INF)
    sel_mask = jnp.concatenate([sel_mask, jnp.zeros((LANES - n_slc, tq), F32)], axis=0)
    q_sel = jnp.concatenate([q_aug, all_heads(sel_mask.astype(BF16))], axis=0)

    n_chunks = s_len // SLC_CHUNK
    blk_any = jnp.max(sel_mask[:n_slc], axis=1, keepdims=True)
    blocks_per_chunk = SLC_CHUNK // SLC_LEN
    needed = jnp.int32(0)
    for j in range(n_chunks):
        hit = jnp.max(blk_any[j * blocks_per_chunk:(j + 1) * blocks_per_chunk]) > 0.5 * NEG_INF
        needed = needed | (hit.astype(jnp.int32) << j)

    m0 = col_max(sd)
    pd = jnp.exp2(sd - m0).astype(BF16)
    acc0 = _dot(vs_aug[:, pl.ds(tile_k0, tq)], pd)

    n_needed = jnp.int32(0)
    for j in range(n_chunks):
        chunk_list[n_needed] = jnp.int32(j)
        n_needed = n_needed + ((needed >> j) & 1)

    n_kb = SLC_CHUNK // SLC_KEY_BLOCK

    def chunk_logits(j, kb):
        kk = pl.multiple_of(j * SLC_CHUNK, SLC_CHUNK) + kb * SLC_KEY_BLOCK
        return _dot(ks_aug[pl.ds(kk, SLC_KEY_BLOCK), :], q_sel)

    first = chunk_list[0]
    first_logits = [chunk_logits(first, kb) for kb in range(n_kb)]
    pw = jnp.exp2(sw - col_max(sw)).astype(BF16)
    acc_win = _dot(vw_aug[:, pl.ds(w0, span)], pw)
    m1 = m0
    for kb, s_new in enumerate(first_logits):
        s_buf[kb * SLC_KEY_BLOCK:(kb + 1) * SLC_KEY_BLOCK, :] = s_new
        m1 = jnp.maximum(m1, col_max(s_new))

    def slc_step(c, carry):
        m_prev, m_cur, acc = carry
        acc = jnp.exp2(m_prev - m_cur) * acc
        k0 = pl.multiple_of(chunk_list[c] * SLC_CHUNK, SLC_CHUNK)
        j_next = chunk_list[jnp.minimum(c + 1, n_needed - 1)]
        m_next = m_cur
        for kb in range(n_kb):
            blk_rows = slice(kb * SLC_KEY_BLOCK, (kb + 1) * SLC_KEY_BLOCK)
            s_new = chunk_logits(j_next, kb)
            p = jnp.exp2(s_buf[blk_rows, :] - m_cur).astype(BF16)
            acc = acc + _dot(vs_aug[:, pl.ds(k0 + kb * SLC_KEY_BLOCK, SLC_KEY_BLOCK)], p)
            s_buf[blk_rows, :] = s_new
            m_next = jnp.maximum(m_next, col_max(s_new))
        return m_cur, m_next, acc

    gates = _sigmoid(gate_ref[0]).T

    def gate_rows(br):
        return jnp.concatenate([gates[3 * r + br:3 * r + br + 1, :] for r in range(NSA_REP)], axis=1)

    gated_cmp = gate_rows(0) * o_cmp
    gate_slc, gate_win = gate_rows(1), gate_rows(2)

    _, _, acc = lax.fori_loop(0, n_needed, slc_step, (m0, m1, acc0))
    o = gated_cmp + gate_win * normalized(acc_win) + gate_slc * normalized(acc)
    o_ref[...] = jnp.concatenate([o[:, r * tq:(r + 1) * tq] for r in range(NSA_REP)],
                                 axis=0).astype(o_ref.dtype)


def nsa_attention(zqt, zg, ck, cv, zk, zvt, tq=128):
    b, s, _ = zg.shape
    n_cmp = ck.shape[2]
    n_slc = s // SLC_LEN
    top_n = min(SLC_TOP, n_slc)
    assert tq == 2 * SLC_LEN and top_n >= 3 and n_slc <= LANES and s % SLC_CHUNK == 0
    c_lo = jnp.arange(n_cmp) * CMP_STRIDE
    s_lo = jnp.arange(n_slc) * SLC_LEN
    ovl = (jnp.clip(jnp.minimum(c_lo[None, :] + CMP_LEN, s_lo[:, None] + SLC_LEN)
                    - jnp.maximum(c_lo[None, :], s_lo[:, None]), 0, None).astype(F32) / CMP_LEN)
    ovl = ovl * (jnp.arange(n_cmp) < n_cmp - 1)[None, :]
    hd = jnp.arange(1, NSA_HEADS + 1, dtype=F32)
    rest = jnp.exp2(-8.0 * hd / NSA_HEADS) * LOG2E
    pieces = []
    for _ in range(SLOPE_PIECES):
        piece = rest.astype(BF16).astype(F32)
        pieces.append(piece)
        rest = rest - piece
    pieces = jnp.stack(pieces * 2, axis=0).reshape(2 * SLOPE_PIECES, NSA_GROUPS, NSA_REP)
    slope_tab = jnp.zeros((NSA_GROUPS, NSA_HEAD_DIM, NSA_REP, tq), F32)
    slope_tab = slope_tab.at[:, :2 * SLOPE_PIECES].set(
        jnp.transpose(pieces, (1, 0, 2))[..., None]).reshape(NSA_GROUPS, NSA_HEAD_DIM, NSA_REP * tq)
    def k_spec(branch):
        return pl.BlockSpec((1, s, LANES), lambda bi, g, i: (bi, 0, branch * NSA_GROUPS + g))

    def v_spec(branch):
        return pl.BlockSpec((NSA_HEAD_DIM, s), lambda bi, g, i: (branch * NSA_GROUPS + g, bi))

    ck_spec = pl.BlockSpec((1, 1, n_cmp, NSA_HEAD_DIM), lambda bi, g, i: (bi, g, 0, 0))
    cv_spec = pl.BlockSpec((1, 1, NSA_HEAD_DIM, n_cmp), lambda bi, g, i: (bi, g, 0, 0))
    gw = NSA_REP * NSA_HEAD_DIM
    nq = s // tq
    return pl.pallas_call(
        functools.partial(_nsa_kernel, top_n=top_n),
        grid=(b, NSA_GROUPS, s // tq),
        in_specs=[pl.BlockSpec((gw, tq), lambda bi, g, i: (g, bi * nq + i)),
                  pl.BlockSpec((1, tq, LANES), lambda bi, g, i: (bi, i, g)),
                  pl.BlockSpec((1, NSA_HEAD_DIM, NSA_REP * tq), lambda bi, g, i: (g, 0, 0)),
                  ck_spec, cv_spec, k_spec(0), v_spec(0), k_spec(1), v_spec(1),
                  _const_spec(ovl.shape)],
        out_specs=pl.BlockSpec((gw, tq), lambda bi, g, i: (g, bi * nq + i)),
        out_shape=jax.ShapeDtypeStruct((NSA_GROUPS * gw, b * s), BF16),
        scratch_shapes=[pltpu.VMEM((n_cmp, 2 * NSA_HEAD_DIM), BF16),
                        pltpu.VMEM((s, 2 * NSA_HEAD_DIM + LANES), BF16),
                        pltpu.VMEM((s, 2 * NSA_HEAD_DIM), BF16),
                        pltpu.VMEM((NSA_HEAD_DIM + V_PAD_ROWS, s), BF16),
                        pltpu.VMEM((NSA_HEAD_DIM + V_PAD_ROWS, s), BF16),
                        pltpu.VMEM((SLC_CHUNK, NSA_REP * tq), F32),
                        pltpu.VMEM((n_slc, tq), F32),
                        pltpu.SMEM((s // SLC_CHUNK,), jnp.int32)],
        compiler_params=_params("parallel", "parallel", "arbitrary"),
    )(zqt, zg, slope_tab, ck, cv, zk, zvt, zk, zvt, ovl)


def _pad_cols(w, n):
    return jnp.pad(w, ((0, 0), (0, n - w.shape[1])))


def pool_delta_layer(x, ln, w_in, pool_w, pool_scale, conv_w, a_log, dt_bias, o_norm, w_out):
    b, s, d = x.shape
    main = POOL_WIDTH + 4 * DN_WIDTH
    w_all = jnp.concatenate([w_in[:, :main], _pad_cols(w_in[:, main:], LANES)], axis=1).astype(BF16)
    (z,) = norm_matmul(x.reshape(b * s, d), ln, [w_all], [F32])
    z = z.reshape(b, s, main + LANES)
    y_pool = pool_mixer(z, pool_w.astype(BF16), pool_scale)
    y_dn = gated_deltanet(z, main // LANES, conv_w, a_log, dt_bias, o_norm)
    w_out = w_out.astype(BF16)
    return [y_pool, y_dn], [w_out[:POOL_WIDTH], w_out[POOL_WIDTH:]]


def _cmp_weights(pe, w1, w2):
    half = (CMP_LEN // 2) * NSA_HEAD_DIM
    w1_pair = jnp.concatenate([w1[:half], w1[half:]], axis=1).astype(BF16)
    pe_rows = jnp.zeros((8, half), F32).at[0:2].set(pe.reshape(2, half)).astype(BF16)
    return w1_pair, pe_rows, w2.astype(BF16)


def nsa_layer(x, ln, w_in, pe_k, w1_k, w2_k, pe_v, w1_v, w2_v, w_out):
    b, s, d = x.shape
    g_, r_, hd = NSA_GROUPS, NSA_REP, NSA_HEAD_DIM
    kvw = g_ * hd
    main = d + 6 * kvw
    wg = w_in[:, main:].reshape(d, g_, r_ * 3)
    wg = jnp.pad(wg, ((0, 0), (0, 0), (0, LANES - r_ * 3))).reshape(d, g_ * LANES)
    cols = lambda j: w_in[:, d + j * kvw:d + (j + 1) * kvw]

    def lane_block_per_group(w):
        return jnp.pad(w.reshape(d, g_, hd), ((0, 0), (0, 0), (0, LANES - hd))).reshape(d, g_ * LANES)

    w_k = jnp.concatenate([lane_block_per_group(cols(2)), lane_block_per_group(cols(4))], axis=1)
    w_vt = jnp.concatenate([cols(3), cols(5)], axis=1).T
    z, zg, zk, zvt, zqt = norm_matmul(
        x.reshape(b * s, d), ln,
        [w_in[:, d:d + 2 * kvw].astype(BF16), wg.astype(BF16), w_k.astype(BF16)], [BF16, F32, BF16],
        wts=[w_vt.astype(BF16), w_in[:, :d].T.astype(BF16)])
    z = z.reshape(b, s, 2 * kvw)
    zg = zg.reshape(b, s, g_ * LANES)
    zk = zk.reshape(b, s, 2 * g_ * LANES)

    def group_major(j):
        return jnp.transpose(z[..., j * kvw:(j + 1) * kvw].reshape(b, s, g_, hd), (0, 2, 1, 3))

    n_str = s // CMP_STRIDE
    rk = group_major(0).reshape(b, g_, n_str, CMP_STRIDE * hd)
    rv = group_major(1).reshape(b, g_, n_str, CMP_STRIDE * hd)
    ck, cv = compress_kv(rk, rv, *_cmp_weights(pe_k, w1_k, w2_k), *_cmp_weights(pe_v, w1_v, w2_v))
    o_t = nsa_attention(zqt, zg, ck, jnp.swapaxes(cv, 2, 3), zk, zvt)
    return [o_t], [w_out.astype(BF16)]


def kernel(x, mem, a_ln, a_w_in, a_pool_w, a_pool_scale, a_conv_w, a_a_log, a_dt_bias, a_o_norm, a_w_out, c_ln, c_w_in, c_pe_k, c_w1_k, c_w2_k, c_pe_v, c_w1_v, c_w2_v, c_w_out, xa_ln, xa_mem_ln, xa_wq, xa_wk, xa_wv, xa_wo, ff_ln, ff_w1, ff_w2, final_ln):
    b, s, d = x.shape
    depth = xa_ln.shape[0]
    mem2d = mem.reshape(b * mem.shape[1], d)
    for l in range(depth):
        i = l // 2
        if l % 2 == 0:
            mix_a, mix_w = pool_delta_layer(x, a_ln[i], a_w_in[i], a_pool_w[i], a_pool_scale[i],
                                            a_conv_w[i], a_a_log[i], a_dt_bias[i], a_o_norm[i],
                                            a_w_out[i])
        else:
            mix_a, mix_w = nsa_layer(x, c_ln[i], c_w_in[i], c_pe_k[i], c_w1_k[i], c_w2_k[i],
                                     c_pe_v[i], c_w1_v[i], c_w2_v[i], c_w_out[i])
        mk, mv = norm_matmul(mem2d, xa_mem_ln[l], [xa_wk[l].astype(BF16), xa_wv[l].astype(BF16)],
                             [BF16, BF16])
        x = xattn_sublayer(x, mix_a, mix_w, xa_ln[l], xa_wq[l].astype(BF16), mk, mv,
                           xa_wo[l].astype(BF16))
        x = mlp_sublayer(x.reshape(b * s, d), ff_ln[l], ff_w1[l].astype(BF16), ff_w2[l].astype(BF16),
                         final_ln, final_norm=(l == depth - 1)).reshape(b, s, d)
    return x
```
